```python
import math
import jax
import jax.numpy as jnp
from jax import lax
import numpy as np

D_MODEL = 1024
BATCH = 2
SEQ = 8192
DEPTH = 2
DEC_BATCH = 16
DEC_SEQ = 32
PAST_LEN = 2048

CHUNK = 64
N_META = 16
Q_BLOCK = 128
EPS = 1e-6
N_EVEN = (DEPTH + 1) // 2
N_ODD = DEPTH // 2
D_FF = 4 * D_MODEL
DH = 64

D_S5 = D_MODEL // 2
S5_GROUP = 16
G_S5 = D_S5 // S5_GROUP
N_S5 = 64
H_FOX = D_MODEL // (2 * DH)
D_FOX = H_FOX * DH
D_SSM = D_MODEL // 2
P_SSM = 64
H_SSM = D_SSM // P_SSM
G_SSM = 2
N_SSM = 128
CONV_W = 4
CONV_DIM = D_SSM + 2 * G_SSM * N_SSM
H_SB = D_MODEL // (2 * DH)
D_SB = H_SB * DH

IN_AB = D_S5 + 3 * D_FOX + H_FOX
IN_CD = D_SSM + CONV_DIM + H_SSM + 3 * D_SB
MIX_AB = D_S5 + D_FOX
MIX_CD = D_SSM + D_SB

kernel_name = 'hybrid_streaming_encoder_step'


def rmsnorm(x, g):
    x32 = x.astype(jnp.float32)
    y = x32 * lax.rsqrt(jnp.mean(x32 * x32, axis=-1, keepdims=True) + EPS)
    return (y * g.astype(jnp.float32)).astype(x.dtype)


def swiglu(x, w_gate, w_up, w_down):
    return (jax.nn.silu(x @ w_gate) * (x @ w_up)) @ w_down


def ffn_half(h, g_pre, g_post, w_gate, w_up, w_down):
    return h + 0.5 * rmsnorm(swiglu(rmsnorm(h, g_pre), w_gate, w_up, w_down), g_post)


def sweep_query_blocks(block_fn, q_arrays, q_pos):
    lq = q_pos.shape[0]
    blk = min(Q_BLOCK, lq)
    nblk = -(-lq // blk)
    pad = nblk * blk - lq

    def to_blocks(a):
        a = jnp.pad(a, [(0, 0), (0, pad)] + [(0, 0)] * (a.ndim - 2))
        return jnp.moveaxis(a.reshape((a.shape[0], nblk, blk) + a.shape[2:]), 1, 0)

    blocks = tuple(to_blocks(a) for a in q_arrays)
    pos = jnp.pad(q_pos, (0, pad), mode='edge').reshape(nblk, blk)
    out = jnp.moveaxis(lax.map(block_fn, (blocks, pos)), 0, 1)
    out = out.reshape((out.shape[0], nblk * blk) + out.shape[3:])
    return out[:, :lq]


def fox_attention(q, k, v, f_q, f_k, q_pos, k_pos):
    scale = DH ** -0.5
    fk = jnp.swapaxes(f_k, 1, 2)

    def block(args):
        (qb, fqb), pb = args
        s = jnp.einsum('bqhd,bkhd->bhqk', qb, k, preferred_element_type=jnp.float32) * scale
        s = s + jnp.swapaxes(fqb, 1, 2)[..., None] - fk[:, :, None, :]
        mask = k_pos[None, :] <= pb[:, None]
        p = jax.nn.softmax(jnp.where(mask, s, -jnp.inf), axis=-1)
        return jnp.einsum('bhqk,bkhd->bqhd', p.astype(v.dtype), v)

    return sweep_query_blocks(block, (q, f_q), q_pos)


def stick_breaking_attention(q, k, v, q_pos, k_pos):
    scale = DH ** -0.5

    def block(args):
        (qb,), pb = args
        z = jnp.einsum('bqhd,bkhd->bhqk', qb, k, preferred_element_type=jnp.float32) * scale
        mask = k_pos[None, :] < pb[:, None]
        log_keep = jnp.where(mask, jax.nn.log_sigmoid(-z), 0.0)
        later = lax.cumsum(log_keep, axis=3, reverse=True) - log_keep
        w = jnp.where(mask, jnp.exp(jax.nn.log_sigmoid(z) + later), 0.0)
        return jnp.einsum('bhqk,bkhd->bqhd', w.astype(v.dtype), v)

    return sweep_query_blocks(block, (q,), q_pos)


def s5_mixer(u, h0_re, h0_im, a_re, a_im, log_dt, b_re, b_im, c_re, c_im, d_skip, w_glu, b_glu):
    f32 = jnp.float32
    bsz, L, _ = u.shape
    ug = u.reshape(bsz, L, G_S5, S5_GROUP).astype(f32)
    a_re = a_re.astype(f32)
    a_im = a_im.astype(f32)
    dt = jnp.exp(log_dt.astype(f32))[:, None]
    mag = jnp.exp(dt * a_re)
    ab_re = mag * jnp.cos(dt * a_im)
    ab_im = mag * jnp.sin(dt * a_im)
    den = a_re * a_re + a_im * a_im
    nr = ab_re - 1.0
    coef_re = (nr * a_re + ab_im * a_im) / den
    coef_im = (ab_im * a_re - nr * a_im) / den
    b_re = b_re.astype(f32)
    b_im = b_im.astype(f32)
    bb_re = coef_re[..., None] * b_re - coef_im[..., None] * b_im
    bb_im = coef_re[..., None] * b_im + coef_im[..., None] * b_re
    bu_re = jnp.einsum('gnc,blgc->blgn', bb_re, ug)
    bu_im = jnp.einsum('gnc,blgc->blgn', bb_im, ug)
    h0r = h0_re.astype(f32)
    h0i = h0_im.astype(f32)
    bu_re = bu_re.at[:, 0].add(ab_re * h0r - ab_im * h0i)
    bu_im = bu_im.at[:, 0].add(ab_re * h0i + ab_im * h0r)
    ar = jnp.broadcast_to(ab_re, bu_re.shape)
    ai = jnp.broadcast_to(ab_im, bu_im.shape)

    def combine(e1, e2):
        a1r, a1i, b1r, b1i = e1
        a2r, a2i, b2r, b2i = e2
        return (a1r * a2r - a1i * a2i, a1r * a2i + a1i * a2r,
                a2r * b1r - a2i * b1i + b2r, a2r * b1i + a2i * b1r + b2i)

    _, _, h_re, h_im = lax.associative_scan(combine, (ar, ai, bu_re, bu_im), axis=1)
    y = (jnp.einsum('gcn,blgn->blgc', c_re.astype(f32), h_re)
         - jnp.einsum('gcn,blgn->blgc', c_im.astype(f32), h_im)
         + d_skip.astype(f32) * ug).reshape(bsz, L, D_S5)
    g = jax.nn.gelu(y)
    out = g * jax.nn.sigmoid(g @ w_glu.astype(f32) + b_glu.astype(f32))
    return out.astype(u.dtype), h_re[:, -1], h_im[:, -1]


def ssd_scan(x, dt, a, bm, cm, h0):
    f32 = jnp.float32
    bsz, L = x.shape[:2]
    pad = (-L) % CHUNK

    def pad_left(t):
        return jnp.pad(t.astype(f32), [(0, 0), (pad, 0)] + [(0, 0)] * (t.ndim - 2))

    x, dt, bm, cm = pad_left(x), pad_left(dt), pad_left(bm), pad_left(cm)
    nc = (L + pad) // CHUNK
    e = H_SSM // G_SSM
    xc = x.reshape(bsz, nc, CHUNK, G_SSM, e, P_SSM)
    dtc = dt.reshape(bsz, nc, CHUNK, G_SSM, e)
    bc = bm.reshape(bsz, nc, CHUNK, G_SSM, N_SSM)
    cc = cm.reshape(bsz, nc, CHUNK, G_SSM, N_SSM)
    adt = dtc * a.reshape(G_SSM, e)
    xdt = xc * dtc[..., None]
    acs = jnp.cumsum(adt, axis=2)
    seg = acs[:, :, :, None] - acs[:, :, None, :]
    tri = jnp.tril(jnp.ones((CHUNK, CHUNK), bool))[:, :, None, None]
    lmat = jnp.where(tri, jnp.exp(jnp.where(tri, seg, 0.0)), 0.0)
    cb = jnp.einsum('bctgn,bcsgn->bctsg', cc, bc)
    y_diag = jnp.einsum('bctsg,bctsge,bcsgep->bctgep', cb, lmat, xdt)
    decay_states = jnp.exp(acs[:, :, -1:] - acs)
    chunk_states = jnp.einsum('bclgn,bclge,bclgep->bcgepn', bc, decay_states, xdt)
    chunk_decay = jnp.exp(acs[:, :, -1])

    def step(h, inp):
        s_c, d_c = inp
        return h * d_c[..., None, None] + s_c, h

    h0g = h0.astype(f32).reshape(bsz, G_SSM, e, P_SSM, N_SSM)
    h_final, h_prev = lax.scan(step, h0g, (jnp.moveaxis(chunk_states, 1, 0), jnp.moveaxis(chunk_decay, 1, 0)))
    h_prev = jnp.moveaxis(h_prev, 0, 1)
    y_off = jnp.einsum('bclgn,bcgepn,bclge->bclgep', cc, h_prev, jnp.exp(acs))
    y = (y_diag + y_off).reshape(bsz, nc * CHUNK, H_SSM, P_SSM)[:, pad:]
    return y, h_final.reshape(bsz, H_SSM, P_SSM, N_SSM)


def ssd_mixer(z, xbc, dt_raw, conv_buf, h0, conv_w, conv_b, dt_bias, a_log, d_skip, norm_g):
    f32 = jnp.float32
    bsz, L, _ = xbc.shape
    xp = jnp.concatenate([conv_buf.astype(xbc.dtype), xbc], axis=1)
    conv = sum((xp[:, w:w + L] * conv_w[w] for w in range(CONV_W)), conv_b)
    new_buf = xp[:, L:]
    act = jax.nn.silu(conv)
    xs = act[..., :D_SSM].reshape(bsz, L, H_SSM, P_SSM)
    bm = act[..., D_SSM:D_SSM + G_SSM * N_SSM].reshape(bsz, L, G_SSM, N_SSM)
    cm = act[..., D_SSM + G_SSM * N_SSM:].reshape(bsz, L, G_SSM, N_SSM)
    dt = jax.nn.softplus(dt_raw.astype(f32) + dt_bias.astype(f32))
    a = -jnp.exp(a_log.astype(f32))
    y, h_new = ssd_scan(xs, dt, a, bm, cm, h0)
    y = y + d_skip.astype(f32)[:, None] * xs.astype(f32)
    y = y.reshape(bsz, L, D_SSM) * jax.nn.silu(z.astype(f32))
    yg = y.reshape(bsz, L, G_SSM, D_SSM // G_SSM)
    yg = yg * lax.rsqrt(jnp.mean(yg * yg, axis=-1, keepdims=True) + EPS)
    y = yg.reshape(bsz, L, D_SSM) * norm_g.astype(f32)
    return y.astype(z.dtype), h_new, new_buf


def mixer_ab(xn, s5_re0, s5_im0, past_k, past_v, past_logf, w_in, b_f, a_re, a_im, log_dt,
             b_re, b_im, c_re, c_im, d_skip, w_glu, b_glu, w_out):
    bsz, L, _ = xn.shape
    n_past = past_k.shape[1]
    proj = xn @ w_in
    u = proj[..., :D_S5]
    q, k, v = jnp.split(proj[..., D_S5:D_S5 + 3 * D_FOX], 3, axis=-1)
    q = q.reshape(bsz, L, H_FOX, DH)
    k = k.reshape(bsz, L, H_FOX, DH)
    v = v.reshape(bsz, L, H_FOX, DH)
    logf = jax.nn.log_sigmoid((proj[..., D_S5 + 3 * D_FOX:] + b_f).astype(jnp.float32))
    s5_out, s5_re, s5_im = s5_mixer(u, s5_re0, s5_im0, a_re, a_im, log_dt, b_re, b_im, c_re, c_im,
                                    d_skip, w_glu, b_glu)
    k_all = jnp.concatenate([past_k.astype(k.dtype), k], axis=1)
    v_all = jnp.concatenate([past_v.astype(v.dtype), v], axis=1)
    f_cum = jnp.cumsum(jnp.concatenate([past_logf.astype(jnp.float32), logf], axis=1), axis=1)
    k_pos = jnp.arange(n_past + L)
    fox = fox_attention(q, k_all, v_all, f_cum[:, n_past:], f_cum, k_pos[n_past:], k_pos)
    mixed = jnp.concatenate([s5_out, fox.reshape(bsz, L, D_FOX).astype(s5_out.dtype)], axis=-1)
    return mixed @ w_out, s5_re, s5_im, k, v, logf


def mixer_cd(xn, ssd_h0, conv_buf, past_k, past_v, w_in, conv_w, conv_b, dt_bias, a_log, d_skip,
             norm_g, w_out):
    bsz, L, _ = xn.shape
    n_past = past_k.shape[1]
    proj = xn @ w_in
    z = proj[..., :D_SSM]
    xbc = proj[..., D_SSM:D_SSM + CONV_DIM]
    dt_raw = proj[..., D_SSM + CONV_DIM:D_SSM + CONV_DIM + H_SSM]
    q, k, v = jnp.split(proj[..., D_SSM + CONV_DIM + H_SSM:], 3, axis=-1)
    q = q.reshape(bsz, L, H_SB, DH)
    k = k.reshape(bsz, L, H_SB, DH)
    v = v.reshape(bsz, L, H_SB, DH)
    ssd_out, ssd_h, new_buf = ssd_mixer(z, xbc, dt_raw, conv_buf, ssd_h0, conv_w, conv_b, dt_bias,
                                        a_log, d_skip, norm_g)
    k_all = jnp.concatenate([past_k.astype(k.dtype), k], axis=1)
    v_all = jnp.concatenate([past_v.astype(v.dtype), v], axis=1)
    k_pos = jnp.arange(n_past + L)
    sb = stick_breaking_attention(q, k_all, v_all, k_pos[n_past:], k_pos)
    mixed = jnp.concatenate([ssd_out, sb.reshape(bsz, L, D_SB).astype(ssd_out.dtype)], axis=-1)
    return mixed @ w_out, ssd_h, new_buf, k, v


def trunk(h, s5_re, s5_im, fox_k, fox_v, fox_logf, ssd_h, conv_buf, sb_k, sb_v,
          norm_ffn1_pre, norm_ffn1_post, norm_mix_pre, norm_mix_post, norm_ffn2_pre, norm_ffn2_post,
          ffn1_w_gate, ffn1_w_up, ffn1_w_down, ffn2_w_gate, ffn2_w_up, ffn2_w_down,
          ab_w_in, fox_b_f, s5_a_re, s5_a_im, s5_log_dt, s5_b_re, s5_b_im, s5_c_re, s5_c_im, s5_d,
          s5_w_glu, s5_b_glu, ab_w_out,
          cd_w_in, ssd_conv_w, ssd_conv_b, ssd_dt_bias, ssd_a_log, ssd_d, ssd_norm, cd_w_out):
    ab_states, cd_states = [], []
    for i in range(DEPTH):
        j = i // 2
        h = ffn_half(h, norm_ffn1_pre[i], norm_ffn1_post[i], ffn1_w_gate[i], ffn1_w_up[i], ffn1_w_down[i])
        xn = rmsnorm(h, norm_mix_pre[i])
        if i % 2 == 0:
            out, *st = mixer_ab(xn, s5_re[j], s5_im[j], fox_k[j], fox_v[j], fox_logf[j], ab_w_in[j],
                                fox_b_f[j], s5_a_re[j], s5_a_im[j], s5_log_dt[j], s5_b_re[j], s5_b_im[j],
                                s5_c_re[j], s5_c_im[j], s5_d[j], s5_w_glu[j], s5_b_glu[j], ab_w_out[j])
            ab_states.append(st)
        else:
            out, *st = mixer_cd(xn, ssd_h[j], conv_buf[j], sb_k[j], sb_v[j], cd_w_in[j], ssd_conv_w[j],
                                ssd_conv_b[j], ssd_dt_bias[j], ssd_a_log[j], ssd_d[j], ssd_norm[j],
                                cd_w_out[j])
            cd_states.append(st)
        h = h + rmsnorm(out, norm_mix_post[i])
        h = ffn_half(h, norm_ffn2_pre[i], norm_ffn2_post[i], ffn2_w_gate[i], ffn2_w_up[i], ffn2_w_down[i])
    ab = [jnp.stack(z) for z in zip(*ab_states)]
    cd = [jnp.stack(z) for z in zip(*cd_states)]
    return (h, *ab, *cd)


def setup_inputs(seed: int = 0) -> dict:
    key = jax.random.key(seed)
    ks = iter(jax.random.split(key, 64))
    f32 = jnp.float32

    def nrm(shape, scale=1.0):
        return scale * jax.random.normal(next(ks), shape, f32)

    def gain(shape):
        return 1.0 + nrm(shape, 0.05)

    def log_uniform(shape, lo, hi):
        return jax.random.uniform(next(ks), shape, f32, math.log(lo), math.log(hi))

    dt_init = jnp.exp(log_uniform((N_ODD, H_SSM), 1e-3, 1e-1))
    return {
        'x_prompt': nrm((BATCH, SEQ, D_MODEL)),
        'x_sample': nrm((DEC_BATCH, DEC_SEQ, D_MODEL)),
        'state_s5_re': nrm((N_EVEN, DEC_BATCH, G_S5, N_S5), 0.3),
        'state_s5_im': nrm((N_EVEN, DEC_BATCH, G_S5, N_S5), 0.3),
        'cache_fox_k': nrm((N_EVEN, DEC_BATCH, PAST_LEN, H_FOX, DH)),
        'cache_fox_v': nrm((N_EVEN, DEC_BATCH, PAST_LEN, H_FOX, DH)),
        'cache_fox_logf': jax.nn.log_sigmoid(2.0 + nrm((N_EVEN, DEC_BATCH, PAST_LEN, H_FOX))),
        'state_ssd': nrm((N_ODD, DEC_BATCH, H_SSM, P_SSM, N_SSM), 0.1),
        'state_conv': nrm((N_ODD, DEC_BATCH, CONV_W - 1, CONV_DIM)),
        'cache_sb_k': nrm((N_ODD, DEC_BATCH, PAST_LEN, H_SB, DH)),
        'cache_sb_v': nrm((N_ODD, DEC_BATCH, PAST_LEN, H_SB, DH)),
        'meta_tokens': nrm((N_META, D_MODEL)),
        'norm_ffn1_pre': gain((DEPTH, D_MODEL)),
        'norm_ffn1_post': gain((DEPTH, D_MODEL)),
        'norm_mix_pre': gain((DEPTH, D_MODEL)),
        'norm_mix_post': gain((DEPTH, D_MODEL)),
        'norm_ffn2_pre': gain((DEPTH, D_MODEL)),
        'norm_ffn2_post': gain((DEPTH, D_MODEL)),
        'ffn1_w_gate': nrm((DEPTH, D_MODEL, D_FF), D_MODEL ** -0.5),
        'ffn1_w_up': nrm((DEPTH, D_MODEL, D_FF), D_MODEL ** -0.5),
        'ffn1_w_down': nrm((DEPTH, D_FF, D_MODEL), D_FF ** -0.5),
        'ffn2_w_gate': nrm((DEPTH, D_MODEL, D_FF), D_MODEL ** -0.5),
        'ffn2_w_up': nrm((DEPTH, D_MODEL, D_FF), D_MODEL ** -0.5),
        'ffn2_w_down': nrm((DEPTH, D_FF, D_MODEL), D_FF ** -0.5),
        'ab_w_in': nrm((N_EVEN, D_MODEL, IN_AB), D_MODEL ** -0.5),
        'fox_b_f': 2.0 + nrm((N_EVEN, H_FOX), 0.1),
        's5_a_re': -0.5 + nrm((N_EVEN, G_S5, N_S5), 0.01),
        's5_a_im': jnp.pi * jnp.arange(N_S5, dtype=f32) + nrm((N_EVEN, G_S5, N_S5), 0.01),
        's5_log_dt': log_uniform((N_EVEN, G_S5), 1e-3, 1e-1),
        's5_b_re': nrm((N_EVEN, G_S5, N_S5, S5_GROUP), (2 * S5_GROUP) ** -0.5),
        's5_b_im': nrm((N_EVEN, G_S5, N_S5, S5_GROUP), (2 * S5_GROUP) ** -0.5),
        's5_c_re': nrm((N_EVEN, G_S5, S5_GROUP, N_S5), N_S5 ** -0.5),
        's5_c_im': nrm((N_EVEN, G_S5, S5_GROUP, N_S5), N_S5 ** -0.5),
        's5_d': nrm((N_EVEN, G_S5, S5_GROUP)),
        's5_w_glu': nrm((N_EVEN, D_S5, D_S5), D_S5 ** -0.5),
        's5_b_glu': nrm((N_EVEN, D_S5), 0.01),
        'ab_w_out': nrm((N_EVEN, MIX_AB, D_MODEL), MIX_AB ** -0.5),
        'cd_w_in': nrm((N_ODD, D_MODEL, IN_CD), D_MODEL ** -0.5),
        'ssd_conv_w': nrm((N_ODD, CONV_W, CONV_DIM), CONV_W ** -0.5),
        'ssd_conv_b': nrm((N_ODD, CONV_DIM), 0.01),
        'ssd_dt_bias': dt_init + jnp.log(-jnp.expm1(-dt_init)),
        'ssd_a_log': jnp.log(jax.random.uniform(next(ks), (N_ODD, H_SSM), f32, 1.0, 16.0)),
        'ssd_d': 1.0 + nrm((N_ODD, H_SSM), 0.1),
        'ssd_norm': gain((N_ODD, D_SSM)),
        'cd_w_out': nrm((N_ODD, MIX_CD, D_MODEL), MIX_CD ** -0.5),
    }


def reference(x_prompt, x_sample, state_s5_re, state_s5_im, cache_fox_k, cache_fox_v, cache_fox_logf,
              state_ssd, state_conv, cache_sb_k, cache_sb_v, meta_tokens,
              norm_ffn1_pre, norm_ffn1_post, norm_mix_pre, norm_mix_post, norm_ffn2_pre, norm_ffn2_post,
              ffn1_w_gate, ffn1_w_up, ffn1_w_down, ffn2_w_gate, ffn2_w_up, ffn2_w_down,
              ab_w_in, fox_b_f, s5_a_re, s5_a_im, s5_log_dt, s5_b_re, s5_b_im, s5_c_re, s5_c_im, s5_d,
              s5_w_glu, s5_b_glu, ab_w_out,
              cd_w_in, ssd_conv_w, ssd_conv_b, ssd_dt_bias, ssd_a_log, ssd_d, ssd_norm, cd_w_out):
    weights = (norm_ffn1_pre, norm_ffn1_post, norm_mix_pre, norm_mix_post, norm_ffn2_pre, norm_ffn2_post,
               ffn1_w_gate, ffn1_w_up, ffn1_w_down, ffn2_w_gate, ffn2_w_up, ffn2_w_down,
               ab_w_in, fox_b_f, s5_a_re, s5_a_im, s5_log_dt, s5_b_re, s5_b_im, s5_c_re, s5_c_im, s5_d,
               s5_w_glu, s5_b_glu, ab_w_out,
               cd_w_in, ssd_conv_w, ssd_conv_b, ssd_dt_bias, ssd_a_log, ssd_d, ssd_norm, cd_w_out)
    bp = x_prompt.shape[0]
    dtp = x_prompt.dtype
    meta = jnp.broadcast_to(meta_tokens.astype(dtp)[None], (bp, N_META, D_MODEL))
    h_p = jnp.concatenate([meta, x_prompt], axis=1)
    (h_p, s5_re_p, s5_im_p, fox_k_p, fox_v_p, fox_logf_p, ssd_p, conv_p, sb_k_p, sb_v_p) = trunk(
        h_p,
        jnp.zeros((N_EVEN, bp, G_S5, N_S5), jnp.float32),
        jnp.zeros((N_EVEN, bp, G_S5, N_S5), jnp.float32),
        jnp.zeros((N_EVEN, bp, 0, H_FOX, DH), dtp),
        jnp.zeros((N_EVEN, bp, 0, H_FOX, DH), dtp),
        jnp.zeros((N_EVEN, bp, 0, H_FOX), jnp.float32),
        jnp.zeros((N_ODD, bp, H_SSM, P_SSM, N_SSM), jnp.float32),
        jnp.zeros((N_ODD, bp, CONV_W - 1, CONV_DIM), dtp),
        jnp.zeros((N_ODD, bp, 0, H_SB, DH), dtp),
        jnp.zeros((N_ODD, bp, 0, H_SB, DH), dtp),
        *weights)
    y_prompt = h_p[:, N_META:]
    (y_sample, s5_re_s, s5_im_s, fox_k_s, fox_v_s, fox_logf_s, ssd_s, conv_s, sb_k_s, sb_v_s) = trunk(
        x_sample, state_s5_re, state_s5_im, cache_fox_k, cache_fox_v, cache_fox_logf,
        state_ssd, state_conv, cache_sb_k, cache_sb_v, *weights)
    return (y_prompt, y_sample,
            s5_re_p, s5_im_p, fox_k_p, fox_v_p, fox_logf_p, ssd_p, conv_p, sb_k_p, sb_v_p,
            s5_re_s, s5_im_s, fox_k_s, fox_v_s, fox_logf_s, ssd_s, conv_s, sb_k_s, sb_v_s)
```

```python
import functools

import jax
import jax.numpy as jnp
from jax import lax
from jax.experimental import pallas as pl
from jax.experimental.pallas import tpu as pltpu

F32 = jnp.float32
BF16 = jnp.bfloat16

EPS = 1e-6
D_MODEL = 1024
DH = 64
N_META = 16
D_S5 = 512
S5_GROUP = 16
G_S5 = 32
N_S5 = 64
S5_W = G_S5 * N_S5
H_ATT = 8
D_ATT = H_ATT * DH
D_SSM = 512
P_SSM = 64
H_SSM = 8
G_SSM = 2
N_SSM = 128
CONV_W = 4
CONV_DIM = D_SSM + 2 * G_SSM * N_SSM

LANES = 128
SUBLANES = 8
SEQ_TILE = 256
PROJ_TN = 256
VMEM_LIMIT = 56 * 1024 * 1024

NT_DIMS = (((1,), (1,)), ((), ()))


def _cparams(sem):
    return pltpu.CompilerParams(dimension_semantics=sem, vmem_limit_bytes=VMEM_LIMIT)


def _rms(x, g):
    return x * lax.rsqrt(jnp.mean(x * x, axis=-1, keepdims=True) + EPS) * g


def _sigmoid(x):
    return 1.0 / (1.0 + jnp.exp(-x))


def _softplus(x):
    return jnp.maximum(x, 0.0) + jnp.log1p(jnp.exp(-jnp.abs(x)))


def _ones_where(mask):
    return jnp.where(mask, 1.0, 0.0).astype(BF16)


def _split3(x):
    hi = x.astype(BF16)
    r1 = x - hi.astype(F32)
    mid = r1.astype(BF16)
    lo = (r1 - mid.astype(F32)).astype(BF16)
    return hi, mid, lo


def _split3_dot(tri, x):
    return sum(jnp.dot(tri, part, preferred_element_type=F32) for part in _split3(x))


def _split3_transpose(eye, x):
    return sum(lax.dot_general(eye, part, NT_DIMS, preferred_element_type=F32) for part in _split3(x))


def _ffn_kernel(h_ref, gpre_ref, gpost_ref, wg_ref, wu_ref, wd_ref, o_ref, xn_ref, acc_ref, *, nj):
    j = pl.program_id(1)

    @pl.when(j == 0)
    def _():
        xn_ref[...] = _rms(h_ref[...], gpre_ref[...]).astype(BF16)
        acc_ref[...] = jnp.zeros_like(acc_ref)

    xn = xn_ref[...]
    g = jnp.dot(xn, wg_ref[...], preferred_element_type=F32)
    u = jnp.dot(xn, wu_ref[...], preferred_element_type=F32)
    a = (g * _sigmoid(g)) * u
    acc_ref[...] += jnp.dot(a.astype(BF16), wd_ref[...], preferred_element_type=F32)

    @pl.when(j == nj - 1)
    def _():
        o_ref[...] = h_ref[...] + 0.5 * _rms(acc_ref[...], gpost_ref[...])


def _ffn_half(h, gpre, gpost, wg, wu, wd, layer, tm, tf):
    t, d = h.shape
    ff = wg.shape[-1]
    nj = ff // tf
    return pl.pallas_call(
        functools.partial(_ffn_kernel, nj=nj),
        name="ffn_half",
        grid=(t // tm, nj),
        in_specs=[
            pl.BlockSpec((tm, d), lambda i, j: (i, 0)),
            pl.BlockSpec((None, 1, d), lambda i, j: (layer, 0, 0)),
            pl.BlockSpec((None, 1, d), lambda i, j: (layer, 0, 0)),
            pl.BlockSpec((None, d, tf), lambda i, j: (layer, 0, j)),
            pl.BlockSpec((None, d, tf), lambda i, j: (layer, 0, j)),
            pl.BlockSpec((None, tf, d), lambda i, j: (layer, j, 0)),
        ],
        out_specs=pl.BlockSpec((tm, d), lambda i, j: (i, 0)),
        out_shape=jax.ShapeDtypeStruct((t, d), F32),
        scratch_shapes=[pltpu.VMEM((tm, d), BF16), pltpu.VMEM((tm, d), F32)],
        compiler_params=_cparams(("parallel", "arbitrary")),
    )(h, gpre, gpost, wg, wu, wd)


def _inproj_kernel(h_ref, g_ref, w_ref, b_ref, o32_ref, o16_ref, xn_ref, *, nj, tail):
    j = pl.program_id(1)

    @pl.when(j == 0)
    def _():
        xn_ref[...] = _rms(h_ref[...], g_ref[...]).astype(BF16)

    p = jnp.dot(xn_ref[...], w_ref[...], preferred_element_type=F32)

    @pl.when(j < nj - 1)
    def _():
        o32_ref[...] = p
        o16_ref[...] = p.astype(BF16)

    @pl.when(j == nj - 1)
    def _():
        x = p + b_ref[...]
        r = -_softplus(-x) if tail == "log_sigmoid" else _softplus(x)
        o32_ref[...] = r
        o16_ref[...] = r.astype(BF16)


def _inproj(h, g, w, bias, layer, tm, tail):
    t, d = h.shape
    n = w.shape[-1]
    nj = n // PROJ_TN
    return pl.pallas_call(
        functools.partial(_inproj_kernel, nj=nj, tail=tail),
        name="mix_inproj",
        grid=(t // tm, nj),
        in_specs=[
            pl.BlockSpec((tm, d), lambda i, j: (i, 0)),
            pl.BlockSpec((None, 1, d), lambda i, j: (layer, 0, 0)),
            pl.BlockSpec((d, PROJ_TN), lambda i, j: (0, j)),
            pl.BlockSpec((1, PROJ_TN), lambda i, j: (0, j)),
        ],
        out_specs=[pl.BlockSpec((tm, PROJ_TN), lambda i, j: (i, j)),
                   pl.BlockSpec((tm, PROJ_TN), lambda i, j: (i, j))],
        out_shape=[jax.ShapeDtypeStruct((t, n), F32), jax.ShapeDtypeStruct((t, n), BF16)],
        scratch_shapes=[pltpu.VMEM((tm, d), BF16)],
        compiler_params=_cparams(("parallel", "arbitrary")),
    )(h, g, w, bias)


def _gelu_tanh(x):
    return 0.5 * x * (1.0 + jnp.tanh(0.7978845608028654 * (x + 0.044715 * (x * x * x))))


def _mixout_kernel(*refs, s5_glu):
    if s5_glu:
        (ya_ref, u_ref, dsk_ref, wglu_ref, bglu_ref, yb_ref, wo_ref, h_ref, gpost_ref, o_ref) = refs
        y = ya_ref[...] + dsk_ref[...] * u_ref[...]
        g = _gelu_tanh(y)
        gate = _sigmoid(jnp.dot(g.astype(BF16), wglu_ref[...], preferred_element_type=F32) + bglu_ref[...])
        a = (g * gate).astype(BF16)
    else:
        (ya_ref, yb_ref, wo_ref, h_ref, gpost_ref, o_ref) = refs
        a = ya_ref[...]
    half = a.shape[-1]
    out = (jnp.dot(a, wo_ref[:half, :], preferred_element_type=F32)
           + jnp.dot(yb_ref[...], wo_ref[half:, :], preferred_element_type=F32))
    o_ref[...] = h_ref[...] + _rms(out, gpost_ref[...])


def _mixout(h, gpost, layer, wo, yb, tm, *, ya, u=None, dsk=None, wglu=None, bglu=None):
    t, d = h.shape
    half = yb.shape[-1]
    s5_glu = u is not None
    row = lambda i: (i, 0)
    const = lambda i: (0, 0)
    args, specs = [ya], [pl.BlockSpec((tm, half), row)]
    if s5_glu:
        args += [u, dsk, wglu, bglu]
        specs += [pl.BlockSpec((tm, half), row), pl.BlockSpec((1, half), const),
                  pl.BlockSpec((half, half), const), pl.BlockSpec((1, half), const)]
    args += [yb, wo, h, gpost]
    specs += [pl.BlockSpec((tm, half), row), pl.BlockSpec((2 * half, d), const),
              pl.BlockSpec((tm, d), row), pl.BlockSpec((None, 1, d), lambda i: (layer, 0, 0))]
    return pl.pallas_call(
        functools.partial(_mixout_kernel, s5_glu=s5_glu),
        name="mix_out",
        grid=(t // tm,),
        in_specs=specs,
        out_specs=pl.BlockSpec((tm, d), row),
        out_shape=jax.ShapeDtypeStruct((t, d), F32),
        compiler_params=_cparams(("parallel",)),
    )(*args)


def _s5_prep_kernel(are_ref, aim_ref, ldt_ref, bre_ref, bim_ref, cre_ref, cim_ref,
                    tab_ref, bf_ref, cf_ref):
    a_re = are_ref[...]
    a_im = aim_ref[...]
    dt = jnp.exp(ldt_ref[...])
    mag = jnp.exp(dt * a_re)
    ab_re = mag * jnp.cos(dt * a_im)
    ab_im = mag * jnp.sin(dt * a_im)
    den = a_re * a_re + a_im * a_im
    nr = ab_re - 1.0
    coef_re = (nr * a_re + ab_im * a_im) / den
    coef_im = (ab_im * a_re - nr * a_im) / den

    pw_re, pw_im = [ab_re], [ab_im]
    for _ in range(SUBLANES - 1):
        pr, pi = pw_re[-1], pw_im[-1]
        pw_re.append(pr * ab_re - pi * ab_im)
        pw_im.append(pr * ab_im + pi * ab_re)
    sub = lax.broadcasted_iota(jnp.int32, (SUBLANES, S5_W), 0)
    for k in range(3):
        sh = 1 << k
        tab_ref[k] = jnp.where(sub >= sh, pw_re[sh - 1], 0.0)
        tab_ref[3 + k] = jnp.where(sub >= sh, pw_im[sh - 1], 0.0)
    q_re = jnp.zeros((SUBLANES, S5_W), F32)
    q_im = jnp.zeros((SUBLANES, S5_W), F32)
    for r in range(SUBLANES):
        q_re = jnp.where(sub == r, pw_re[r], q_re)
        q_im = jnp.where(sub == r, pw_im[r], q_im)
    tab_ref[6] = q_re
    tab_ref[7] = q_im

    rg = lax.broadcasted_iota(jnp.int32, (D_S5, S5_W), 0) // S5_GROUP
    cg = lax.broadcasted_iota(jnp.int32, (D_S5, S5_W), 1) // N_S5
    b_re = bre_ref[...]
    b_im = bim_ref[...]
    bb_re = coef_re * b_re - coef_im * b_im
    bb_im = coef_re * b_im + coef_im * b_re
    bf_ref[:, :S5_W] = jnp.where(rg == cg, bb_re, 0.0).astype(BF16)
    bf_ref[:, S5_W:] = jnp.where(rg == cg, bb_im, 0.0).astype(BF16)
    rg2 = lax.broadcasted_iota(jnp.int32, (S5_W, D_S5), 0) // N_S5
    cg2 = lax.broadcasted_iota(jnp.int32, (S5_W, D_S5), 1) // S5_GROUP
    cf_ref[:S5_W, :] = jnp.where(rg2 == cg2, cre_ref[...], 0.0).astype(BF16)
    cf_ref[S5_W:, :] = jnp.where(rg2 == cg2, -cim_ref[...], 0.0).astype(BF16)


def _s5_prep(a_re, a_im, log_dt, b_re, b_im, c_re, c_im):
    flat = lambda x: x.reshape(1, S5_W)
    ldt = jnp.broadcast_to(log_dt[:, None], (G_S5, N_S5))
    b_t = lambda b: jnp.tile(jnp.transpose(b, (2, 0, 1)).reshape(S5_GROUP, S5_W), (G_S5, 1))
    c_t = lambda c: jnp.tile(jnp.transpose(c, (0, 2, 1)).reshape(S5_W, S5_GROUP), (1, G_S5))
    return pl.pallas_call(
        _s5_prep_kernel,
        name="s5_prep",
        out_shape=[jax.ShapeDtypeStruct((8, SUBLANES, S5_W), F32),
                   jax.ShapeDtypeStruct((D_S5, 2 * S5_W), BF16),
                   jax.ShapeDtypeStruct((2 * S5_W, D_S5), BF16)],
        compiler_params=pltpu.CompilerParams(vmem_limit_bytes=VMEM_LIMIT),
    )(flat(a_re), flat(a_im), flat(ldt), b_t(b_re), b_t(b_im), c_t(c_re), c_t(c_im))


S5_CB = 256


def _s5_kernel(u_ref, bf_ref, cf_ref, tab_ref, h0_ref, y_ref, st_ref, bu_ref, car_ref,
               *, rows, seg, chain, last_tile, last_row):
    i = pl.program_id(1)
    w = S5_W
    bu_ref[...] = jnp.dot(u_ref[...], bf_ref[...], preferred_element_type=F32)
    if chain:
        @pl.when(i == 0)
        def _():
            car_ref[...] = h0_ref[...]

    for c in range(0, w, S5_CB):
        re_cols = slice(c, c + S5_CB)
        im_cols = slice(w + c, w + c + S5_CB)
        pr = [tab_ref[k, :, re_cols] for k in range(3)]
        pi = [tab_ref[3 + k, :, re_cols] for k in range(3)]
        qr = tab_ref[6, :, re_cols]
        qi = tab_ref[7, :, re_cols]
        for sg in range(rows // seg):
            src = car_ref if chain else h0_ref
            srow = 0 if chain else sg
            cr0 = src[srow:srow + 1, re_cols]
            ci0 = src[srow:srow + 1, im_cols]

            def body(a, carry, sg=sg, re_cols=re_cols, im_cols=im_cols, pr=pr, pi=pi, qr=qr, qi=qi):
                cr, ci = carry
                r0 = pl.multiple_of(sg * seg + a * SUBLANES, SUBLANES)
                xr = bu_ref[pl.ds(r0, SUBLANES), re_cols]
                xi = bu_ref[pl.ds(r0, SUBLANES), im_cols]
                for k in range(3):
                    sr = pltpu.roll(xr, 1 << k, 0)
                    si = pltpu.roll(xi, 1 << k, 0)
                    xr, xi = xr + pr[k] * sr - pi[k] * si, xi + pr[k] * si + pi[k] * sr
                xr, xi = xr + qr * cr - qi * ci, xi + qr * ci + qi * cr
                bu_ref[pl.ds(r0, SUBLANES), re_cols] = xr
                bu_ref[pl.ds(r0, SUBLANES), im_cols] = xi
                return xr[SUBLANES - 1:SUBLANES, :], xi[SUBLANES - 1:SUBLANES, :]

            cr, ci = lax.fori_loop(0, seg // SUBLANES, body, (cr0, ci0))
            if chain:
                car_ref[0:1, re_cols] = cr
                car_ref[0:1, im_cols] = ci
            else:
                st_ref[sg:sg + 1, re_cols] = cr
                st_ref[sg:sg + 1, im_cols] = ci

    y_ref[...] = jnp.dot(bu_ref[...].astype(BF16), cf_ref[...], preferred_element_type=F32)
    if chain:
        @pl.when(i == last_tile)
        def _():
            st_ref[...] = bu_ref[last_row:last_row + 1, :]


def _s5_scan(proj16, bfull, cfull, tab, h0, *, row_block0, n_groups, n_tiles, seg, chain, l_valid):
    rows = SEQ_TILE
    s = h0.shape[1]
    last = l_valid - 1
    kern = functools.partial(_s5_kernel, rows=rows, seg=seg, chain=chain,
                             last_tile=last // rows, last_row=last % rows)
    if chain:
        umap = lambda b, i: (row_block0 + b * n_tiles + i, 0)
        hmap = lambda b, i: (b, 0, 0)
    else:
        umap = lambda b, i: (row_block0 + i, 0)
        hmap = lambda b, i: (i, 0, 0)
    const2 = lambda b, i: (0, 0)
    return pl.pallas_call(
        kern,
        name="s5_scan",
        grid=(n_groups, n_tiles),
        in_specs=[
            pl.BlockSpec((rows, D_S5), umap),
            pl.BlockSpec((D_S5, 2 * S5_W), const2),
            pl.BlockSpec((2 * S5_W, D_S5), const2),
            pl.BlockSpec((8, SUBLANES, S5_W), lambda b, i: (0, 0, 0)),
            pl.BlockSpec((None, s, 2 * S5_W), hmap),
        ],
        out_specs=[pl.BlockSpec((rows, D_S5), lambda b, i: (b * n_tiles + i, 0)),
                   pl.BlockSpec((None, s, 2 * S5_W), hmap)],
        out_shape=[jax.ShapeDtypeStruct((n_groups * n_tiles * rows, D_S5), F32),
                   jax.ShapeDtypeStruct(h0.shape, F32)],
        scratch_shapes=[pltpu.VMEM((rows, 2 * S5_W), F32), pltpu.VMEM((1, 2 * S5_W), F32)],
        compiler_params=_cparams(("arbitrary", "arbitrary")),
    )(proj16, bfull, cfull, tab, h0)


def _cumsum_kernel(x_ref, o_ref, car_ref, *, tk):
    j = pl.program_id(1)

    @pl.when(j == 0)
    def _():
        car_ref[...] = jnp.zeros_like(car_ref)

    r = lax.broadcasted_iota(jnp.int32, (tk, tk), 0)
    c = lax.broadcasted_iota(jnp.int32, (tk, tk), 1)
    tri = _ones_where(r <= c)
    acc = sum(jnp.dot(part, tri, preferred_element_type=F32) for part in _split3(x_ref[...])) + car_ref[...]
    o_ref[...] = acc
    car_ref[...] = acc[:, tk - 1:tk]


def _time_cumsum(x):
    b, hh, length = x.shape
    tk = SEQ_TILE
    return pl.pallas_call(
        functools.partial(_cumsum_kernel, tk=tk),
        name="time_cumsum",
        grid=(b, length // tk),
        in_specs=[pl.BlockSpec((None, hh, tk), lambda i, j: (i, 0, j))],
        out_specs=pl.BlockSpec((None, hh, tk), lambda i, j: (i, 0, j)),
        out_shape=jax.ShapeDtypeStruct(x.shape, F32),
        scratch_shapes=[pltpu.VMEM((hh, 1), F32)],
        compiler_params=_cparams(("arbitrary", "arbitrary")),
    )(x)


def _head_pair_queries(q_ref):
    lane = lax.broadcasted_iota(jnp.int32, (1, LANES), 1)
    q = q_ref[...] * jnp.asarray(DH ** -0.5, BF16)
    zero = jnp.zeros_like(q)
    return lane, (jnp.where(lane < DH, q, zero), jnp.where(lane >= DH, q, zero))


def _fox_kernel(q_ref, k_ref, v_ref, fq_ref, fk_ref, o_ref, m_ref, l_ref, acc_ref, *, tq, tk, q_off):
    i = pl.program_id(2)
    lane, qs = _head_pair_queries(q_ref)
    qpos0 = q_off + i * tq
    n_full = (qpos0 + 1) // tk
    j_last = (qpos0 + tq - 1) // tk
    m_ref[...] = jnp.full_like(m_ref, -jnp.inf)
    l_ref[...] = jnp.zeros_like(l_ref)
    acc_ref[...] = jnp.zeros_like(acc_ref)

    def step(j, masked):
        ks = pl.multiple_of(j * tk, tk)
        kj = k_ref[pl.ds(ks, tk), :]
        vj = v_ref[pl.ds(ks, tk), :]
        if masked:
            kpos = ks + lax.broadcasted_iota(jnp.int32, (tq, tk), 1)
            qpos = qpos0 + lax.broadcasted_iota(jnp.int32, (tq, tk), 0)
            vis = kpos <= qpos
        for h in range(2):
            s = lax.dot_general(qs[h], kj, NT_DIMS, preferred_element_type=F32)
            s = s + fq_ref[:, h:h + 1] - fk_ref[j, h:h + 1, :]
            if masked:
                s = jnp.where(vis, s, -jnp.inf)
            m_prev = m_ref[h]
            m_new = jnp.maximum(m_prev, jnp.max(s, axis=-1, keepdims=True))
            alpha = jnp.exp(m_prev - m_new)
            p = jnp.exp(s - m_new)
            l_ref[h] = alpha * l_ref[h] + jnp.sum(p, axis=-1, keepdims=True)
            acc_ref[h] = alpha * acc_ref[h] + jnp.dot(p.astype(BF16), vj, preferred_element_type=F32)
            m_ref[h] = m_new

    def full_body(j, c):
        step(j, False)
        return c

    def part_body(j, c):
        step(j, True)
        return c

    lax.fori_loop(0, n_full, full_body, 0)
    lax.fori_loop(n_full, j_last + 1, part_body, 0)
    o = jnp.where(lane < DH, acc_ref[0] / l_ref[0], acc_ref[1] / l_ref[1])
    o_ref[...] = o.astype(BF16)


def _sb_kernel(q_ref, k_ref, v_ref, o_ref, r_ref, acc_ref, *, tq, tk, q_off):
    i = pl.program_id(2)
    lane, qs = _head_pair_queries(q_ref)
    qpos0 = q_off + i * tq
    n_full = qpos0 // tk
    j_last = (qpos0 + tq - 1) // tk
    r_ref[...] = jnp.zeros_like(r_ref)
    acc_ref[...] = jnp.zeros_like(acc_ref)
    rr = lax.broadcasted_iota(jnp.int32, (tk, tk), 0)
    cc = lax.broadcasted_iota(jnp.int32, (tk, tk), 1)
    tri = _ones_where(rr > cc)

    def step(j, masked):
        ks = pl.multiple_of(j * tk, tk)
        kj = k_ref[pl.ds(ks, tk), :]
        vj = v_ref[pl.ds(ks, tk), :]
        if masked:
            kpos = ks + lax.broadcasted_iota(jnp.int32, (tq, tk), 1)
            qpos = qpos0 + lax.broadcasted_iota(jnp.int32, (tq, tk), 0)
            vis = kpos < qpos
        for h in range(2):
            z = lax.dot_general(qs[h], kj, NT_DIMS, preferred_element_type=F32)
            sp = _softplus(z)
            log_keep = -sp
            if masked:
                log_keep = jnp.where(vis, log_keep, 0.0)
            hi = log_keep.astype(BF16)
            lo = (log_keep - hi.astype(F32)).astype(BF16)
            within = (jnp.dot(hi, tri, preferred_element_type=F32)
                      + jnp.dot(lo, tri, preferred_element_type=F32))
            run = r_ref[h]
            wgt = jnp.exp((z - sp) + (within + run))
            if masked:
                wgt = jnp.where(vis, wgt, 0.0)
            acc_ref[h] += jnp.dot(wgt.astype(BF16), vj, preferred_element_type=F32)
            r_ref[h] = run + within[:, 0:1] + log_keep[:, 0:1]

    def part_body(t, c):
        step(j_last - t, True)
        return c

    def full_body(t, c):
        step(n_full - 1 - t, False)
        return c

    lax.fori_loop(0, j_last + 1 - n_full, part_body, 0)
    lax.fori_loop(0, n_full, full_body, 0)
    o_ref[...] = jnp.where(lane < DH, acc_ref[0], acc_ref[1]).astype(BF16)


def _attention(kind, q_arr, k_arr, v_arr, fq, fk, *, n_seq, n_qt, tq, tk, q_off, lk,
               q_row_block0, q_col_block0, kv_flat_col_block0):
    n_hg = H_ATT // 2
    qmap = lambda b, g, i: (q_row_block0 + b * n_qt + i, q_col_block0 + g)
    if kv_flat_col_block0 is None:
        kspec = pl.BlockSpec((None, lk, LANES), lambda b, g, i: (b, 0, g))
        vspec = kspec
    else:
        kcol, vcol = kv_flat_col_block0
        kspec = pl.BlockSpec((lk, LANES), lambda b, g, i: (b, kcol + g))
        vspec = pl.BlockSpec((lk, LANES), lambda b, g, i: (b, vcol + g))
    in_specs = [pl.BlockSpec((tq, LANES), qmap), kspec, vspec]
    args = [q_arr, k_arr, v_arr]
    if kind == "fox":
        in_specs += [pl.BlockSpec((None, None, tq, 2), lambda b, g, i: (b, g, i, 0)),
                     pl.BlockSpec((None, None, lk // tk, 2, tk), lambda b, g, i: (b, g, 0, 0, 0))]
        args += [fq, fk]
        kern = functools.partial(_fox_kernel, tq=tq, tk=tk, q_off=q_off)
        scratch = [pltpu.VMEM((2, tq, 1), F32), pltpu.VMEM((2, tq, 1), F32), pltpu.VMEM((2, tq, LANES), F32)]
    else:
        kern = functools.partial(_sb_kernel, tq=tq, tk=tk, q_off=q_off)
        scratch = [pltpu.VMEM((2, tq, 1), F32), pltpu.VMEM((2, tq, LANES), F32)]
    return pl.pallas_call(
        kern,
        name=kind + "_attention",
        grid=(n_seq, n_hg, n_qt),
        in_specs=in_specs,
        out_specs=pl.BlockSpec((tq, LANES), lambda b, g, i: (b * n_qt + i, g)),
        out_shape=jax.ShapeDtypeStruct((n_seq * n_qt * tq, D_ATT), BF16),
        scratch_shapes=scratch,
        compiler_params=_cparams(("parallel", "parallel", "arbitrary")),
    )(*args)


def _ssd_kernel(z_ref, xbc_ref, dt_ref, cst_ref, s0_ref, cw_ref, cb_ref, alog_ref, dsk_ref, ng_ref,
                y_ref, sout_ref, cout_ref, xw_ref, s_ref, ysc_ref,
                *, q, l_valid, n_chunks):
    c = pl.program_id(1)
    hist = CONV_W - 1
    base = SUBLANES

    @pl.when(c == 0)
    def _():
        xw_ref[base - hist:base, :] = cst_ref[...]
        s_ref[...] = s0_ref[...]

    xw_ref[base:base + q, :] = xbc_ref[...]
    conv = cb_ref[...]
    for w in range(CONV_W):
        conv = conv + xw_ref[base - hist + w:base - hist + w + q, :] * cw_ref[w:w + 1, :]
    last = l_valid - 1

    @pl.when(c == last // q)
    def _():
        lr = base + last % q
        cout_ref[...] = xw_ref[lr - hist + 1:lr + 1, :]

    xw_ref[base - hist:base, :] = xw_ref[base + q - hist:base + q, :]

    act = conv * _sigmoid(conv)
    xs = act[:, :D_SSM]
    gw = G_SSM * N_SSM
    bm = act[:, D_SSM:D_SSM + gw].astype(BF16)
    cm = act[:, D_SSM + gw:].astype(BF16)

    rowg = c * q + lax.broadcasted_iota(jnp.int32, (q, LANES), 0)
    dt = jnp.where(rowg < l_valid, dt_ref[...], 0.0)
    adt = dt * (-jnp.exp(alog_ref[...]))
    rr = lax.broadcasted_iota(jnp.int32, (q, q), 0)
    cc = lax.broadcasted_iota(jnp.int32, (q, q), 1)
    causal = rr >= cc
    acs = _split3_dot(_ones_where(causal), adt)
    er = lax.broadcasted_iota(jnp.int32, (LANES, LANES), 0)
    ec = lax.broadcasted_iota(jnp.int32, (LANES, LANES), 1)
    eye = _ones_where(er == ec)
    acs_t = _split3_transpose(eye, acs)
    lane = lax.broadcasted_iota(jnp.int32, (1, LANES), 1)
    first = lane < P_SSM
    srow_first = lax.broadcasted_iota(jnp.int32, (LANES, 1), 0) < P_SSM
    heads_per_group = H_SSM // G_SSM

    cb_mats = []
    for g in range(G_SSM):
        cg = cm[:, g * N_SSM:(g + 1) * N_SSM]
        bg = bm[:, g * N_SSM:(g + 1) * N_SSM]
        cb_mats.append(lax.dot_general(cg, bg, NT_DIMS, preferred_element_type=F32))

    for pr in range(H_SSM // 2):
        h0, h1 = 2 * pr, 2 * pr + 1
        g = h0 // heads_per_group
        cg = cm[:, g * N_SSM:(g + 1) * N_SSM]
        bg = bm[:, g * N_SSM:(g + 1) * N_SSM]
        cols = slice(pr * LANES, (pr + 1) * LANES)
        xs_p = xs[:, cols]
        a0, a1 = acs[:, h0:h0 + 1], acs[:, h1:h1 + 1]
        xdt = xs_p * jnp.where(first, dt[:, h0:h0 + 1], dt[:, h1:h1 + 1])
        y = jnp.zeros((q, LANES), F32)
        for hh, a_col, keep in ((h0, a0, first), (h1, a1, jnp.logical_not(first))):
            seg = a_col - acs_t[hh:hh + 1, :]
            lmat = jnp.where(causal, jnp.exp(jnp.where(causal, seg, 0.0)), 0.0)
            m = (cb_mats[g] * lmat).astype(BF16)
            y = y + jnp.dot(m, jnp.where(keep, xdt, 0.0).astype(BF16), preferred_element_type=F32)
        st = s_ref[cols, :]
        y_off = lax.dot_general(cg, st.astype(BF16), NT_DIMS, preferred_element_type=F32)
        y = y + y_off * jnp.where(first, jnp.exp(a0), jnp.exp(a1))
        e0, e1 = a0[q - 1:q, :], a1[q - 1:q, :]
        wdec = jnp.where(first, jnp.exp(e0 - a0), jnp.exp(e1 - a1))
        xw = lax.dot_general(eye, (xdt * wdec).astype(BF16), NT_DIMS,
                             preferred_element_type=F32).astype(BF16)
        dec_rows = jnp.where(srow_first, jnp.exp(e0), jnp.exp(e1))
        s_ref[cols, :] = dec_rows * st + jnp.dot(xw, bg, preferred_element_type=F32)
        ysc_ref[:, cols] = y + dsk_ref[:, cols] * xs_p

    zt = z_ref[...]
    yg = ysc_ref[...] * (zt * _sigmoid(zt))
    gwid = D_SSM // G_SSM
    for g in range(G_SSM):
        blk = yg[:, g * gwid:(g + 1) * gwid]
        nrm = blk * lax.rsqrt(jnp.mean(blk * blk, axis=-1, keepdims=True) + EPS)
        y_ref[:, g * gwid:(g + 1) * gwid] = (nrm * ng_ref[:, g * gwid:(g + 1) * gwid]).astype(BF16)

    @pl.when(c == n_chunks - 1)
    def _():
        sout_ref[...] = s_ref[...]


def _ssd(proj32, cst, s0, cw, cb, alog, dsk, ng, *, n_seq, n_chunks, q, l_valid, row_block0,
         z_col, xbc_col, dt_col):
    rmap = lambda col: (lambda b, c: (row_block0 + b * n_chunks + c, col))
    const2 = lambda b, c: (0, 0)
    return pl.pallas_call(
        functools.partial(_ssd_kernel, q=q, l_valid=l_valid, n_chunks=n_chunks),
        name="ssd_mixer",
        grid=(n_seq, n_chunks),
        in_specs=[
            pl.BlockSpec((q, D_SSM), rmap(z_col)),
            pl.BlockSpec((q, CONV_DIM), rmap(xbc_col)),
            pl.BlockSpec((q, LANES), rmap(dt_col)),
            pl.BlockSpec((None, CONV_W - 1, CONV_DIM), lambda b, c: (b, 0, 0)),
            pl.BlockSpec((None, H_SSM * P_SSM, N_SSM), lambda b, c: (b, 0, 0)),
            pl.BlockSpec((CONV_W, CONV_DIM), const2),
            pl.BlockSpec((1, CONV_DIM), const2),
            pl.BlockSpec((1, LANES), const2),
            pl.BlockSpec((1, D_SSM), const2),
            pl.BlockSpec((1, D_SSM), const2),
        ],
        out_specs=[pl.BlockSpec((q, D_SSM), lambda b, c: (b * n_chunks + c, 0)),
                   pl.BlockSpec((None, H_SSM * P_SSM, N_SSM), lambda b, c: (b, 0, 0)),
                   pl.BlockSpec((None, CONV_W - 1, CONV_DIM), lambda b, c: (b, 0, 0))],
        out_shape=[jax.ShapeDtypeStruct((n_seq * n_chunks * q, D_SSM), BF16),
                   jax.ShapeDtypeStruct((n_seq, H_SSM * P_SSM, N_SSM), F32),
                   jax.ShapeDtypeStruct((n_seq, CONV_W - 1, CONV_DIM), F32)],
        scratch_shapes=[pltpu.VMEM((q + SUBLANES, CONV_DIM), F32),
                        pltpu.VMEM((H_SSM * P_SSM, N_SSM), F32),
                        pltpu.VMEM((q, D_SSM), F32)],
        compiler_params=_cparams(("arbitrary", "arbitrary")),
    )(proj32, proj32, proj32, cst, s0, cw, cb, alog, dsk, ng)


def _round_up(x, m):
    return (x + m - 1) // m * m


def _pad_cols(w, n):
    return jnp.pad(w, ((0, 0), (0, n - w.shape[1])))


def kernel(x_prompt, x_sample, state_s5_re, state_s5_im, cache_fox_k, cache_fox_v, cache_fox_logf, state_ssd, state_conv, cache_sb_k, cache_sb_v, meta_tokens, norm_ffn1_pre, norm_ffn1_post, norm_mix_pre, norm_mix_post, norm_ffn2_pre, norm_ffn2_post, ffn1_w_gate, ffn1_w_up, ffn1_w_down, ffn2_w_gate, ffn2_w_up, ffn2_w_down, ab_w_in, fox_b_f, s5_a_re, s5_a_im, s5_log_dt, s5_b_re, s5_b_im, s5_c_re, s5_c_im, s5_d, s5_w_glu, s5_b_glu, ab_w_out, cd_w_in, ssd_conv_w, ssd_conv_b, ssd_dt_bias, ssd_a_log, ssd_d, ssd_norm, cd_w_out):
    bp, seq, d = x_prompt.shape
    bs, ls, _ = x_sample.shape
    past = cache_fox_k.shape[2]
    depth = norm_ffn1_pre.shape[0]
    assert d == D_MODEL and depth == 2 and ab_w_in.shape[0] == 1 and cd_w_in.shape[0] == 1
    l0 = N_META + seq
    lp = _round_up(l0, SEQ_TILE)
    n_pt = lp // SEQ_TILE
    rows_s = bs * ls
    assert rows_s % SEQ_TILE == 0 and SEQ_TILE % ls == 0 and ls % 16 == 0 and l0 >= CONV_W
    t = bp * lp + rows_s
    tm = next(c for c in (1024, 512, 256) if t % c == 0)
    tf = 512
    s_blk0 = bp * n_pt
    lk_s = _round_up(past + ls, SEQ_TILE)

    meta = meta_tokens.astype(F32)
    zpad = jnp.zeros((lp - l0, d), F32)
    pieces = []
    for b in range(bp):
        pieces += [meta, x_prompt[b], zpad]
    pieces.append(x_sample.reshape(rows_s, d))
    h = jnp.concatenate(pieces, axis=0)

    g3 = lambda g: g.reshape(depth, 1, d)
    w16 = lambda w: w.astype(BF16)
    ffn1 = (g3(norm_ffn1_pre), g3(norm_ffn1_post), w16(ffn1_w_gate), w16(ffn1_w_up), w16(ffn1_w_down))
    ffn2 = (g3(norm_ffn2_pre), g3(norm_ffn2_post), w16(ffn2_w_gate), w16(ffn2_w_up), w16(ffn2_w_down))
    g_mix_pre, g_mix_post = g3(norm_mix_pre), g3(norm_mix_post)

    def prompt_rows(x):
        return x[:bp * lp].reshape(bp, lp, -1)[:, :l0]

    def sample_rows(x):
        return x[bp * lp:].reshape(bs, ls, -1)

    h = _ffn_half(h, *ffn1, 0, tm, tf)

    n_main = D_S5 + 3 * D_ATT
    n_ab = n_main + PROJ_TN
    w_ab = w16(_pad_cols(ab_w_in[0], n_ab))
    bias_ab = jnp.zeros((1, n_ab), F32).at[0, n_main:n_main + H_ATT].set(fox_b_f[0])
    p32, p16 = _inproj(h, g_mix_pre, w_ab, bias_ab, 0, tm, "log_sigmoid")

    logf = p32[:, n_main:n_main + H_ATT]
    kf = p32[:, D_S5 + D_ATT:D_S5 + 2 * D_ATT]
    vf = p32[:, D_S5 + 2 * D_ATT:n_main]

    tab, bfull, cfull = _s5_prep(s5_a_re[0], s5_a_im[0], s5_log_dt[0], s5_b_re[0], s5_b_im[0],
                                 s5_c_re[0], s5_c_im[0])
    h0_p = jnp.zeros((bp, 1, 2 * S5_W), F32)
    y_s5_p, st_p = _s5_scan(p16, bfull, cfull, tab, h0_p, row_block0=0, n_groups=bp, n_tiles=n_pt,
                            seg=SEQ_TILE, chain=True, l_valid=l0)
    spt = SEQ_TILE // ls
    h0_s = jnp.concatenate([state_s5_re[0].reshape(bs, S5_W), state_s5_im[0].reshape(bs, S5_W)], axis=-1)
    y_s5_s, st_s = _s5_scan(p16, bfull, cfull, tab, h0_s.reshape(bs // spt, spt, 2 * S5_W),
                            row_block0=s_blk0, n_groups=1, n_tiles=rows_s // SEQ_TILE,
                            seg=ls, chain=False, l_valid=ls)
    y_s5 = jnp.concatenate([y_s5_p, y_s5_s], axis=0)

    n_hg = H_ATT // 2

    def head_major(x):
        return jnp.transpose(x, (0, 2, 1))

    def fox_bias_layouts(fcum, q_lo, lq, lk):
        b = fcum.shape[0]
        fq = jnp.transpose(fcum[:, :, q_lo:q_lo + lq].reshape(b, n_hg, 2, lq), (0, 1, 3, 2))
        fk = jnp.transpose(fcum.reshape(b, n_hg, 2, lk // SEQ_TILE, SEQ_TILE), (0, 1, 3, 2, 4))
        return fq, fk

    logf_p = p32[:bp * lp, n_main:n_main + H_ATT].reshape(bp, lp, H_ATT)
    fq_p, fk_p = fox_bias_layouts(_time_cumsum(head_major(logf_p)), 0, lp, lp)
    qcol = D_S5 // LANES
    fox_p = _attention("fox", p16, p16, p16, fq_p, fk_p, n_seq=bp, n_qt=n_pt, tq=SEQ_TILE, tk=SEQ_TILE,
                       q_off=0, lk=lp, q_row_block0=0, q_col_block0=qcol,
                       kv_flat_col_block0=(qcol + D_ATT // LANES, qcol + 2 * D_ATT // LANES))

    def with_past(past_x, new_x, dtype):
        c = new_x.shape[-1]
        return jnp.concatenate([past_x.reshape(bs, past, c).astype(dtype),
                                new_x.reshape(bs, ls, c).astype(dtype),
                                jnp.zeros((bs, lk_s - past - ls, c), dtype)], axis=1)

    s_rows = slice(bp * lp, t)
    logf_s = with_past(cache_fox_logf[0], logf[s_rows], F32)
    fq_s, fk_s = fox_bias_layouts(_time_cumsum(head_major(logf_s)), past, ls, lk_s)
    k_s = with_past(cache_fox_k[0], p16[s_rows, D_S5 + D_ATT:D_S5 + 2 * D_ATT], BF16)
    v_s = with_past(cache_fox_v[0], p16[s_rows, D_S5 + 2 * D_ATT:n_main], BF16)
    fox_s = _attention("fox", p16, k_s, v_s, fq_s, fk_s, n_seq=bs, n_qt=1, tq=ls, tk=SEQ_TILE,
                       q_off=past, lk=lk_s, q_row_block0=bp * lp // ls, q_col_block0=qcol,
                       kv_flat_col_block0=None)
    fox = jnp.concatenate([fox_p, fox_s], axis=0)

    h = _mixout(h, g_mix_post, 0, w16(ab_w_out[0]), fox, tm, ya=y_s5, u=p32,
                dsk=s5_d[0].reshape(1, D_S5), wglu=w16(s5_w_glu[0]), bglu=s5_b_glu[0].reshape(1, D_S5))
    h = _ffn_half(h, *ffn2, 0, tm, tf)

    def split_state(st):
        return (st[:, :S5_W].reshape(1, -1, G_S5, N_S5), st[:, S5_W:].reshape(1, -1, G_S5, N_S5))

    s5_re_p, s5_im_p = split_state(st_p.reshape(bp, 2 * S5_W))
    s5_re_s, s5_im_s = split_state(st_s.reshape(bs, 2 * S5_W))
    heads = lambda x: x.reshape(1, x.shape[0], x.shape[1], H_ATT, DH)
    fox_k_p, fox_v_p = heads(prompt_rows(kf)), heads(prompt_rows(vf))
    fox_k_s, fox_v_s = heads(sample_rows(kf)), heads(sample_rows(vf))
    fox_logf_p, fox_logf_s = prompt_rows(logf)[None], sample_rows(logf)[None]

    h = _ffn_half(h, *ffn1, 1, tm, tf)

    wc = cd_w_in[0]
    o_xbc, o_dt = D_SSM, D_SSM + CONV_DIM
    o_q = CONV_DIM + D_SSM
    n_main_cd = o_q + 3 * D_ATT
    n_cd = n_main_cd + PROJ_TN
    w_cd = w16(_pad_cols(jnp.concatenate([wc[:, o_xbc:o_dt], wc[:, :o_xbc], wc[:, o_dt + H_SSM:],
                                          wc[:, o_dt:o_dt + H_SSM]], axis=1), n_cd))
    bias_cd = jnp.zeros((1, n_cd), F32).at[0, n_main_cd:n_main_cd + H_SSM].set(ssd_dt_bias[0])
    c32, c16 = _inproj(h, g_mix_pre, w_cd, bias_cd, 1, tm, "softplus")
    kc = c32[:, o_q + D_ATT:o_q + 2 * D_ATT]
    vc = c32[:, o_q + 2 * D_ATT:n_main_cd]

    cw = ssd_conv_w[0]
    cb = ssd_conv_b[0].reshape(1, CONV_DIM)
    alog = jnp.zeros((1, LANES), F32).at[0, :H_SSM].set(ssd_a_log[0])
    dsk = jnp.repeat(ssd_d[0], P_SSM).reshape(1, D_SSM)
    ng = ssd_norm[0].reshape(1, D_SSM)
    ssd_cols = dict(z_col=CONV_DIM // D_SSM, xbc_col=0, dt_col=n_main_cd // LANES)
    ssd_p_out, ssd_st_p, conv_p = _ssd(c32, jnp.zeros((bp, CONV_W - 1, CONV_DIM), F32),
                                       jnp.zeros((bp, H_SSM * P_SSM, N_SSM), F32), cw, cb, alog, dsk, ng,
                                       n_seq=bp, n_chunks=n_pt, q=SEQ_TILE, l_valid=l0, row_block0=0, **ssd_cols)
    ssd_s_out, ssd_st_s, conv_s = _ssd(c32, state_conv[0], state_ssd[0].reshape(bs, H_SSM * P_SSM, N_SSM),
                                       cw, cb, alog, dsk, ng, n_seq=bs, n_chunks=1, q=ls, l_valid=ls,
                                       row_block0=bp * lp // ls, **ssd_cols)
    ssd_out = jnp.concatenate([ssd_p_out, ssd_s_out], axis=0)

    qcol_cd = o_q // LANES
    sb_p = _attention("sb", c16, c16, c16, None, None, n_seq=bp, n_qt=n_pt, tq=SEQ_TILE, tk=SEQ_TILE,
                      q_off=0, lk=lp, q_row_block0=0, q_col_block0=qcol_cd,
                      kv_flat_col_block0=(qcol_cd + D_ATT // LANES, qcol_cd + 2 * D_ATT // LANES))
    k_s = with_past(cache_sb_k[0], c16[s_rows, o_q + D_ATT:o_q + 2 * D_ATT], BF16)
    v_s = with_past(cache_sb_v[0], c16[s_rows, o_q + 2 * D_ATT:n_main_cd], BF16)
    sb_s = _attention("sb", c16, k_s, v_s, None, None, n_seq=bs, n_qt=1, tq=ls, tk=SEQ_TILE,
                      q_off=past, lk=lk_s, q_row_block0=bp * lp // ls, q_col_block0=qcol_cd,
                      kv_flat_col_block0=None)
    sb = jnp.concatenate([sb_p, sb_s], axis=0)

    h = _mixout(h, g_mix_post, 1, w16(cd_w_out[0]), sb, tm, ya=ssd_out)
    h = _ffn_half(h, *ffn2, 1, tm, tf)

    ssd_p = ssd_st_p.reshape(1, bp, H_SSM, P_SSM, N_SSM)
    ssd_s = ssd_st_s.reshape(1, bs, H_SSM, P_SSM, N_SSM)
    sb_k_p, sb_v_p = heads(prompt_rows(kc)), heads(prompt_rows(vc))
    sb_k_s, sb_v_s = heads(sample_rows(kc)), heads(sample_rows(vc))

    y_prompt = h[:bp * lp].reshape(bp, lp, d)[:, N_META:l0]
    y_sample = h[bp * lp:].reshape(bs, ls, d)
    return (y_prompt, y_sample,
            s5_re_p, s5_im_p, fox_k_p, fox_v_p, fox_logf_p, ssd_p, conv_p[None], sb_k_p, sb_v_p,
            s5_re_s, s5_im_s, fox_k_s, fox_v_s, fox_logf_s, ssd_s, conv_s[None], sb_k_s, sb_v_s)
```

```python
import functools

import jax
import jax.numpy as jnp
from jax import lax
from jax.experimental import pallas as pl
from jax.experimental.pallas import tpu as pltpu

F32 = jnp.float32
BF16 = jnp.bfloat16

EPS = 1e-6
D_MODEL = 1024
DH = 64
N_META = 16
D_S5 = 512
S5_GROUP = 16
G_S5 = 32
N_S5 = 64
S5_W = G_S5 * N_S5
H_ATT = 8
D_ATT = H_ATT * DH
D_SSM = 512
P_SSM = 64
H_SSM = 8
G_SSM = 2
N_SSM = 128
CONV_W = 4
CONV_DIM = D_SSM + 2 * G_SSM * N_SSM

LANES = 128
SUBLANES = 8
SEQ_TILE = 256
PROJ_TN = 256
VMEM_LIMIT = 56 * 1024 * 1024

NT_DIMS = (((1,), (1,)), ((), ()))
EXP_ZERO = 110.0
M_INIT = -1e30


def _cparams(sem):
    return pltpu.CompilerParams(dimension_semantics=sem, vmem_limit_bytes=VMEM_LIMIT)


def _rms(x, g):
    return x * lax.rsqrt(jnp.mean(x * x, axis=-1, keepdims=True) + EPS) * g


def _sigmoid(x):
    return 1.0 / (1.0 + jnp.exp(-x))


def _softplus(x):
    return jnp.maximum(x, 0.0) + jnp.log1p(jnp.exp(-jnp.abs(x)))


def _ones_where(mask):
    return jnp.where(mask, 1.0, 0.0).astype(BF16)


def _split3(x):
    hi = x.astype(BF16)
    r1 = x - hi.astype(F32)
    mid = r1.astype(BF16)
    lo = (r1 - mid.astype(F32)).astype(BF16)
    return hi, mid, lo


def _split3_dot(tri, x):
    return sum(jnp.dot(tri, part, preferred_element_type=F32) for part in _split3(x))


def _split3_transpose(eye, x):
    return sum(lax.dot_general(eye, part, NT_DIMS, preferred_element_type=F32) for part in _split3(x))


def _ffn_kernel(h_ref, gpre_ref, gpost_ref, wg_ref, wu_ref, wd_ref, o_ref, xn_ref, acc_ref, *, nj):
    j = pl.program_id(1)

    @pl.when(j == 0)
    def _():
        xn_ref[...] = _rms(h_ref[...], gpre_ref[...]).astype(BF16)
        acc_ref[...] = jnp.zeros_like(acc_ref)

    xn = xn_ref[...]
    g = jnp.dot(xn, wg_ref[...], preferred_element_type=F32)
    u = jnp.dot(xn, wu_ref[...], preferred_element_type=F32)
    a = (g * _sigmoid(g)) * u
    acc_ref[...] += jnp.dot(a.astype(BF16), wd_ref[...], preferred_element_type=F32)

    @pl.when(j == nj - 1)
    def _():
        o_ref[...] = h_ref[...] + 0.5 * _rms(acc_ref[...], gpost_ref[...])


def _ffn_half(h, gpre, gpost, wg, wu, wd, layer, tm, tf):
    t, d = h.shape
    ff = wg.shape[-1]
    nj = ff // tf
    return pl.pallas_call(
        functools.partial(_ffn_kernel, nj=nj),
        name="ffn_half",
        grid=(t // tm, nj),
        in_specs=[
            pl.BlockSpec((tm, d), lambda i, j: (i, 0)),
            pl.BlockSpec((None, 1, d), lambda i, j: (layer, 0, 0)),
            pl.BlockSpec((None, 1, d), lambda i, j: (layer, 0, 0)),
            pl.BlockSpec((None, d, tf), lambda i, j: (layer, 0, j)),
            pl.BlockSpec((None, d, tf), lambda i, j: (layer, 0, j)),
            pl.BlockSpec((None, tf, d), lambda i, j: (layer, j, 0)),
        ],
        out_specs=pl.BlockSpec((tm, d), lambda i, j: (i, 0)),
        out_shape=jax.ShapeDtypeStruct((t, d), F32),
        scratch_shapes=[pltpu.VMEM((tm, d), BF16), pltpu.VMEM((tm, d), F32)],
        compiler_params=_cparams(("parallel", "arbitrary")),
    )(h, gpre, gpost, wg, wu, wd)


def _inproj_kernel(h_ref, g_ref, w_ref, b_ref, o32_ref, o16_ref, xn_ref, *, nj, tail):
    j = pl.program_id(1)

    @pl.when(j == 0)
    def _():
        xn_ref[...] = _rms(h_ref[...], g_ref[...]).astype(BF16)

    p = jnp.dot(xn_ref[...], w_ref[...], preferred_element_type=F32)

    @pl.when(j < nj - 1)
    def _():
        o32_ref[...] = p
        o16_ref[...] = p.astype(BF16)

    @pl.when(j == nj - 1)
    def _():
        x = p + b_ref[...]
        r = -_softplus(-x) if tail == "log_sigmoid" else _softplus(x)
        o32_ref[...] = r
        o16_ref[...] = r.astype(BF16)


def _inproj(h, g, w, bias, layer, tm, tail):
    t, d = h.shape
    n = w.shape[-1]
    nj = n // PROJ_TN
    return pl.pallas_call(
        functools.partial(_inproj_kernel, nj=nj, tail=tail),
        name="mix_inproj",
        grid=(t // tm, nj),
        in_specs=[
            pl.BlockSpec((tm, d), lambda i, j: (i, 0)),
            pl.BlockSpec((None, 1, d), lambda i, j: (layer, 0, 0)),
            pl.BlockSpec((d, PROJ_TN), lambda i, j: (0, j)),
            pl.BlockSpec((1, PROJ_TN), lambda i, j: (0, j)),
        ],
        out_specs=[pl.BlockSpec((tm, PROJ_TN), lambda i, j: (i, j)),
                   pl.BlockSpec((tm, PROJ_TN), lambda i, j: (i, j))],
        out_shape=[jax.ShapeDtypeStruct((t, n), F32), jax.ShapeDtypeStruct((t, n), BF16)],
        scratch_shapes=[pltpu.VMEM((tm, d), BF16)],
        compiler_params=_cparams(("parallel", "arbitrary")),
    )(h, g, w, bias)


def _gelu_tanh(x):
    return 0.5 * x * (1.0 + jnp.tanh(0.7978845608028654 * (x + 0.044715 * (x * x * x))))


def _mixout_kernel(*refs, s5_glu):
    if s5_glu:
        (ya_ref, u_ref, dsk_ref, wglu_ref, bglu_ref, yb_ref, wo_ref, h_ref, gpost_ref, o_ref) = refs
        y = ya_ref[...] + dsk_ref[...] * u_ref[...]
        g = _gelu_tanh(y)
        gate = _sigmoid(jnp.dot(g.astype(BF16), wglu_ref[...], preferred_element_type=F32) + bglu_ref[...])
        a = (g * gate).astype(BF16)
    else:
        (ya_ref, yb_ref, wo_ref, h_ref, gpost_ref, o_ref) = refs
        a = ya_ref[...]
    half = a.shape[-1]
    out = (jnp.dot(a, wo_ref[:half, :], preferred_element_type=F32)
           + jnp.dot(yb_ref[...], wo_ref[half:, :], preferred_element_type=F32))
    o_ref[...] = h_ref[...] + _rms(out, gpost_ref[...])


def _mixout(h, gpost, layer, wo, yb, tm, *, ya, u=None, dsk=None, wglu=None, bglu=None):
    t, d = h.shape
    half = yb.shape[-1]
    s5_glu = u is not None
    row = lambda i: (i, 0)
    const = lambda i: (0, 0)
    args, specs = [ya], [pl.BlockSpec((tm, half), row)]
    if s5_glu:
        args += [u, dsk, wglu, bglu]
        specs += [pl.BlockSpec((tm, half), row), pl.BlockSpec((1, half), const),
                  pl.BlockSpec((half, half), const), pl.BlockSpec((1, half), const)]
    args += [yb, wo, h, gpost]
    specs += [pl.BlockSpec((tm, half), row), pl.BlockSpec((2 * half, d), const),
              pl.BlockSpec((tm, d), row), pl.BlockSpec((None, 1, d), lambda i: (layer, 0, 0))]
    return pl.pallas_call(
        functools.partial(_mixout_kernel, s5_glu=s5_glu),
        name="mix_out",
        grid=(t // tm,),
        in_specs=specs,
        out_specs=pl.BlockSpec((tm, d), row),
        out_shape=jax.ShapeDtypeStruct((t, d), F32),
        compiler_params=_cparams(("parallel",)),
    )(*args)


def _s5_prep_kernel(are_ref, aim_ref, ldt_ref, bre_ref, bim_ref, cre_ref, cim_ref,
                    tab_ref, bf_ref, cf_ref):
    a_re = are_ref[...]
    a_im = aim_ref[...]
    dt = jnp.exp(ldt_ref[...])
    mag = jnp.exp(dt * a_re)
    ab_re = mag * jnp.cos(dt * a_im)
    ab_im = mag * jnp.sin(dt * a_im)
    den = a_re * a_re + a_im * a_im
    nr = ab_re - 1.0
    coef_re = (nr * a_re + ab_im * a_im) / den
    coef_im = (ab_im * a_re - nr * a_im) / den

    pw_re, pw_im = [ab_re], [ab_im]
    for _ in range(SUBLANES - 1):
        pr, pi = pw_re[-1], pw_im[-1]
        pw_re.append(pr * ab_re - pi * ab_im)
        pw_im.append(pr * ab_im + pi * ab_re)
    sub = lax.broadcasted_iota(jnp.int32, (SUBLANES, S5_W), 0)
    for k in range(3):
        sh = 1 << k
        tab_ref[k] = jnp.where(sub >= sh, pw_re[sh - 1], 0.0)
        tab_ref[3 + k] = jnp.where(sub >= sh, pw_im[sh - 1], 0.0)
    q_re = jnp.zeros((SUBLANES, S5_W), F32)
    q_im = jnp.zeros((SUBLANES, S5_W), F32)
    for r in range(SUBLANES):
        q_re = jnp.where(sub == r, pw_re[r], q_re)
        q_im = jnp.where(sub == r, pw_im[r], q_im)
    tab_ref[6] = q_re
    tab_ref[7] = q_im

    rg = lax.broadcasted_iota(jnp.int32, (D_S5, S5_W), 0) // S5_GROUP
    cg = lax.broadcasted_iota(jnp.int32, (D_S5, S5_W), 1) // N_S5
    b_re = bre_ref[...]
    b_im = bim_ref[...]
    bb_re = coef_re * b_re - coef_im * b_im
    bb_im = coef_re * b_im + coef_im * b_re
    bf_ref[:, :S5_W] = jnp.where(rg == cg, bb_re, 0.0).astype(BF16)
    bf_ref[:, S5_W:] = jnp.where(rg == cg, bb_im, 0.0).astype(BF16)
    rg2 = lax.broadcasted_iota(jnp.int32, (S5_W, D_S5), 0) // N_S5
    cg2 = lax.broadcasted_iota(jnp.int32, (S5_W, D_S5), 1) // S5_GROUP
    cf_ref[:S5_W, :] = jnp.where(rg2 == cg2, cre_ref[...], 0.0).astype(BF16)
    cf_ref[S5_W:, :] = jnp.where(rg2 == cg2, -cim_ref[...], 0.0).astype(BF16)


def _s5_prep(a_re, a_im, log_dt, b_re, b_im, c_re, c_im):
    flat = lambda x: x.reshape(1, S5_W)
    ldt = jnp.broadcast_to(log_dt[:, None], (G_S5, N_S5))
    b_t = lambda b: jnp.tile(jnp.transpose(b, (2, 0, 1)).reshape(S5_GROUP, S5_W), (G_S5, 1))
    c_t = lambda c: jnp.tile(jnp.transpose(c, (0, 2, 1)).reshape(S5_W, S5_GROUP), (1, G_S5))
    return pl.pallas_call(
        _s5_prep_kernel,
        name="s5_prep",
        out_shape=[jax.ShapeDtypeStruct((8, SUBLANES, S5_W), F32),
                   jax.ShapeDtypeStruct((D_S5, 2 * S5_W), BF16),
                   jax.ShapeDtypeStruct((2 * S5_W, D_S5), BF16)],
        compiler_params=pltpu.CompilerParams(vmem_limit_bytes=VMEM_LIMIT),
    )(flat(a_re), flat(a_im), flat(ldt), b_t(b_re), b_t(b_im), c_t(c_re), c_t(c_im))


S5_CB = 256


def _s5_kernel(u_ref, bf_ref, cf_ref, tab_ref, h0_ref, y_ref, st_ref, bu_ref, car_ref,
               *, rows, seg, chain, last_tile, last_row):
    i = pl.program_id(1)
    w = S5_W
    bu_ref[...] = jnp.dot(u_ref[...], bf_ref[...], preferred_element_type=F32)
    if chain:
        @pl.when(i == 0)
        def _():
            car_ref[...] = h0_ref[...]

    for c in range(0, w, S5_CB):
        re_cols = slice(c, c + S5_CB)
        im_cols = slice(w + c, w + c + S5_CB)
        pr = [tab_ref[k, :, re_cols] for k in range(3)]
        pi = [tab_ref[3 + k, :, re_cols] for k in range(3)]
        qr = tab_ref[6, :, re_cols]
        qi = tab_ref[7, :, re_cols]
        for sg in range(rows // seg):
            src = car_ref if chain else h0_ref
            srow = 0 if chain else sg
            cr0 = src[srow:srow + 1, re_cols]
            ci0 = src[srow:srow + 1, im_cols]

            def body(a, carry, sg=sg, re_cols=re_cols, im_cols=im_cols, pr=pr, pi=pi, qr=qr, qi=qi):
                cr, ci = carry
                r0 = pl.multiple_of(sg * seg + a * SUBLANES, SUBLANES)
                xr = bu_ref[pl.ds(r0, SUBLANES), re_cols]
                xi = bu_ref[pl.ds(r0, SUBLANES), im_cols]
                for k in range(3):
                    sr = pltpu.roll(xr, 1 << k, 0)
                    si = pltpu.roll(xi, 1 << k, 0)
                    xr, xi = xr + pr[k] * sr - pi[k] * si, xi + pr[k] * si + pi[k] * sr
                xr, xi = xr + qr * cr - qi * ci, xi + qr * ci + qi * cr
                bu_ref[pl.ds(r0, SUBLANES), re_cols] = xr
                bu_ref[pl.ds(r0, SUBLANES), im_cols] = xi
                return xr[SUBLANES - 1:SUBLANES, :], xi[SUBLANES - 1:SUBLANES, :]

            cr, ci = lax.fori_loop(0, seg // SUBLANES, body, (cr0, ci0))
            if chain:
                car_ref[0:1, re_cols] = cr
                car_ref[0:1, im_cols] = ci
            else:
                st_ref[sg:sg + 1, re_cols] = cr
                st_ref[sg:sg + 1, im_cols] = ci

    y_ref[...] = jnp.dot(bu_ref[...].astype(BF16), cf_ref[...], preferred_element_type=F32)
    if chain:
        @pl.when(i == last_tile)
        def _():
            st_ref[...] = bu_ref[last_row:last_row + 1, :]


def _s5_scan(proj16, bfull, cfull, tab, h0, *, row_block0, n_groups, n_tiles, seg, chain, l_valid):
    rows = SEQ_TILE
    s = h0.shape[1]
    last = l_valid - 1
    kern = functools.partial(_s5_kernel, rows=rows, seg=seg, chain=chain,
                             last_tile=last // rows, last_row=last % rows)
    if chain:
        umap = lambda b, i: (row_block0 + b * n_tiles + i, 0)
        hmap = lambda b, i: (b, 0, 0)
    else:
        umap = lambda b, i: (row_block0 + i, 0)
        hmap = lambda b, i: (i, 0, 0)
    const2 = lambda b, i: (0, 0)
    return pl.pallas_call(
        kern,
        name="s5_scan",
        grid=(n_groups, n_tiles),
        in_specs=[
            pl.BlockSpec((rows, D_S5), umap),
            pl.BlockSpec((D_S5, 2 * S5_W), const2),
            pl.BlockSpec((2 * S5_W, D_S5), const2),
            pl.BlockSpec((8, SUBLANES, S5_W), lambda b, i: (0, 0, 0)),
            pl.BlockSpec((None, s, 2 * S5_W), hmap),
        ],
        out_specs=[pl.BlockSpec((rows, D_S5), lambda b, i: (b * n_tiles + i, 0)),
                   pl.BlockSpec((None, s, 2 * S5_W), hmap)],
        out_shape=[jax.ShapeDtypeStruct((n_groups * n_tiles * rows, D_S5), F32),
                   jax.ShapeDtypeStruct(h0.shape, F32)],
        scratch_shapes=[pltpu.VMEM((rows, 2 * S5_W), F32), pltpu.VMEM((1, 2 * S5_W), F32)],
        compiler_params=_cparams(("arbitrary", "arbitrary")),
    )(proj16, bfull, cfull, tab, h0)


def _cumsum_kernel(x_ref, o_ref, pmin_ref, car_ref, min_ref, *, tk):
    j = pl.program_id(1)

    @pl.when(j == 0)
    def _():
        car_ref[...] = jnp.zeros_like(car_ref)
        min_ref[...] = jnp.full_like(min_ref, jnp.inf)

    r = lax.broadcasted_iota(jnp.int32, (tk, tk), 0)
    c = lax.broadcasted_iota(jnp.int32, (tk, tk), 1)
    tri = _ones_where(r <= c)
    acc = sum(jnp.dot(part, tri, preferred_element_type=F32) for part in _split3(x_ref[...])) + car_ref[...]
    o_ref[...] = acc
    car_ref[...] = acc[:, tk - 1:tk]
    run_min = jnp.minimum(min_ref[...], jnp.min(acc, axis=-1, keepdims=True))
    min_ref[...] = run_min
    pmin_ref[...] = run_min


def _time_cumsum(x):
    b, hh, length = x.shape
    tk = SEQ_TILE
    return pl.pallas_call(
        functools.partial(_cumsum_kernel, tk=tk),
        name="time_cumsum",
        grid=(b, length // tk),
        in_specs=[pl.BlockSpec((None, hh, tk), lambda i, j: (i, 0, j))],
        out_specs=[pl.BlockSpec((None, hh, tk), lambda i, j: (i, 0, j)),
                   pl.BlockSpec((None, None, hh, 1), lambda i, j: (i, j, 0, 0))],
        out_shape=[jax.ShapeDtypeStruct(x.shape, F32),
                   jax.ShapeDtypeStruct((b, length // tk, hh, 1), F32)],
        scratch_shapes=[pltpu.VMEM((hh, 1), F32), pltpu.VMEM((hh, 1), F32)],
        compiler_params=_cparams(("arbitrary", "arbitrary")),
    )(x)


def _head_pair_queries(q_ref):
    lane = lax.broadcasted_iota(jnp.int32, (1, LANES), 1)
    q = q_ref[...] * jnp.asarray(DH ** -0.5, BF16)
    zero = jnp.zeros_like(q)
    return lane, (jnp.where(lane < DH, q, zero), jnp.where(lane >= DH, q, zero))


def _fox_kernel(*refs, tq, tk, q_off, prune):
    if prune:
        q_ref, k_ref, v_ref, fq_ref, fk_ref, fmin_ref, o_ref, m_ref, l_ref, acc_ref, kabs_ref = refs
    else:
        q_ref, k_ref, v_ref, fq_ref, fk_ref, o_ref, m_ref, l_ref, acc_ref = refs
    i = pl.program_id(2)
    lane, qs = _head_pair_queries(q_ref)
    qpos0 = q_off + i * tq
    n_full = (qpos0 + 1) // tk
    j_last = (qpos0 + tq - 1) // tk
    m_ref[...] = jnp.full_like(m_ref, M_INIT)
    l_ref[...] = jnp.zeros_like(l_ref)
    acc_ref[...] = jnp.zeros_like(acc_ref)
    if prune:
        @pl.when(i == 0)
        def _():
            kabs_ref[...] = jnp.max(jnp.abs(k_ref[...].astype(F32)), axis=0, keepdims=True)

        kabs = kabs_ref[...]
        slack = [jnp.sum(jnp.abs(qs[h].astype(F32)) * kabs, axis=-1, keepdims=True) + fq_ref[:, h:h + 1]
                 for h in range(2)]

        def worth_visiting(j):
            jc = jnp.maximum(j, 0)
            gap = jnp.maximum(jnp.max(slack[0] - m_ref[0] - fmin_ref[jc, 0:1, :]),
                              jnp.max(slack[1] - m_ref[1] - fmin_ref[jc, 1:2, :]))
            return gap > -EXP_ZERO

    def step(j, masked):
        ks = pl.multiple_of(j * tk, tk)
        kj = k_ref[pl.ds(ks, tk), :]
        vj = v_ref[pl.ds(ks, tk), :]
        if masked:
            kpos = ks + lax.broadcasted_iota(jnp.int32, (tq, tk), 1)
            qpos = qpos0 + lax.broadcasted_iota(jnp.int32, (tq, tk), 0)
            vis = kpos <= qpos
        for h in range(2):
            s = lax.dot_general(qs[h], kj, NT_DIMS, preferred_element_type=F32)
            s = s + fq_ref[:, h:h + 1] - fk_ref[j, h:h + 1, :]
            if masked:
                s = jnp.where(vis, s, -jnp.inf)
            m_prev = m_ref[h]
            m_new = jnp.maximum(m_prev, jnp.max(s, axis=-1, keepdims=True))
            alpha = jnp.exp(m_prev - m_new)
            p = jnp.exp(s - m_new)
            l_ref[h] = alpha * l_ref[h] + jnp.sum(p, axis=-1, keepdims=True)
            acc_ref[h] = alpha * acc_ref[h] + jnp.dot(p.astype(BF16), vj, preferred_element_type=F32)
            m_ref[h] = m_new

    def part_body(t, c):
        step(j_last - t, True)
        return c

    lax.fori_loop(0, j_last + 1 - n_full, part_body, 0)
    if prune:
        def more(c):
            t, go = c
            return jnp.logical_and(t < n_full, go)

        def full_body(c):
            t, _ = c
            step(n_full - 1 - t, False)
            return t + 1, worth_visiting(n_full - 2 - t)

        lax.while_loop(more, full_body, (0, worth_visiting(n_full - 1)))
    else:
        def full_body(t, c):
            step(n_full - 1 - t, False)
            return c

        lax.fori_loop(0, n_full, full_body, 0)
    o = jnp.where(lane < DH, acc_ref[0] / l_ref[0], acc_ref[1] / l_ref[1])
    o_ref[...] = o.astype(BF16)


def _sb_kernel(q_ref, k_ref, v_ref, o_ref, r_ref, acc_ref, *, tq, tk, q_off):
    i = pl.program_id(2)
    lane, qs = _head_pair_queries(q_ref)
    qpos0 = q_off + i * tq
    n_full = qpos0 // tk
    j_last = (qpos0 + tq - 1) // tk
    r_ref[...] = jnp.zeros_like(r_ref)
    acc_ref[...] = jnp.zeros_like(acc_ref)
    rr = lax.broadcasted_iota(jnp.int32, (tk, tk), 0)
    cc = lax.broadcasted_iota(jnp.int32, (tk, tk), 1)
    tri = _ones_where(rr > cc)

    def step(j, masked):
        ks = pl.multiple_of(j * tk, tk)
        kj = k_ref[pl.ds(ks, tk), :]
        vj = v_ref[pl.ds(ks, tk), :]
        if masked:
            kpos = ks + lax.broadcasted_iota(jnp.int32, (tq, tk), 1)
            qpos = qpos0 + lax.broadcasted_iota(jnp.int32, (tq, tk), 0)
            vis = kpos < qpos
        for h in range(2):
            z = lax.dot_general(qs[h], kj, NT_DIMS, preferred_element_type=F32)
            sp = _softplus(z)
            log_keep = -sp
            if masked:
                log_keep = jnp.where(vis, log_keep, 0.0)
            hi = log_keep.astype(BF16)
            lo = (log_keep - hi.astype(F32)).astype(BF16)
            within = (jnp.dot(hi, tri, preferred_element_type=F32)
                      + jnp.dot(lo, tri, preferred_element_type=F32))
            run = r_ref[h]
            wgt = jnp.exp((z - sp) + (within + run))
            if masked:
                wgt = jnp.where(vis, wgt, 0.0)
            acc_ref[h] += jnp.dot(wgt.astype(BF16), vj, preferred_element_type=F32)
            r_ref[h] = run + within[:, 0:1] + log_keep[:, 0:1]

    def part_body(t, c):
        step(j_last - t, True)
        return c

    def worth_visiting():
        return jnp.max(jnp.maximum(r_ref[0], r_ref[1])) > -EXP_ZERO

    def more(c):
        t, go = c
        return jnp.logical_and(t < n_full, go)

    def full_body(c):
        t, _ = c
        step(n_full - 1 - t, False)
        return t + 1, worth_visiting()

    lax.fori_loop(0, j_last + 1 - n_full, part_body, 0)
    lax.while_loop(more, full_body, (0, worth_visiting()))
    o_ref[...] = jnp.where(lane < DH, acc_ref[0], acc_ref[1]).astype(BF16)


def _attention(kind, q_arr, k_arr, v_arr, fq, fk, fmin=None, *, n_seq, n_qt, tq, tk, q_off, lk,
               q_row_block0, q_col_block0, kv_flat_col_block0):
    n_hg = H_ATT // 2
    qmap = lambda b, g, i: (q_row_block0 + b * n_qt + i, q_col_block0 + g)
    if kv_flat_col_block0 is None:
        kspec = pl.BlockSpec((None, lk, LANES), lambda b, g, i: (b, 0, g))
        vspec = kspec
    else:
        kcol, vcol = kv_flat_col_block0
        kspec = pl.BlockSpec((lk, LANES), lambda b, g, i: (b, kcol + g))
        vspec = pl.BlockSpec((lk, LANES), lambda b, g, i: (b, vcol + g))
    in_specs = [pl.BlockSpec((tq, LANES), qmap), kspec, vspec]
    args = [q_arr, k_arr, v_arr]
    if kind == "fox":
        prune = fmin is not None
        in_specs += [pl.BlockSpec((None, None, tq, 2), lambda b, g, i: (b, g, i, 0)),
                     pl.BlockSpec((None, None, lk // tk, 2, tk), lambda b, g, i: (b, g, 0, 0, 0))]
        args += [fq, fk]
        scratch = [pltpu.VMEM((2, tq, 1), F32), pltpu.VMEM((2, tq, 1), F32), pltpu.VMEM((2, tq, LANES), F32)]
        if prune:
            in_specs.append(pl.BlockSpec((None, None, lk // tk, 2, 1), lambda b, g, i: (b, g, 0, 0, 0)))
            args.append(fmin)
            scratch.append(pltpu.VMEM((1, LANES), F32))
        kern = functools.partial(_fox_kernel, tq=tq, tk=tk, q_off=q_off, prune=prune)
    else:
        kern = functools.partial(_sb_kernel, tq=tq, tk=tk, q_off=q_off)
        scratch = [pltpu.VMEM((2, tq, 1), F32), pltpu.VMEM((2, tq, LANES), F32)]
    return pl.pallas_call(
        kern,
        name=kind + "_attention",
        grid=(n_seq, n_hg, n_qt),
        in_specs=in_specs,
        out_specs=pl.BlockSpec((tq, LANES), lambda b, g, i: (b * n_qt + i, g)),
        out_shape=jax.ShapeDtypeStruct((n_seq * n_qt * tq, D_ATT), BF16),
        scratch_shapes=scratch,
        compiler_params=_cparams(("parallel", "parallel", "arbitrary")),
    )(*args)


def _ssd_kernel(z_ref, xbc_ref, dt_ref, cst_ref, s0_ref, cw_ref, cb_ref, alog_ref, dsk_ref, ng_ref,
                y_ref, sout_ref, cout_ref, xw_ref, s_ref, ysc_ref,
                *, q, l_valid, n_chunks):
    c = pl.program_id(1)
    hist = CONV_W - 1
    base = SUBLANES

    @pl.when(c == 0)
    def _():
        xw_ref[base - hist:base, :] = cst_ref[...]
        s_ref[...] = s0_ref[...]

    xw_ref[base:base + q, :] = xbc_ref[...]
    conv = cb_ref[...]
    for w in range(CONV_W):
        conv = conv + xw_ref[base - hist + w:base - hist + w + q, :] * cw_ref[w:w + 1, :]
    last = l_valid - 1

    @pl.when(c == last // q)
    def _():
        lr = base + last % q
        cout_ref[...] = xw_ref[lr - hist + 1:lr + 1, :]

    xw_ref[base - hist:base, :] = xw_ref[base + q - hist:base + q, :]

    act = conv * _sigmoid(conv)
    xs = act[:, :D_SSM]
    gw = G_SSM * N_SSM
    bm = act[:, D_SSM:D_SSM + gw].astype(BF16)
    cm = act[:, D_SSM + gw:].astype(BF16)

    rowg = c * q + lax.broadcasted_iota(jnp.int32, (q, LANES), 0)
    dt = jnp.where(rowg < l_valid, dt_ref[...], 0.0)
    adt = dt * (-jnp.exp(alog_ref[...]))
    rr = lax.broadcasted_iota(jnp.int32, (q, q), 0)
    cc = lax.broadcasted_iota(jnp.int32, (q, q), 1)
    causal = rr >= cc
    acs = _split3_dot(_ones_where(causal), adt)
    er = lax.broadcasted_iota(jnp.int32, (LANES, LANES), 0)
    ec = lax.broadcasted_iota(jnp.int32, (LANES, LANES), 1)
    eye = _ones_where(er == ec)
    acs_t = _split3_transpose(eye, acs)
    lane = lax.broadcasted_iota(jnp.int32, (1, LANES), 1)
    first = lane < P_SSM
    srow_first = lax.broadcasted_iota(jnp.int32, (LANES, 1), 0) < P_SSM
    heads_per_group = H_SSM // G_SSM

    cb_mats = []
    for g in range(G_SSM):
        cg = cm[:, g * N_SSM:(g + 1) * N_SSM]
        bg = bm[:, g * N_SSM:(g + 1) * N_SSM]
        cb_mats.append(lax.dot_general(cg, bg, NT_DIMS, preferred_element_type=F32))

    for pr in range(H_SSM // 2):
        h0, h1 = 2 * pr, 2 * pr + 1
        g = h0 // heads_per_group
        cg = cm[:, g * N_SSM:(g + 1) * N_SSM]
        bg = bm[:, g * N_SSM:(g + 1) * N_SSM]
        cols = slice(pr * LANES, (pr + 1) * LANES)
        xs_p = xs[:, cols]
        a0, a1 = acs[:, h0:h0 + 1], acs[:, h1:h1 + 1]
        xdt = xs_p * jnp.where(first, dt[:, h0:h0 + 1], dt[:, h1:h1 + 1])
        y = jnp.zeros((q, LANES), F32)
        for hh, a_col, keep in ((h0, a0, first), (h1, a1, jnp.logical_not(first))):
            seg = a_col - acs_t[hh:hh + 1, :]
            lmat = jnp.where(causal, jnp.exp(jnp.where(causal, seg, 0.0)), 0.0)
            m = (cb_mats[g] * lmat).astype(BF16)
            y = y + jnp.dot(m, jnp.where(keep, xdt, 0.0).astype(BF16), preferred_element_type=F32)
        st = s_ref[cols, :]
        y_off = lax.dot_general(cg, st.astype(BF16), NT_DIMS, preferred_element_type=F32)
        y = y + y_off * jnp.where(first, jnp.exp(a0), jnp.exp(a1))
        e0, e1 = a0[q - 1:q, :], a1[q - 1:q, :]
        wdec = jnp.where(first, jnp.exp(e0 - a0), jnp.exp(e1 - a1))
        xw = lax.dot_general(eye, (xdt * wdec).astype(BF16), NT_DIMS,
                             preferred_element_type=F32).astype(BF16)
        dec_rows = jnp.where(srow_first, jnp.exp(e0), jnp.exp(e1))
        s_ref[cols, :] = dec_rows * st + jnp.dot(xw, bg, preferred_element_type=F32)
        ysc_ref[:, cols] = y + dsk_ref[:, cols] * xs_p

    zt = z_ref[...]
    yg = ysc_ref[...] * (zt * _sigmoid(zt))
    gwid = D_SSM // G_SSM
    for g in range(G_SSM):
        blk = yg[:, g * gwid:(g + 1) * gwid]
        nrm = blk * lax.rsqrt(jnp.mean(blk * blk, axis=-1, keepdims=True) + EPS)
        y_ref[:, g * gwid:(g + 1) * gwid] = (nrm * ng_ref[:, g * gwid:(g + 1) * gwid]).astype(BF16)

    @pl.when(c == n_chunks - 1)
    def _():
        sout_ref[...] = s_ref[...]


def _ssd(proj32, cst, s0, cw, cb, alog, dsk, ng, *, n_seq, n_chunks, q, l_valid, row_block0,
         z_col, xbc_col, dt_col):
    rmap = lambda col: (lambda b, c: (row_block0 + b * n_chunks + c, col))
    const2 = lambda b, c: (0, 0)
    return pl.pallas_call(
        functools.partial(_ssd_kernel, q=q, l_valid=l_valid, n_chunks=n_chunks),
        name="ssd_mixer",
        grid=(n_seq, n_chunks),
        in_specs=[
            pl.BlockSpec((q, D_SSM), rmap(z_col)),
            pl.BlockSpec((q, CONV_DIM), rmap(xbc_col)),
            pl.BlockSpec((q, LANES), rmap(dt_col)),
            pl.BlockSpec((None, CONV_W - 1, CONV_DIM), lambda b, c: (b, 0, 0)),
            pl.BlockSpec((None, H_SSM * P_SSM, N_SSM), lambda b, c: (b, 0, 0)),
            pl.BlockSpec((CONV_W, CONV_DIM), const2),
            pl.BlockSpec((1, CONV_DIM), const2),
            pl.BlockSpec((1, LANES), const2),
            pl.BlockSpec((1, D_SSM), const2),
            pl.BlockSpec((1, D_SSM), const2),
        ],
        out_specs=[pl.BlockSpec((q, D_SSM), lambda b, c: (b * n_chunks + c, 0)),
                   pl.BlockSpec((None, H_SSM * P_SSM, N_SSM), lambda b, c: (b, 0, 0)),
                   pl.BlockSpec((None, CONV_W - 1, CONV_DIM), lambda b, c: (b, 0, 0))],
        out_shape=[jax.ShapeDtypeStruct((n_seq * n_chunks * q, D_SSM), BF16),
                   jax.ShapeDtypeStruct((n_seq, H_SSM * P_SSM, N_SSM), F32),
                   jax.ShapeDtypeStruct((n_seq, CONV_W - 1, CONV_DIM), F32)],
        scratch_shapes=[pltpu.VMEM((q + SUBLANES, CONV_DIM), F32),
                        pltpu.VMEM((H_SSM * P_SSM, N_SSM), F32),
                        pltpu.VMEM((q, D_SSM), F32)],
        compiler_params=_cparams(("arbitrary", "arbitrary")),
    )(proj32, proj32, proj32, cst, s0, cw, cb, alog, dsk, ng)


def _round_up(x, m):
    return (x + m - 1) // m * m


def _pad_cols(w, n):
    return jnp.pad(w, ((0, 0), (0, n - w.shape[1])))


def kernel(x_prompt, x_sample, state_s5_re, state_s5_im, cache_fox_k, cache_fox_v, cache_fox_logf, state_ssd, state_conv, cache_sb_k, cache_sb_v, meta_tokens, norm_ffn1_pre, norm_ffn1_post, norm_mix_pre, norm_mix_post, norm_ffn2_pre, norm_ffn2_post, ffn1_w_gate, ffn1_w_up, ffn1_w_down, ffn2_w_gate, ffn2_w_up, ffn2_w_down, ab_w_in, fox_b_f, s5_a_re, s5_a_im, s5_log_dt, s5_b_re, s5_b_im, s5_c_re, s5_c_im, s5_d, s5_w_glu, s5_b_glu, ab_w_out, cd_w_in, ssd_conv_w, ssd_conv_b, ssd_dt_bias, ssd_a_log, ssd_d, ssd_norm, cd_w_out):
    bp, seq, d = x_prompt.shape
    bs, ls, _ = x_sample.shape
    past = cache_fox_k.shape[2]
    depth = norm_ffn1_pre.shape[0]
    assert d == D_MODEL and depth == 2 and ab_w_in.shape[0] == 1 and cd_w_in.shape[0] == 1
    l0 = N_META + seq
    lp = _round_up(l0, SEQ_TILE)
    n_pt = lp // SEQ_TILE
    rows_s = bs * ls
    assert rows_s % SEQ_TILE == 0 and SEQ_TILE % ls == 0 and ls % 16 == 0 and l0 >= CONV_W
    t = bp * lp + rows_s
    tm = next(c for c in (1024, 512, 256) if t % c == 0)
    tf = 512
    s_blk0 = bp * n_pt
    lk_s = _round_up(past + ls, SEQ_TILE)

    meta = meta_tokens.astype(F32)
    zpad = jnp.zeros((lp - l0, d), F32)
    pieces = []
    for b in range(bp):
        pieces += [meta, x_prompt[b], zpad]
    pieces.append(x_sample.reshape(rows_s, d))
    h = jnp.concatenate(pieces, axis=0)

    g3 = lambda g: g.reshape(depth, 1, d)
    w16 = lambda w: w.astype(BF16)
    ffn1 = (g3(norm_ffn1_pre), g3(norm_ffn1_post), w16(ffn1_w_gate), w16(ffn1_w_up), w16(ffn1_w_down))
    ffn2 = (g3(norm_ffn2_pre), g3(norm_ffn2_post), w16(ffn2_w_gate), w16(ffn2_w_up), w16(ffn2_w_down))
    g_mix_pre, g_mix_post = g3(norm_mix_pre), g3(norm_mix_post)

    def prompt_rows(x):
        return x[:bp * lp].reshape(bp, lp, -1)[:, :l0]

    def sample_rows(x):
        return x[bp * lp:].reshape(bs, ls, -1)

    h = _ffn_half(h, *ffn1, 0, tm, tf)

    n_main = D_S5 + 3 * D_ATT
    n_ab = n_main + PROJ_TN
    w_ab = w16(_pad_cols(ab_w_in[0], n_ab))
    bias_ab = jnp.zeros((1, n_ab), F32).at[0, n_main:n_main + H_ATT].set(fox_b_f[0])
    p32, p16 = _inproj(h, g_mix_pre, w_ab, bias_ab, 0, tm, "log_sigmoid")

    logf = p32[:, n_main:n_main + H_ATT]
    kf = p32[:, D_S5 + D_ATT:D_S5 + 2 * D_ATT]
    vf = p32[:, D_S5 + 2 * D_ATT:n_main]

    tab, bfull, cfull = _s5_prep(s5_a_re[0], s5_a_im[0], s5_log_dt[0], s5_b_re[0], s5_b_im[0],
                                 s5_c_re[0], s5_c_im[0])
    h0_p = jnp.zeros((bp, 1, 2 * S5_W), F32)
    y_s5_p, st_p = _s5_scan(p16, bfull, cfull, tab, h0_p, row_block0=0, n_groups=bp, n_tiles=n_pt,
                            seg=SEQ_TILE, chain=True, l_valid=l0)
    spt = SEQ_TILE // ls
    h0_s = jnp.concatenate([state_s5_re[0].reshape(bs, S5_W), state_s5_im[0].reshape(bs, S5_W)], axis=-1)
    y_s5_s, st_s = _s5_scan(p16, bfull, cfull, tab, h0_s.reshape(bs // spt, spt, 2 * S5_W),
                            row_block0=s_blk0, n_groups=1, n_tiles=rows_s // SEQ_TILE,
                            seg=ls, chain=False, l_valid=ls)
    y_s5 = jnp.concatenate([y_s5_p, y_s5_s], axis=0)

    n_hg = H_ATT // 2

    def head_major(x):
        return jnp.transpose(x, (0, 2, 1))

    def fox_bias_layouts(fcum, q_lo, lq, lk, tk):
        b = fcum.shape[0]
        fq = jnp.transpose(fcum[:, :, q_lo:q_lo + lq].reshape(b, n_hg, 2, lq), (0, 1, 3, 2))
        fk = jnp.transpose(fcum.reshape(b, n_hg, 2, lk // tk, tk), (0, 1, 3, 2, 4))
        return fq, fk

    logf_p = p32[:bp * lp, n_main:n_main + H_ATT].reshape(bp, lp, H_ATT)
    fcum_p, fmin_p = _time_cumsum(head_major(logf_p))
    fq_p, fk_p = fox_bias_layouts(fcum_p, 0, lp, lp, SEQ_TILE)
    fmin_p = jnp.transpose(fmin_p.reshape(bp, n_pt, n_hg, 2, 1), (0, 2, 1, 3, 4))
    qcol = D_S5 // LANES
    fox_p = _attention("fox", p16, p16, p16, fq_p, fk_p, fmin_p, n_seq=bp, n_qt=n_pt, tq=SEQ_TILE, tk=SEQ_TILE,
                       q_off=0, lk=lp, q_row_block0=0, q_col_block0=qcol,
                       kv_flat_col_block0=(qcol + D_ATT // LANES, qcol + 2 * D_ATT // LANES))

    def with_past(past_x, new_x, dtype):
        c = new_x.shape[-1]
        return jnp.concatenate([past_x.reshape(bs, past, c).astype(dtype),
                                new_x.reshape(bs, ls, c).astype(dtype),
                                jnp.zeros((bs, lk_s - past - ls, c), dtype)], axis=1)

    s_rows = slice(bp * lp, t)
    logf_s = with_past(cache_fox_logf[0], logf[s_rows], F32)
    fq_s, fk_s = fox_bias_layouts(_time_cumsum(head_major(logf_s))[0], past, ls, lk_s, lk_s)
    k_s = with_past(cache_fox_k[0], p16[s_rows, D_S5 + D_ATT:D_S5 + 2 * D_ATT], BF16)
    v_s = with_past(cache_fox_v[0], p16[s_rows, D_S5 + 2 * D_ATT:n_main], BF16)
    fox_s = _attention("fox", p16, k_s, v_s, fq_s, fk_s, n_seq=bs, n_qt=1, tq=ls, tk=lk_s,
                       q_off=past, lk=lk_s, q_row_block0=bp * lp // ls, q_col_block0=qcol,
                       kv_flat_col_block0=None)
    fox = jnp.concatenate([fox_p, fox_s], axis=0)

    h = _mixout(h, g_mix_post, 0, w16(ab_w_out[0]), fox, tm, ya=y_s5, u=p32,
                dsk=s5_d[0].reshape(1, D_S5), wglu=w16(s5_w_glu[0]), bglu=s5_b_glu[0].reshape(1, D_S5))
    h = _ffn_half(h, *ffn2, 0, tm, tf)

    def split_state(st):
        return (st[:, :S5_W].reshape(1, -1, G_S5, N_S5), st[:, S5_W:].reshape(1, -1, G_S5, N_S5))

    s5_re_p, s5_im_p = split_state(st_p.reshape(bp, 2 * S5_W))
    s5_re_s, s5_im_s = split_state(st_s.reshape(bs, 2 * S5_W))
    heads = lambda x: x.reshape(1, x.shape[0], x.shape[1], H_ATT, DH)
    fox_k_p, fox_v_p = heads(prompt_rows(kf)), heads(prompt_rows(vf))
    fox_k_s, fox_v_s = heads(sample_rows(kf)), heads(sample_rows(vf))
    fox_logf_p, fox_logf_s = prompt_rows(logf)[None], sample_rows(logf)[None]

    h = _ffn_half(h, *ffn1, 1, tm, tf)

    wc = cd_w_in[0]
    o_xbc, o_dt = D_SSM, D_SSM + CONV_DIM
    o_q = CONV_DIM + D_SSM
    n_main_cd = o_q + 3 * D_ATT
    n_cd = n_main_cd + PROJ_TN
    w_cd = w16(_pad_cols(jnp.concatenate([wc[:, o_xbc:o_dt], wc[:, :o_xbc], wc[:, o_dt + H_SSM:],
                                          wc[:, o_dt:o_dt + H_SSM]], axis=1), n_cd))
    bias_cd = jnp.zeros((1, n_cd), F32).at[0, n_main_cd:n_main_cd + H_SSM].set(ssd_dt_bias[0])
    c32, c16 = _inproj(h, g_mix_pre, w_cd, bias_cd, 1, tm, "softplus")
    kc = c32[:, o_q + D_ATT:o_q + 2 * D_ATT]
    vc = c32[:, o_q + 2 * D_ATT:n_main_cd]

    cw = ssd_conv_w[0]
    cb = ssd_conv_b[0].reshape(1, CONV_DIM)
    alog = jnp.zeros((1, LANES), F32).at[0, :H_SSM].set(ssd_a_log[0])
    dsk = jnp.repeat(ssd_d[0], P_SSM).reshape(1, D_SSM)
    ng = ssd_norm[0].reshape(1, D_SSM)
    ssd_cols = dict(z_col=CONV_DIM // D_SSM, xbc_col=0, dt_col=n_main_cd // LANES)
    ssd_p_out, ssd_st_p, conv_p = _ssd(c32, jnp.zeros((bp, CONV_W - 1, CONV_DIM), F32),
                                       jnp.zeros((bp, H_SSM * P_SSM, N_SSM), F32), cw, cb, alog, dsk, ng,
                                       n_seq=bp, n_chunks=n_pt, q=SEQ_TILE, l_valid=l0, row_block0=0, **ssd_cols)
    ssd_s_out, ssd_st_s, conv_s = _ssd(c32, state_conv[0], state_ssd[0].reshape(bs, H_SSM * P_SSM, N_SSM),
                                       cw, cb, alog, dsk, ng, n_seq=bs, n_chunks=1, q=ls, l_valid=ls,
                                       row_block0=bp * lp // ls, **ssd_cols)
    ssd_out = jnp.concatenate([ssd_p_out, ssd_s_out], axis=0)

    qcol_cd = o_q // LANES
    sb_p = _attention("sb", c16, c16, c16, None, None, n_seq=bp, n_qt=n_pt, tq=SEQ_TILE, tk=SEQ_TILE,
                      q_off=0, lk=lp, q_row_block0=0, q_col_block0=qcol_cd,
                      kv_flat_col_block0=(qcol_cd + D_ATT // LANES, qcol_cd + 2 * D_ATT // LANES))
    k_s = with_past(cache_sb_k[0], c16[s_rows, o_q + D_ATT:o_q + 2 * D_ATT], BF16)
    v_s = with_past(cache_sb_v[0], c16[s_rows, o_q + 2 * D_ATT:n_main_cd], BF16)
    sb_s = _attention("sb", c16, k_s, v_s, None, None, n_seq=bs, n_qt=1, tq=ls, tk=SEQ_TILE,
                      q_off=past, lk=lk_s, q_row_block0=bp * lp // ls, q_col_block0=qcol_cd,
                      kv_flat_col_block0=None)
    sb = jnp.concatenate([sb_p, sb_s], axis=0)

    h = _mixout(h, g_mix_post, 1, w16(cd_w_out[0]), sb, tm, ya=ssd_out)
    h = _ffn_half(h, *ffn2, 1, tm, tf)

    ssd_p = ssd_st_p.reshape(1, bp, H_SSM, P_SSM, N_SSM)
    ssd_s = ssd_st_s.reshape(1, bs, H_SSM, P_SSM, N_SSM)
    sb_k_p, sb_v_p = heads(prompt_rows(kc)), heads(prompt_rows(vc))
    sb_k_s, sb_v_s = heads(sample_rows(kc)), heads(sample_rows(vc))

    y_prompt = h[:bp * lp].reshape(bp, lp, d)[:, N_META:l0]
    y_sample = h[bp * lp:].reshape(bs, ls, d)
    return (y_prompt, y_sample,
            s5_re_p, s5_im_p, fox_k_p, fox_v_p, fox_logf_p, ssd_p, conv_p[None], sb_k_p, sb_v_p,
            s5_re_s, s5_im_s, fox_k_s, fox_v_s, fox_logf_s, ssd_s, conv_s[None], sb_k_s, sb_v_s)
```

```python
import functools

import jax
import jax.numpy as jnp
from jax import lax
from jax.experimental import pallas as pl
from jax.experimental.pallas import tpu as pltpu

F32 = jnp.float32
BF16 = jnp.bfloat16

EPS = 1e-6
D_MODEL = 1024
DH = 64
N_META = 16
D_S5 = 512
S5_GROUP = 16
G_S5 = 32
N_S5 = 64
S5_W = G_S5 * N_S5
H_ATT = 8
D_ATT = H_ATT * DH
D_SSM = 512
P_SSM = 64
H_SSM = 8
G_SSM = 2
N_SSM = 128
CONV_W = 4
CONV_DIM = D_SSM + 2 * G_SSM * N_SSM

LANES = 128
SUBLANES = 8
SEQ_TILE = 256
PROJ_TN = 256
VMEM_LIMIT = 56 * 1024 * 1024

NT_DIMS = (((1,), (1,)), ((), ()))
EXP_ZERO = 110.0
M_INIT = -1e30


def _cparams(sem):
    return pltpu.CompilerParams(dimension_semantics=sem, vmem_limit_bytes=VMEM_LIMIT)


def _rms(x, g):
    return x * lax.rsqrt(jnp.mean(x * x, axis=-1, keepdims=True) + EPS) * g


def _sigmoid(x):
    return 1.0 / (1.0 + jnp.exp(-x))


def _softplus(x):
    return jnp.maximum(x, 0.0) + jnp.log1p(jnp.exp(-jnp.abs(x)))


def _ones_where(mask):
    return jnp.where(mask, 1.0, 0.0).astype(BF16)


def _split3(x):
    hi = x.astype(BF16)
    r1 = x - hi.astype(F32)
    mid = r1.astype(BF16)
    lo = (r1 - mid.astype(F32)).astype(BF16)
    return hi, mid, lo


def _split3_dot(tri, x):
    return sum(jnp.dot(tri, part, preferred_element_type=F32) for part in _split3(x))


def _split3_transpose(eye, x):
    return sum(lax.dot_general(eye, part, NT_DIMS, preferred_element_type=F32) for part in _split3(x))


def _ffn_kernel(h_ref, gpre_ref, gpost_ref, wg_ref, wu_ref, wd_ref, o_ref, xn_ref, acc_ref, *, nj):
    j = pl.program_id(1)

    @pl.when(j == 0)
    def _():
        xn_ref[...] = _rms(h_ref[...], gpre_ref[...]).astype(BF16)
        acc_ref[...] = jnp.zeros_like(acc_ref)

    xn = xn_ref[...]
    g = jnp.dot(xn, wg_ref[...].astype(BF16), preferred_element_type=F32)
    u = jnp.dot(xn, wu_ref[...].astype(BF16), preferred_element_type=F32)
    a = (g * _sigmoid(g)) * u
    acc_ref[...] += jnp.dot(a.astype(BF16), wd_ref[...].astype(BF16), preferred_element_type=F32)

    @pl.when(j == nj - 1)
    def _():
        o_ref[...] = h_ref[...] + 0.5 * _rms(acc_ref[...], gpost_ref[...])


def _ffn_half(h, gpre, gpost, wg, wu, wd, layer, tm, tf):
    t, d = h.shape
    ff = wg.shape[-1]
    nj = ff // tf
    return pl.pallas_call(
        functools.partial(_ffn_kernel, nj=nj),
        name="ffn_half",
        grid=(t // tm, nj),
        in_specs=[
            pl.BlockSpec((tm, d), lambda i, j: (i, 0)),
            pl.BlockSpec((None, 1, d), lambda i, j: (layer, 0, 0)),
            pl.BlockSpec((None, 1, d), lambda i, j: (layer, 0, 0)),
            pl.BlockSpec((None, d, tf), lambda i, j: (layer, 0, j)),
            pl.BlockSpec((None, d, tf), lambda i, j: (layer, 0, j)),
            pl.BlockSpec((None, tf, d), lambda i, j: (layer, j, 0)),
        ],
        out_specs=pl.BlockSpec((tm, d), lambda i, j: (i, 0)),
        out_shape=jax.ShapeDtypeStruct((t, d), F32),
        scratch_shapes=[pltpu.VMEM((tm, d), BF16), pltpu.VMEM((tm, d), F32)],
        compiler_params=_cparams(("parallel", "arbitrary")),
    )(h, gpre, gpost, wg, wu, wd)


def _inproj_kernel(h_ref, g_ref, w_ref, b_ref, o32_ref, o16_ref, xn_ref, *, nj, tn, tail):
    j = pl.program_id(1)

    @pl.when(j == 0)
    def _():
        xn_ref[...] = _rms(h_ref[...], g_ref[...]).astype(BF16)

    p = jnp.dot(xn_ref[...], w_ref[...], preferred_element_type=F32)
    o32_ref[...] = p
    o16_ref[...] = p.astype(BF16)

    @pl.when(j == nj - 1)
    def _():
        x = p[:, tn - PROJ_TN:] + b_ref[...]
        r = -_softplus(-x) if tail == "log_sigmoid" else _softplus(x)
        o32_ref[:, tn - PROJ_TN:] = r
        o16_ref[:, tn - PROJ_TN:] = r.astype(BF16)


def _inproj(h, g, w, gate_bias, layer, tm, tail):
    t, d = h.shape
    n = w.shape[-1]
    nj = 2
    tn = n // nj
    assert tn % LANES == 0 and tn >= PROJ_TN
    return pl.pallas_call(
        functools.partial(_inproj_kernel, nj=nj, tn=tn, tail=tail),
        name="mix_inproj",
        grid=(t // tm, nj),
        in_specs=[
            pl.BlockSpec((tm, d), lambda i, j: (i, 0)),
            pl.BlockSpec((None, 1, d), lambda i, j: (layer, 0, 0)),
            pl.BlockSpec((d, tn), lambda i, j: (0, j)),
            pl.BlockSpec((1, PROJ_TN), lambda i, j: (0, 0)),
        ],
        out_specs=[pl.BlockSpec((tm, tn), lambda i, j: (i, j)),
                   pl.BlockSpec((tm, tn), lambda i, j: (i, j))],
        out_shape=[jax.ShapeDtypeStruct((t, n), F32), jax.ShapeDtypeStruct((t, n), BF16)],
        scratch_shapes=[pltpu.VMEM((tm, d), BF16)],
        compiler_params=_cparams(("parallel", "arbitrary")),
    )(h, g, w, gate_bias)


def _gelu_tanh(x):
    return 0.5 * x * (1.0 + jnp.tanh(0.7978845608028654 * (x + 0.044715 * (x * x * x))))


def _mixout_kernel(*refs, s5_glu):
    if s5_glu:
        (ya_ref, u_ref, dsk_ref, wglu_ref, bglu_ref, yb_ref, wo_ref, h_ref, gpost_ref, o_ref) = refs
        y = ya_ref[...] + dsk_ref[...] * u_ref[...]
        g = _gelu_tanh(y)
        gate = _sigmoid(jnp.dot(g.astype(BF16), wglu_ref[...], preferred_element_type=F32) + bglu_ref[...])
        a = (g * gate).astype(BF16)
    else:
        (ya_ref, yb_ref, wo_ref, h_ref, gpost_ref, o_ref) = refs
        a = ya_ref[...]
    half = a.shape[-1]
    out = (jnp.dot(a, wo_ref[:half, :], preferred_element_type=F32)
           + jnp.dot(yb_ref[...], wo_ref[half:, :], preferred_element_type=F32))
    o_ref[...] = h_ref[...] + _rms(out, gpost_ref[...])


def _mixout(h, gpost, layer, wo, yb, tm, *, ya, u=None, dsk=None, wglu=None, bglu=None):
    t, d = h.shape
    half = yb.shape[-1]
    s5_glu = u is not None
    row = lambda i: (i, 0)
    const = lambda i: (0, 0)
    args, specs = [ya], [pl.BlockSpec((tm, half), row)]
    if s5_glu:
        args += [u, dsk, wglu, bglu]
        specs += [pl.BlockSpec((tm, half), row), pl.BlockSpec((1, half), const),
                  pl.BlockSpec((half, half), const), pl.BlockSpec((1, half), const)]
    args += [yb, wo, h, gpost]
    specs += [pl.BlockSpec((tm, half), row), pl.BlockSpec((2 * half, d), const),
              pl.BlockSpec((tm, d), row), pl.BlockSpec((None, 1, d), lambda i: (layer, 0, 0))]
    return pl.pallas_call(
        functools.partial(_mixout_kernel, s5_glu=s5_glu),
        name="mix_out",
        grid=(t // tm,),
        in_specs=specs,
        out_specs=pl.BlockSpec((tm, d), row),
        out_shape=jax.ShapeDtypeStruct((t, d), F32),
        compiler_params=_cparams(("parallel",)),
    )(*args)


def _s5_prep_kernel(are_ref, aim_ref, ldt_ref, bre_ref, bim_ref, cre_ref, cim_ref,
                    tab_ref, bf_ref, cf_ref):
    a_re = are_ref[...]
    a_im = aim_ref[...]
    dt = jnp.exp(ldt_ref[...])
    mag = jnp.exp(dt * a_re)
    ab_re = mag * jnp.cos(dt * a_im)
    ab_im = mag * jnp.sin(dt * a_im)
    den = a_re * a_re + a_im * a_im
    nr = ab_re - 1.0
    coef_re = (nr * a_re + ab_im * a_im) / den
    coef_im = (ab_im * a_re - nr * a_im) / den

    pw_re, pw_im = [ab_re], [ab_im]
    for _ in range(SUBLANES - 1):
        pr, pi = pw_re[-1], pw_im[-1]
        pw_re.append(pr * ab_re - pi * ab_im)
        pw_im.append(pr * ab_im + pi * ab_re)
    sub = lax.broadcasted_iota(jnp.int32, (SUBLANES, S5_W), 0)
    for k in range(3):
        sh = 1 << k
        tab_ref[k] = jnp.where(sub >= sh, pw_re[sh - 1], 0.0)
        tab_ref[3 + k] = jnp.where(sub >= sh, pw_im[sh - 1], 0.0)
    q_re = jnp.zeros((SUBLANES, S5_W), F32)
    q_im = jnp.zeros((SUBLANES, S5_W), F32)
    for r in range(SUBLANES):
        q_re = jnp.where(sub == r, pw_re[r], q_re)
        q_im = jnp.where(sub == r, pw_im[r], q_im)
    tab_ref[6] = q_re
    tab_ref[7] = q_im

    rg = lax.broadcasted_iota(jnp.int32, (D_S5, S5_W), 0) // S5_GROUP
    cg = lax.broadcasted_iota(jnp.int32, (D_S5, S5_W), 1) // N_S5
    b_re = bre_ref[...]
    b_im = bim_ref[...]
    bb_re = coef_re * b_re - coef_im * b_im
    bb_im = coef_re * b_im + coef_im * b_re
    bf_ref[:, :S5_W] = jnp.where(rg == cg, bb_re, 0.0).astype(BF16)
    bf_ref[:, S5_W:] = jnp.where(rg == cg, bb_im, 0.0).astype(BF16)
    rg2 = lax.broadcasted_iota(jnp.int32, (S5_W, D_S5), 0) // N_S5
    cg2 = lax.broadcasted_iota(jnp.int32, (S5_W, D_S5), 1) // S5_GROUP
    cf_ref[:S5_W, :] = jnp.where(rg2 == cg2, cre_ref[...], 0.0).astype(BF16)
    cf_ref[S5_W:, :] = jnp.where(rg2 == cg2, -cim_ref[...], 0.0).astype(BF16)


def _s5_prep(a_re, a_im, log_dt, b_re, b_im, c_re, c_im):
    flat = lambda x: x.reshape(1, S5_W)
    ldt = jnp.broadcast_to(log_dt[:, None], (G_S5, N_S5))
    b_t = lambda b: jnp.tile(jnp.transpose(b, (2, 0, 1)).reshape(S5_GROUP, S5_W), (G_S5, 1))
    c_t = lambda c: jnp.tile(jnp.transpose(c, (0, 2, 1)).reshape(S5_W, S5_GROUP), (1, G_S5))
    return pl.pallas_call(
        _s5_prep_kernel,
        name="s5_prep",
        out_shape=[jax.ShapeDtypeStruct((8, SUBLANES, S5_W), F32),
                   jax.ShapeDtypeStruct((D_S5, 2 * S5_W), BF16),
                   jax.ShapeDtypeStruct((2 * S5_W, D_S5), BF16)],
        compiler_params=pltpu.CompilerParams(vmem_limit_bytes=VMEM_LIMIT),
    )(flat(a_re), flat(a_im), flat(ldt), b_t(b_re), b_t(b_im), c_t(c_re), c_t(c_im))


S5_CB = 256


def _s5_kernel(u_ref, bf_ref, cf_ref, tab_ref, h0_ref, y_ref, st_ref, bu_ref, car_ref,
               *, rows, seg, chain, last_tile, last_row):
    i = pl.program_id(1)
    w = S5_W
    bu_ref[...] = jnp.dot(u_ref[...], bf_ref[...], preferred_element_type=F32)
    if chain:
        @pl.when(i == 0)
        def _():
            car_ref[...] = h0_ref[...]

    for c in range(0, w, S5_CB):
        re_cols = slice(c, c + S5_CB)
        im_cols = slice(w + c, w + c + S5_CB)
        pr = [tab_ref[k, :, re_cols] for k in range(3)]
        pi = [tab_ref[3 + k, :, re_cols] for k in range(3)]
        qr = tab_ref[6, :, re_cols]
        qi = tab_ref[7, :, re_cols]
        for sg in range(rows // seg):
            src = car_ref if chain else h0_ref
            srow = 0 if chain else sg
            cr0 = src[srow:srow + 1, re_cols]
            ci0 = src[srow:srow + 1, im_cols]

            def body(a, carry, sg=sg, re_cols=re_cols, im_cols=im_cols, pr=pr, pi=pi, qr=qr, qi=qi):
                cr, ci = carry
                r0 = pl.multiple_of(sg * seg + a * SUBLANES, SUBLANES)
                xr = bu_ref[pl.ds(r0, SUBLANES), re_cols]
                xi = bu_ref[pl.ds(r0, SUBLANES), im_cols]
                for k in range(3):
                    sr = pltpu.roll(xr, 1 << k, 0)
                    si = pltpu.roll(xi, 1 << k, 0)
                    xr, xi = xr + pr[k] * sr - pi[k] * si, xi + pr[k] * si + pi[k] * sr
                xr, xi = xr + qr * cr - qi * ci, xi + qr * ci + qi * cr
                bu_ref[pl.ds(r0, SUBLANES), re_cols] = xr
                bu_ref[pl.ds(r0, SUBLANES), im_cols] = xi
                return xr[SUBLANES - 1:SUBLANES, :], xi[SUBLANES - 1:SUBLANES, :]

            cr, ci = lax.fori_loop(0, seg // SUBLANES, body, (cr0, ci0), unroll=2)
            if chain:
                car_ref[0:1, re_cols] = cr
                car_ref[0:1, im_cols] = ci
            else:
                st_ref[sg:sg + 1, re_cols] = cr
                st_ref[sg:sg + 1, im_cols] = ci

    y_ref[...] = jnp.dot(bu_ref[...].astype(BF16), cf_ref[...], preferred_element_type=F32)
    if chain:
        @pl.when(i == last_tile)
        def _():
            st_ref[...] = bu_ref[last_row:last_row + 1, :]


def _s5_scan(proj16, bfull, cfull, tab, h0, *, row_block0, n_groups, n_tiles, seg, chain, l_valid):
    rows = SEQ_TILE
    s = h0.shape[1]
    last = l_valid - 1
    kern = functools.partial(_s5_kernel, rows=rows, seg=seg, chain=chain,
                             last_tile=last // rows, last_row=last % rows)
    if chain:
        umap = lambda b, i: (row_block0 + b * n_tiles + i, 0)
        hmap = lambda b, i: (b, 0, 0)
    else:
        umap = lambda b, i: (row_block0 + i, 0)
        hmap = lambda b, i: (i, 0, 0)
    const2 = lambda b, i: (0, 0)
    return pl.pallas_call(
        kern,
        name="s5_scan",
        grid=(n_groups, n_tiles),
        in_specs=[
            pl.BlockSpec((rows, D_S5), umap),
            pl.BlockSpec((D_S5, 2 * S5_W), const2),
            pl.BlockSpec((2 * S5_W, D_S5), const2),
            pl.BlockSpec((8, SUBLANES, S5_W), lambda b, i: (0, 0, 0)),
            pl.BlockSpec((None, s, 2 * S5_W), hmap),
        ],
        out_specs=[pl.BlockSpec((rows, D_S5), lambda b, i: (b * n_tiles + i, 0)),
                   pl.BlockSpec((None, s, 2 * S5_W), hmap)],
        out_shape=[jax.ShapeDtypeStruct((n_groups * n_tiles * rows, D_S5), F32),
                   jax.ShapeDtypeStruct(h0.shape, F32)],
        scratch_shapes=[pltpu.VMEM((rows, 2 * S5_W), F32), pltpu.VMEM((1, 2 * S5_W), F32)],
        compiler_params=_cparams(("arbitrary", "arbitrary")),
    )(proj16, bfull, cfull, tab, h0)


def _cumsum_kernel(x_ref, o_ref, pmin_ref, car_ref, min_ref, *, tk):
    j = pl.program_id(0)

    @pl.when(j == 0)
    def _():
        car_ref[...] = jnp.zeros_like(car_ref)
        min_ref[...] = jnp.full_like(min_ref, jnp.inf)

    r = lax.broadcasted_iota(jnp.int32, (tk, tk), 0)
    c = lax.broadcasted_iota(jnp.int32, (tk, tk), 1)
    tri = _ones_where(r <= c)
    acc = sum(jnp.dot(part, tri, preferred_element_type=F32) for part in _split3(x_ref[...])) + car_ref[...]
    o_ref[...] = acc
    car_ref[...] = acc[:, tk - 1:tk]
    run_min = jnp.minimum(min_ref[...], jnp.min(acc, axis=-1, keepdims=True))
    min_ref[...] = run_min
    pmin_ref[...] = run_min


def _time_cumsum(x):
    b, hh, length = x.shape
    tk = SEQ_TILE
    rows = b * hh
    fcum, pmin = pl.pallas_call(
        functools.partial(_cumsum_kernel, tk=tk),
        name="time_cumsum",
        grid=(length // tk,),
        in_specs=[pl.BlockSpec((rows, tk), lambda j: (0, j))],
        out_specs=[pl.BlockSpec((rows, tk), lambda j: (0, j)),
                   pl.BlockSpec((None, rows, 1), lambda j: (j, 0, 0))],
        out_shape=[jax.ShapeDtypeStruct((rows, length), F32),
                   jax.ShapeDtypeStruct((length // tk, rows, 1), F32)],
        scratch_shapes=[pltpu.VMEM((rows, 1), F32), pltpu.VMEM((rows, 1), F32)],
        compiler_params=_cparams(("arbitrary",)),
    )(x.reshape(rows, length))
    return fcum.reshape(b, hh, length), pmin.reshape(length // tk, b, hh, 1)


def _head_pair_queries(q_ref):
    lane = lax.broadcasted_iota(jnp.int32, (1, LANES), 1)
    q = q_ref[...] * jnp.asarray(DH ** -0.5, BF16)
    zero = jnp.zeros_like(q)
    return lane, (jnp.where(lane < DH, q, zero), jnp.where(lane >= DH, q, zero))


def _fox_kernel(*refs, tq, tk, q_off, prune):
    if prune:
        q_ref, k_ref, v_ref, fq_ref, fk_ref, fmin_ref, o_ref, m_ref, l_ref, acc_ref, kabs_ref = refs
    else:
        q_ref, k_ref, v_ref, fq_ref, fk_ref, o_ref, m_ref, l_ref, acc_ref = refs
    i = pl.program_id(2)
    lane, qs = _head_pair_queries(q_ref)
    qpos0 = q_off + i * tq
    n_full = (qpos0 + 1) // tk
    j_last = (qpos0 + tq - 1) // tk
    m_ref[...] = jnp.full_like(m_ref, M_INIT)
    l_ref[...] = jnp.zeros_like(l_ref)
    acc_ref[...] = jnp.zeros_like(acc_ref)
    if prune:
        @pl.when(i == 0)
        def _():
            kabs_ref[...] = jnp.max(jnp.abs(k_ref[...].astype(F32)), axis=0, keepdims=True)

        kabs = kabs_ref[...]
        slack = [jnp.sum(jnp.abs(qs[h].astype(F32)) * kabs, axis=-1, keepdims=True) + fq_ref[:, h:h + 1]
                 for h in range(2)]

        def worth_visiting(j):
            jc = jnp.maximum(j, 0)
            gap = jnp.maximum(jnp.max(slack[0] - m_ref[0] - fmin_ref[jc, 0:1, :]),
                              jnp.max(slack[1] - m_ref[1] - fmin_ref[jc, 1:2, :]))
            return gap > -EXP_ZERO

    def step(j, masked):
        ks = pl.multiple_of(j * tk, tk)
        kj = k_ref[pl.ds(ks, tk), :]
        vj = v_ref[pl.ds(ks, tk), :]
        if masked:
            kpos = ks + lax.broadcasted_iota(jnp.int32, (tq, tk), 1)
            qpos = qpos0 + lax.broadcasted_iota(jnp.int32, (tq, tk), 0)
            vis = kpos <= qpos
        for h in range(2):
            s = lax.dot_general(qs[h], kj, NT_DIMS, preferred_element_type=F32)
            s = s + fq_ref[:, h:h + 1] - fk_ref[j, h:h + 1, :]
            if masked:
                s = jnp.where(vis, s, -jnp.inf)
            m_prev = m_ref[h]
            m_new = jnp.maximum(m_prev, jnp.max(s, axis=-1, keepdims=True))
            alpha = jnp.exp(m_prev - m_new)
            p = jnp.exp(s - m_new)
            l_ref[h] = alpha * l_ref[h] + jnp.sum(p, axis=-1, keepdims=True)
            acc_ref[h] = alpha * acc_ref[h] + jnp.dot(p.astype(BF16), vj, preferred_element_type=F32)
            m_ref[h] = m_new

    def part_body(t, c):
        step(j_last - t, True)
        return c

    lax.fori_loop(0, j_last + 1 - n_full, part_body, 0)
    if prune:
        def more(c):
            t, go = c
            return jnp.logical_and(t < n_full, go)

        def full_body(c):
            t, _ = c
            step(n_full - 1 - t, False)
            return t + 1, worth_visiting(n_full - 2 - t)

        lax.while_loop(more, full_body, (0, worth_visiting(n_full - 1)))
    else:
        def full_body(t, c):
            step(n_full - 1 - t, False)
            return c

        lax.fori_loop(0, n_full, full_body, 0)
    o = jnp.where(lane < DH, acc_ref[0] / l_ref[0], acc_ref[1] / l_ref[1])
    o_ref[...] = o.astype(BF16)


def _fox_t_kernel(q_ref, k_ref, v_ref, fqr_ref, fkc_ref, fmin_ref, o_ref,
                  m_ref, l_ref, acc_ref, vt_ref, fkb_ref, kabs_ref, *, tq, tk, n_kt):
    i = pl.program_id(2)
    lane, qs = _head_pair_queries(q_ref)
    qpos0 = i * tq
    n_full = (qpos0 + 1) // tk
    j_last = (qpos0 + tq - 1) // tk
    rep = tq // LANES

    @pl.when(i == 0)
    def _():
        kabs_ref[...] = jnp.broadcast_to(jnp.max(jnp.abs(k_ref[...]), axis=0, keepdims=True), kabs_ref.shape)
        er = lax.broadcasted_iota(jnp.int32, (LANES, LANES), 0)
        ec = lax.broadcasted_iota(jnp.int32, (LANES, LANES), 1)
        eye = _ones_where(er == ec)
        for h in range(2):
            fkb_ref[h] = jnp.broadcast_to(fkc_ref[:, h:h + 1], fkb_ref.shape[1:])

        def transpose_tile(j, c):
            vj = v_ref[pl.ds(pl.multiple_of(j * tk, tk), tk), :]
            vt_ref[j] = lax.dot_general(eye, vj, NT_DIMS, preferred_element_type=F32).astype(BF16)
            return c

        lax.fori_loop(0, n_kt, transpose_tile, 0)

    m_ref[...] = jnp.full_like(m_ref, M_INIT)
    l_ref[...] = jnp.zeros_like(l_ref)
    acc_ref[...] = jnp.zeros_like(acc_ref)
    kabs = kabs_ref[...]
    slack = [lax.dot_general(kabs, jnp.abs(qs[h]), NT_DIMS, preferred_element_type=F32)[0:1, :]
             + fqr_ref[h:h + 1, :] for h in range(2)]

    def worth_visiting(j):
        jc = jnp.maximum(j, 0)
        gap = jnp.maximum(jnp.max(slack[0] - m_ref[0] - fmin_ref[jc, 0:1, :]),
                          jnp.max(slack[1] - m_ref[1] - fmin_ref[jc, 1:2, :]))
        return gap > -EXP_ZERO

    def step(j, masked):
        ks = pl.multiple_of(j * tk, tk)
        kj = k_ref[pl.ds(ks, tk), :]
        if masked:
            kpos = ks + lax.broadcasted_iota(jnp.int32, (tk, tq), 0)
            qpos = qpos0 + lax.broadcasted_iota(jnp.int32, (tk, tq), 1)
            vis = kpos <= qpos
        for h in range(2):
            s = lax.dot_general(kj, qs[h], NT_DIMS, preferred_element_type=F32)
            fk = fkb_ref[h, pl.ds(ks, tk), :]
            s = s + fqr_ref[h:h + 1, :] - jnp.concatenate([fk] * rep, axis=1)
            if masked:
                s = jnp.where(vis, s, -jnp.inf)
            m_prev = m_ref[h]
            m_new = jnp.maximum(m_prev, jnp.max(s, axis=0, keepdims=True))
            alpha = jnp.exp(m_prev - m_new)
            p = jnp.exp(s - m_new)
            l_ref[h] = alpha * l_ref[h] + jnp.sum(p, axis=0, keepdims=True)
            rows = slice(h * DH, (h + 1) * DH)
            acc_ref[rows, :] = alpha * acc_ref[rows, :] + jnp.dot(vt_ref[j, rows, :], p.astype(BF16),
                                                                  preferred_element_type=F32)
            m_ref[h] = m_new

    def part_body(t, c):
        step(j_last - t, True)
        return c

    def more(c):
        t, go = c
        return jnp.logical_and(t < n_full, go)

    def full_body(c):
        t, _ = c
        step(n_full - 1 - t, False)
        return t + 1, worth_visiting(n_full - 2 - t)

    lax.fori_loop(0, j_last + 1 - n_full, part_body, 0)
    lax.while_loop(more, full_body, (0, worth_visiting(n_full - 1)))
    o_t = jnp.concatenate([acc_ref[0:DH, :] / l_ref[0], acc_ref[DH:, :] / l_ref[1]], axis=0).astype(BF16)
    qr = lax.broadcasted_iota(jnp.int32, (tq, tq), 0)
    qc = lax.broadcasted_iota(jnp.int32, (tq, tq), 1)
    o_ref[...] = lax.dot_general(_ones_where(qr == qc), o_t, NT_DIMS,
                                 preferred_element_type=F32).astype(BF16)


def _fox_prompt(p16, fqr, fkc, fmin, *, n_seq, n_qt, lk, q_col_block0, k_col_block0, v_col_block0):
    tq = tk = SEQ_TILE
    n_hg = H_ATT // 2
    n_kt = lk // tk
    return pl.pallas_call(
        functools.partial(_fox_t_kernel, tq=tq, tk=tk, n_kt=n_kt),
        name="fox_attention_t",
        grid=(n_seq, n_hg, n_qt),
        in_specs=[
            pl.BlockSpec((tq, LANES), lambda b, g, i: (b * n_qt + i, q_col_block0 + g)),
            pl.BlockSpec((lk, LANES), lambda b, g, i: (b, k_col_block0 + g)),
            pl.BlockSpec((lk, LANES), lambda b, g, i: (b, v_col_block0 + g)),
            pl.BlockSpec((None, None, 2, tq), lambda b, g, i: (b, g, 0, i)),
            pl.BlockSpec((None, None, lk, 2), lambda b, g, i: (b, g, 0, 0)),
            pl.BlockSpec((None, None, n_kt, 2, 1), lambda b, g, i: (b, g, 0, 0, 0)),
        ],
        out_specs=pl.BlockSpec((tq, LANES), lambda b, g, i: (b * n_qt + i, g)),
        out_shape=jax.ShapeDtypeStruct((n_seq * n_qt * tq, D_ATT), BF16),
        scratch_shapes=[pltpu.VMEM((2, 1, tq), F32), pltpu.VMEM((2, 1, tq), F32), pltpu.VMEM((LANES, tq), F32),
                        pltpu.VMEM((n_kt, LANES, tk), BF16), pltpu.VMEM((2, lk, LANES), F32),
                        pltpu.VMEM((2 * SUBLANES, LANES), BF16)],
        compiler_params=_cparams(("parallel", "parallel", "arbitrary")),
    )(p16, p16, p16, fqr, fkc, fmin)


def _sb_kernel(q_ref, k_ref, v_ref, o_ref, r_ref, acc_ref, *, tq, tk, q_off):
    i = pl.program_id(2)
    lane, qs = _head_pair_queries(q_ref)
    qpos0 = q_off + i * tq
    n_full = qpos0 // tk
    j_last = (qpos0 + tq - 1) // tk
    r_ref[...] = jnp.zeros_like(r_ref)
    acc_ref[...] = jnp.zeros_like(acc_ref)
    rr = lax.broadcasted_iota(jnp.int32, (tk, tk), 0)
    cc = lax.broadcasted_iota(jnp.int32, (tk, tk), 1)
    tri = _ones_where(rr > cc)

    def step(j, masked):
        ks = pl.multiple_of(j * tk, tk)
        kj = k_ref[pl.ds(ks, tk), :]
        vj = v_ref[pl.ds(ks, tk), :]
        if masked:
            kpos = ks + lax.broadcasted_iota(jnp.int32, (tq, tk), 1)
            qpos = qpos0 + lax.broadcasted_iota(jnp.int32, (tq, tk), 0)
            vis = kpos < qpos
        for h in range(2):
            z = lax.dot_general(qs[h], kj, NT_DIMS, preferred_element_type=F32)
            sp = jnp.maximum(z, 0.0) + jnp.log(1.0 + jnp.exp(-jnp.abs(z)))
            log_keep = -sp
            if masked:
                log_keep = jnp.where(vis, log_keep, 0.0)
            hi = log_keep.astype(BF16)
            lo = (log_keep - hi.astype(F32)).astype(BF16)
            within = (jnp.dot(hi, tri, preferred_element_type=F32)
                      + jnp.dot(lo, tri, preferred_element_type=F32))
            run = r_ref[h]
            wgt = jnp.exp((z - sp) + (within + run))
            if masked:
                wgt = jnp.where(vis, wgt, 0.0)
            acc_ref[h] += jnp.dot(wgt.astype(BF16), vj, preferred_element_type=F32)
            r_ref[h] = run + within[:, 0:1] + log_keep[:, 0:1]

    def part_body(t, c):
        step(j_last - t, True)
        return c

    def worth_visiting():
        return jnp.max(jnp.maximum(r_ref[0], r_ref[1])) > -EXP_ZERO

    def more(c):
        t, go = c
        return jnp.logical_and(t < n_full, go)

    def full_body(c):
        t, _ = c
        step(n_full - 1 - t, False)
        return t + 1, worth_visiting()

    lax.fori_loop(0, j_last + 1 - n_full, part_body, 0)
    lax.while_loop(more, full_body, (0, worth_visiting()))
    o_ref[...] = jnp.where(lane < DH, acc_ref[0], acc_ref[1]).astype(BF16)


def _attention(kind, q_arr, k_arr, v_arr, fq, fk, fmin=None, *, n_seq, n_qt, tq, tk, q_off, lk,
               q_row_block0, q_col_block0, kv_flat_col_block0):
    n_hg = H_ATT // 2
    qmap = lambda b, g, i: (q_row_block0 + b * n_qt + i, q_col_block0 + g)
    if kv_flat_col_block0 is None:
        kspec = pl.BlockSpec((None, lk, LANES), lambda b, g, i: (b, 0, g))
        vspec = kspec
    else:
        kcol, vcol = kv_flat_col_block0
        kspec = pl.BlockSpec((lk, LANES), lambda b, g, i: (b, kcol + g))
        vspec = pl.BlockSpec((lk, LANES), lambda b, g, i: (b, vcol + g))
    in_specs = [pl.BlockSpec((tq, LANES), qmap), kspec, vspec]
    args = [q_arr, k_arr, v_arr]
    if kind == "fox":
        prune = fmin is not None
        in_specs += [pl.BlockSpec((None, None, tq, 2), lambda b, g, i: (b, g, i, 0)),
                     pl.BlockSpec((None, None, lk // tk, 2, tk), lambda b, g, i: (b, g, 0, 0, 0))]
        args += [fq, fk]
        scratch = [pltpu.VMEM((2, tq, 1), F32), pltpu.VMEM((2, tq, 1), F32), pltpu.VMEM((2, tq, LANES), F32)]
        if prune:
            in_specs.append(pl.BlockSpec((None, None, lk // tk, 2, 1), lambda b, g, i: (b, g, 0, 0, 0)))
            args.append(fmin)
            scratch.append(pltpu.VMEM((1, LANES), F32))
        kern = functools.partial(_fox_kernel, tq=tq, tk=tk, q_off=q_off, prune=prune)
    else:
        kern = functools.partial(_sb_kernel, tq=tq, tk=tk, q_off=q_off)
        scratch = [pltpu.VMEM((2, tq, 1), F32), pltpu.VMEM((2, tq, LANES), F32)]
    return pl.pallas_call(
        kern,
        name=kind + "_attention",
        grid=(n_seq, n_hg, n_qt),
        in_specs=in_specs,
        out_specs=pl.BlockSpec((tq, LANES), lambda b, g, i: (b * n_qt + i, g)),
        out_shape=jax.ShapeDtypeStruct((n_seq * n_qt * tq, D_ATT), BF16),
        scratch_shapes=scratch,
        compiler_params=_cparams(("parallel", "parallel", "arbitrary")),
    )(*args)


def _ssd_kernel(z_ref, xbc_ref, dt_ref, cst_ref, s0_ref, cw_ref, cb_ref, alog_ref, dsk_ref, ng_ref,
                y_ref, sout_ref, cout_ref, xw_ref, s_ref, ysc_ref,
                *, q, l_valid, n_chunks):
    c = pl.program_id(1)
    hist = CONV_W - 1
    base = SUBLANES

    @pl.when(c == 0)
    def _():
        xw_ref[base - hist:base, :] = cst_ref[...]
        s_ref[...] = s0_ref[...]

    xw_ref[base:base + q, :] = xbc_ref[...]
    conv = cb_ref[...]
    for w in range(CONV_W):
        conv = conv + xw_ref[base - hist + w:base - hist + w + q, :] * cw_ref[w:w + 1, :]
    last = l_valid - 1

    @pl.when(c == last // q)
    def _():
        lr = base + last % q
        cout_ref[...] = xw_ref[lr - hist + 1:lr + 1, :]

    xw_ref[base - hist:base, :] = xw_ref[base + q - hist:base + q, :]

    act = conv * _sigmoid(conv)
    xs = act[:, :D_SSM]
    gw = G_SSM * N_SSM
    bm = act[:, D_SSM:D_SSM + gw].astype(BF16)
    cm = act[:, D_SSM + gw:].astype(BF16)

    rowg = c * q + lax.broadcasted_iota(jnp.int32, (q, LANES), 0)
    dt = jnp.where(rowg < l_valid, dt_ref[...], 0.0)
    adt = dt * (-jnp.exp(alog_ref[...]))
    rr = lax.broadcasted_iota(jnp.int32, (q, q), 0)
    cc = lax.broadcasted_iota(jnp.int32, (q, q), 1)
    causal = rr >= cc
    acs = _split3_dot(_ones_where(causal), adt)
    er = lax.broadcasted_iota(jnp.int32, (LANES, LANES), 0)
    ec = lax.broadcasted_iota(jnp.int32, (LANES, LANES), 1)
    eye = _ones_where(er == ec)
    acs_t = _split3_transpose(eye, acs)
    lane = lax.broadcasted_iota(jnp.int32, (1, LANES), 1)
    first = lane < P_SSM
    srow_first = lax.broadcasted_iota(jnp.int32, (LANES, 1), 0) < P_SSM
    heads_per_group = H_SSM // G_SSM

    cb_mats = []
    for g in range(G_SSM):
        cg = cm[:, g * N_SSM:(g + 1) * N_SSM]
        bg = bm[:, g * N_SSM:(g + 1) * N_SSM]
        cb_mats.append(lax.dot_general(cg, bg, NT_DIMS, preferred_element_type=F32))

    for pr in range(H_SSM // 2):
        h0, h1 = 2 * pr, 2 * pr + 1
        g = h0 // heads_per_group
        cg = cm[:, g * N_SSM:(g + 1) * N_SSM]
        bg = bm[:, g * N_SSM:(g + 1) * N_SSM]
        cols = slice(pr * LANES, (pr + 1) * LANES)
        xs_p = xs[:, cols]
        a0, a1 = acs[:, h0:h0 + 1], acs[:, h1:h1 + 1]
        xdt = xs_p * jnp.where(first, dt[:, h0:h0 + 1], dt[:, h1:h1 + 1])
        y = jnp.zeros((q, LANES), F32)
        for hh, a_col, keep in ((h0, a0, first), (h1, a1, jnp.logical_not(first))):
            seg = a_col - acs_t[hh:hh + 1, :]
            lmat = jnp.where(causal, jnp.exp(jnp.where(causal, seg, 0.0)), 0.0)
            m = (cb_mats[g] * lmat).astype(BF16)
            y = y + jnp.dot(m, jnp.where(keep, xdt, 0.0).astype(BF16), preferred_element_type=F32)
        st = s_ref[cols, :]
        y_off = lax.dot_general(cg, st.astype(BF16), NT_DIMS, preferred_element_type=F32)
        y = y + y_off * jnp.where(first, jnp.exp(a0), jnp.exp(a1))
        e0, e1 = a0[q - 1:q, :], a1[q - 1:q, :]
        wdec = jnp.where(first, jnp.exp(e0 - a0), jnp.exp(e1 - a1))
        xw = lax.dot_general(eye, (xdt * wdec).astype(BF16), NT_DIMS,
                             preferred_element_type=F32).astype(BF16)
        dec_rows = jnp.where(srow_first, jnp.exp(e0), jnp.exp(e1))
        s_ref[cols, :] = dec_rows * st + jnp.dot(xw, bg, preferred_element_type=F32)
        ysc_ref[:, cols] = y + dsk_ref[:, cols] * xs_p

    zt = z_ref[...]
    yg = ysc_ref[...] * (zt * _sigmoid(zt))
    gwid = D_SSM // G_SSM
    for g in range(G_SSM):
        blk = yg[:, g * gwid:(g + 1) * gwid]
        nrm = blk * lax.rsqrt(jnp.mean(blk * blk, axis=-1, keepdims=True) + EPS)
        y_ref[:, g * gwid:(g + 1) * gwid] = (nrm * ng_ref[:, g * gwid:(g + 1) * gwid]).astype(BF16)

    @pl.when(c == n_chunks - 1)
    def _():
        sout_ref[...] = s_ref[...]


def _ssd(proj32, cst, s0, cw, cb, alog, dsk, ng, *, n_seq, n_chunks, q, l_valid, row_block0,
         z_col, xbc_col, dt_col):
    rmap = lambda col: (lambda b, c: (row_block0 + b * n_chunks + c, col))
    const2 = lambda b, c: (0, 0)
    return pl.pallas_call(
        functools.partial(_ssd_kernel, q=q, l_valid=l_valid, n_chunks=n_chunks),
        name="ssd_mixer",
        grid=(n_seq, n_chunks),
        in_specs=[
            pl.BlockSpec((q, D_SSM), rmap(z_col)),
            pl.BlockSpec((q, CONV_DIM), rmap(xbc_col)),
            pl.BlockSpec((q, LANES), rmap(dt_col)),
            pl.BlockSpec((None, CONV_W - 1, CONV_DIM), lambda b, c: (b, 0, 0)),
            pl.BlockSpec((None, H_SSM * P_SSM, N_SSM), lambda b, c: (b, 0, 0)),
            pl.BlockSpec((CONV_W, CONV_DIM), const2),
            pl.BlockSpec((1, CONV_DIM), const2),
            pl.BlockSpec((1, LANES), const2),
            pl.BlockSpec((1, D_SSM), const2),
            pl.BlockSpec((1, D_SSM), const2),
        ],
        out_specs=[pl.BlockSpec((q, D_SSM), lambda b, c: (b * n_chunks + c, 0)),
                   pl.BlockSpec((None, H_SSM * P_SSM, N_SSM), lambda b, c: (b, 0, 0)),
                   pl.BlockSpec((None, CONV_W - 1, CONV_DIM), lambda b, c: (b, 0, 0))],
        out_shape=[jax.ShapeDtypeStruct((n_seq * n_chunks * q, D_SSM), BF16),
                   jax.ShapeDtypeStruct((n_seq, H_SSM * P_SSM, N_SSM), F32),
                   jax.ShapeDtypeStruct((n_seq, CONV_W - 1, CONV_DIM), F32)],
        scratch_shapes=[pltpu.VMEM((q + SUBLANES, CONV_DIM), F32),
                        pltpu.VMEM((H_SSM * P_SSM, N_SSM), F32),
                        pltpu.VMEM((q, D_SSM), F32)],
        compiler_params=_cparams(("arbitrary", "arbitrary")),
    )(proj32, proj32, proj32, cst, s0, cw, cb, alog, dsk, ng)


def _round_up(x, m):
    return (x + m - 1) // m * m


def _pad_cols(w, n):
    return jnp.pad(w, ((0, 0), (0, n - w.shape[1])))


def kernel(x_prompt, x_sample, state_s5_re, state_s5_im, cache_fox_k, cache_fox_v, cache_fox_logf, state_ssd, state_conv, cache_sb_k, cache_sb_v, meta_tokens, norm_ffn1_pre, norm_ffn1_post, norm_mix_pre, norm_mix_post, norm_ffn2_pre, norm_ffn2_post, ffn1_w_gate, ffn1_w_up, ffn1_w_down, ffn2_w_gate, ffn2_w_up, ffn2_w_down, ab_w_in, fox_b_f, s5_a_re, s5_a_im, s5_log_dt, s5_b_re, s5_b_im, s5_c_re, s5_c_im, s5_d, s5_w_glu, s5_b_glu, ab_w_out, cd_w_in, ssd_conv_w, ssd_conv_b, ssd_dt_bias, ssd_a_log, ssd_d, ssd_norm, cd_w_out):
    bp, seq, d = x_prompt.shape
    bs, ls, _ = x_sample.shape
    past = cache_fox_k.shape[2]
    depth = norm_ffn1_pre.shape[0]
    assert d == D_MODEL and depth == 2 and ab_w_in.shape[0] == 1 and cd_w_in.shape[0] == 1
    l0 = N_META + seq
    lp = _round_up(l0, SEQ_TILE)
    n_pt = lp // SEQ_TILE
    rows_s = bs * ls
    assert rows_s % SEQ_TILE == 0 and SEQ_TILE % ls == 0 and ls % 16 == 0 and l0 >= CONV_W
    t = bp * lp + rows_s
    tm = next(c for c in (1024, 512, 256) if t % c == 0)
    tf = 512
    s_blk0 = bp * n_pt
    lk_s = _round_up(past + ls, SEQ_TILE)

    meta = meta_tokens.astype(F32)
    zpad = jnp.zeros((lp - l0, d), F32)
    pieces = []
    for b in range(bp):
        pieces += [meta, x_prompt[b], zpad]
    pieces.append(x_sample.reshape(rows_s, d))
    h = jnp.concatenate(pieces, axis=0)

    g3 = lambda g: g.reshape(depth, 1, d)
    w16 = lambda w: w.astype(BF16)
    ffn1 = (g3(norm_ffn1_pre), g3(norm_ffn1_post), ffn1_w_gate, ffn1_w_up, ffn1_w_down)
    ffn2 = (g3(norm_ffn2_pre), g3(norm_ffn2_post), ffn2_w_gate, ffn2_w_up, ffn2_w_down)
    g_mix_pre, g_mix_post = g3(norm_mix_pre), g3(norm_mix_post)

    def prompt_rows(x):
        return x[:bp * lp].reshape(bp, lp, -1)[:, :l0]

    def sample_rows(x):
        return x[bp * lp:].reshape(bs, ls, -1)

    h = _ffn_half(h, *ffn1, 0, tm, tf)

    n_main = D_S5 + 3 * D_ATT
    n_ab = n_main + PROJ_TN
    w_ab = w16(_pad_cols(ab_w_in[0], n_ab))
    bias_ab = jnp.zeros((1, PROJ_TN), F32).at[0, :H_ATT].set(fox_b_f[0])
    p32, p16 = _inproj(h, g_mix_pre, w_ab, bias_ab, 0, tm, "log_sigmoid")

    logf = p32[:, n_main:n_main + H_ATT]
    kf = p32[:, D_S5 + D_ATT:D_S5 + 2 * D_ATT]
    vf = p32[:, D_S5 + 2 * D_ATT:n_main]

    tab, bfull, cfull = _s5_prep(s5_a_re[0], s5_a_im[0], s5_log_dt[0], s5_b_re[0], s5_b_im[0],
                                 s5_c_re[0], s5_c_im[0])
    h0_p = jnp.zeros((bp, 1, 2 * S5_W), F32)
    y_s5_p, st_p = _s5_scan(p16, bfull, cfull, tab, h0_p, row_block0=0, n_groups=bp, n_tiles=n_pt,
                            seg=SEQ_TILE, chain=True, l_valid=l0)
    spt = SEQ_TILE // ls
    h0_s = jnp.concatenate([state_s5_re[0].reshape(bs, S5_W), state_s5_im[0].reshape(bs, S5_W)], axis=-1)
    y_s5_s, st_s = _s5_scan(p16, bfull, cfull, tab, h0_s.reshape(bs // spt, spt, 2 * S5_W),
                            row_block0=s_blk0, n_groups=1, n_tiles=rows_s // SEQ_TILE,
                            seg=ls, chain=False, l_valid=ls)
    y_s5 = jnp.concatenate([y_s5_p, y_s5_s], axis=0)

    n_hg = H_ATT // 2

    def head_major(x):
        return jnp.transpose(x, (0, 2, 1))

    def fox_bias_layouts(fcum, q_lo, lq, lk, tk):
        b = fcum.shape[0]
        fq = jnp.transpose(fcum[:, :, q_lo:q_lo + lq].reshape(b, n_hg, 2, lq), (0, 1, 3, 2))
        fk = jnp.transpose(fcum.reshape(b, n_hg, 2, lk // tk, tk), (0, 1, 3, 2, 4))
        return fq, fk

    logf_p = p32[:bp * lp, n_main:n_main + H_ATT].reshape(bp, lp, H_ATT)
    fcum_p, fmin_p = _time_cumsum(head_major(logf_p))
    fqr_p = fcum_p.reshape(bp, n_hg, 2, lp)
    fkc_p = jnp.transpose(fqr_p, (0, 1, 3, 2))
    fmin_p = jnp.transpose(fmin_p.reshape(n_pt, bp, n_hg, 2, 1), (1, 2, 0, 3, 4))
    qcol = D_S5 // LANES
    fox_p = _fox_prompt(p16, fqr_p, fkc_p, fmin_p, n_seq=bp, n_qt=n_pt, lk=lp, q_col_block0=qcol,
                        k_col_block0=qcol + D_ATT // LANES, v_col_block0=qcol + 2 * D_ATT // LANES)

    def with_past(past_x, new_x, dtype):
        c = new_x.shape[-1]
        return jnp.concatenate([past_x.reshape(bs, past, c).astype(dtype),
                                new_x.reshape(bs, ls, c).astype(dtype),
                                jnp.zeros((bs, lk_s - past - ls, c), dtype)], axis=1)

    s_rows = slice(bp * lp, t)
    logf_s = with_past(cache_fox_logf[0], logf[s_rows], F32)
    fq_s, fk_s = fox_bias_layouts(_time_cumsum(head_major(logf_s))[0], past, ls, lk_s, lk_s)
    k_s = with_past(cache_fox_k[0], p16[s_rows, D_S5 + D_ATT:D_S5 + 2 * D_ATT], BF16)
    v_s = with_past(cache_fox_v[0], p16[s_rows, D_S5 + 2 * D_ATT:n_main], BF16)
    fox_s = _attention("fox", p16, k_s, v_s, fq_s, fk_s, n_seq=bs, n_qt=1, tq=ls, tk=lk_s,
                       q_off=past, lk=lk_s, q_row_block0=bp * lp // ls, q_col_block0=qcol,
                       kv_flat_col_block0=None)
    fox = jnp.concatenate([fox_p, fox_s], axis=0)

    h = _mixout(h, g_mix_post, 0, w16(ab_w_out[0]), fox, tm, ya=y_s5, u=p32,
                dsk=s5_d[0].reshape(1, D_S5), wglu=w16(s5_w_glu[0]), bglu=s5_b_glu[0].reshape(1, D_S5))
    h = _ffn_half(h, *ffn2, 0, tm, tf)

    def split_state(st):
        return (st[:, :S5_W].reshape(1, -1, G_S5, N_S5), st[:, S5_W:].reshape(1, -1, G_S5, N_S5))

    s5_re_p, s5_im_p = split_state(st_p.reshape(bp, 2 * S5_W))
    s5_re_s, s5_im_s = split_state(st_s.reshape(bs, 2 * S5_W))
    heads = lambda x: x.reshape(1, x.shape[0], x.shape[1], H_ATT, DH)
    fox_k_p, fox_v_p = heads(prompt_rows(kf)), heads(prompt_rows(vf))
    fox_k_s, fox_v_s = heads(sample_rows(kf)), heads(sample_rows(vf))
    fox_logf_p, fox_logf_s = prompt_rows(logf)[None], sample_rows(logf)[None]

    h = _ffn_half(h, *ffn1, 1, tm, tf)

    wc = cd_w_in[0]
    o_xbc, o_dt = D_SSM, D_SSM + CONV_DIM
    o_q = CONV_DIM + D_SSM
    n_main_cd = o_q + 3 * D_ATT
    n_cd = n_main_cd + PROJ_TN
    w_cd = w16(_pad_cols(jnp.concatenate([wc[:, o_xbc:o_dt], wc[:, :o_xbc], wc[:, o_dt + H_SSM:],
                                          wc[:, o_dt:o_dt + H_SSM]], axis=1), n_cd))
    bias_cd = jnp.zeros((1, PROJ_TN), F32).at[0, :H_SSM].set(ssd_dt_bias[0])
    c32, c16 = _inproj(h, g_mix_pre, w_cd, bias_cd, 1, tm, "softplus")
    kc = c32[:, o_q + D_ATT:o_q + 2 * D_ATT]
    vc = c32[:, o_q + 2 * D_ATT:n_main_cd]

    cw = ssd_conv_w[0]
    cb = ssd_conv_b[0].reshape(1, CONV_DIM)
    alog = jnp.zeros((1, LANES), F32).at[0, :H_SSM].set(ssd_a_log[0])
    dsk = jnp.repeat(ssd_d[0], P_SSM).reshape(1, D_SSM)
    ng = ssd_norm[0].reshape(1, D_SSM)
    ssd_cols = dict(z_col=CONV_DIM // D_SSM, xbc_col=0, dt_col=n_main_cd // LANES)
    ssd_p_out, ssd_st_p, conv_p = _ssd(c32, jnp.zeros((bp, CONV_W - 1, CONV_DIM), F32),
                                       jnp.zeros((bp, H_SSM * P_SSM, N_SSM), F32), cw, cb, alog, dsk, ng,
                                       n_seq=bp, n_chunks=n_pt, q=SEQ_TILE, l_valid=l0, row_block0=0, **ssd_cols)
    ssd_s_out, ssd_st_s, conv_s = _ssd(c32, state_conv[0], state_ssd[0].reshape(bs, H_SSM * P_SSM, N_SSM),
                                       cw, cb, alog, dsk, ng, n_seq=bs, n_chunks=1, q=ls, l_valid=ls,
                                       row_block0=bp * lp // ls, **ssd_cols)
    ssd_out = jnp.concatenate([ssd_p_out, ssd_s_out], axis=0)

    qcol_cd = o_q // LANES
    sb_p = _attention("sb", c16, c16, c16, None, None, n_seq=bp, n_qt=n_pt, tq=SEQ_TILE, tk=SEQ_TILE,
                      q_off=0, lk=lp, q_row_block0=0, q_col_block0=qcol_cd,
                      kv_flat_col_block0=(qcol_cd + D_ATT // LANES, qcol_cd + 2 * D_ATT // LANES))
    k_s = with_past(cache_sb_k[0], c16[s_rows, o_q + D_ATT:o_q + 2 * D_ATT], BF16)
    v_s = with_past(cache_sb_v[0], c16[s_rows, o_q + 2 * D_ATT:n_main_cd], BF16)
    sb_s = _attention("sb", c16, k_s, v_s, None, None, n_seq=bs, n_qt=1, tq=ls, tk=SEQ_TILE,
                      q_off=past, lk=lk_s, q_row_block0=bp * lp // ls, q_col_block0=qcol_cd,
                      kv_flat_col_block0=None)
    sb = jnp.concatenate([sb_p, sb_s], axis=0)

    h = _mixout(h, g_mix_post, 1, w16(cd_w_out[0]), sb, tm, ya=ssd_out)
    h = _ffn_half(h, *ffn2, 1, tm, tf)

    ssd_p = ssd_st_p.reshape(1, bp, H_SSM, P_SSM, N_SSM)
    ssd_s = ssd_st_s.reshape(1, bs, H_SSM, P_SSM, N_SSM)
    sb_k_p, sb_v_p = heads(prompt_rows(kc)), heads(prompt_rows(vc))
    sb_k_s, sb_v_s = heads(sample_rows(kc)), heads(sample_rows(vc))

    y_prompt = h[:bp * lp].reshape(bp, lp, d)[:, N_META:l0]
    y_sample = h[bp * lp:].reshape(bs, ls, d)
    return (y_prompt, y_sample,
            s5_re_p, s5_im_p, fox_k_p, fox_v_p, fox_logf_p, ssd_p, conv_p[None], sb_k_p, sb_v_p,
            s5_re_s, s5_im_s, fox_k_s, fox_v_s, fox_logf_s, ssd_s, conv_s[None], sb_k_s, sb_v_s)
```

```python
import functools

import jax
import jax.numpy as jnp
from jax import lax
from jax.experimental import pallas as pl
from jax.experimental.pallas import tpu as pltpu

F32 = jnp.float32
BF16 = jnp.bfloat16

EPS = 1e-6
D_MODEL = 1024
DH = 64
N_META = 16
D_S5 = 512
S5_GROUP = 16
G_S5 = 32
N_S5 = 64
S5_W = G_S5 * N_S5
H_ATT = 8
D_ATT = H_ATT * DH
D_SSM = 512
P_SSM = 64
H_SSM = 8
G_SSM = 2
N_SSM = 128
CONV_W = 4
CONV_DIM = D_SSM + 2 * G_SSM * N_SSM

LANES = 128
SUBLANES = 8
SEQ_TILE = 256
PROJ_TN = 256
VMEM_LIMIT = 56 * 1024 * 1024

NT_DIMS = (((1,), (1,)), ((), ()))
EXP_ZERO = 110.0
M_INIT = -1e30


def _cparams(sem):
    return pltpu.CompilerParams(dimension_semantics=sem, vmem_limit_bytes=VMEM_LIMIT)


def _rms(x, g):
    return x * lax.rsqrt(jnp.mean(x * x, axis=-1, keepdims=True) + EPS) * g


def _sigmoid(x):
    return 1.0 / (1.0 + jnp.exp(-x))


def _softplus(x):
    return jnp.maximum(x, 0.0) + jnp.log1p(jnp.exp(-jnp.abs(x)))


def _ones_where(mask):
    return jnp.where(mask, 1.0, 0.0).astype(BF16)


def _split3(x):
    hi = x.astype(BF16)
    r1 = x - hi.astype(F32)
    mid = r1.astype(BF16)
    lo = (r1 - mid.astype(F32)).astype(BF16)
    return hi, mid, lo


def _split3_dot(tri, x):
    return sum(jnp.dot(tri, part, preferred_element_type=F32) for part in _split3(x))


def _split3_transpose(eye, x):
    return sum(lax.dot_general(eye, part, NT_DIMS, preferred_element_type=F32) for part in _split3(x))


def _ffn_kernel(h_ref, gpre_ref, gpost_ref, wg_ref, wu_ref, wd_ref, o_ref, xn_ref, acc_ref, *, nj):
    j = pl.program_id(1)

    @pl.when(j == 0)
    def _():
        xn_ref[...] = _rms(h_ref[...], gpre_ref[...]).astype(BF16)
        acc_ref[...] = jnp.zeros_like(acc_ref)

    xn = xn_ref[...]
    g = jnp.dot(xn, wg_ref[...].astype(BF16), preferred_element_type=F32)
    u = jnp.dot(xn, wu_ref[...].astype(BF16), preferred_element_type=F32)
    a = (g * _sigmoid(g)) * u
    acc_ref[...] += jnp.dot(a.astype(BF16), wd_ref[...].astype(BF16), preferred_element_type=F32)

    @pl.when(j == nj - 1)
    def _():
        o_ref[...] = h_ref[...] + 0.5 * _rms(acc_ref[...], gpost_ref[...])


def _ffn_half(h, gpre, gpost, wg, wu, wd, layer, tm, tf):
    t, d = h.shape
    ff = wg.shape[-1]
    nj = ff // tf
    return pl.pallas_call(
        functools.partial(_ffn_kernel, nj=nj),
        name="ffn_half",
        grid=(t // tm, nj),
        in_specs=[
            pl.BlockSpec((tm, d), lambda i, j: (i, 0)),
            pl.BlockSpec((None, 1, d), lambda i, j: (layer, 0, 0)),
            pl.BlockSpec((None, 1, d), lambda i, j: (layer, 0, 0)),
            pl.BlockSpec((None, d, tf), lambda i, j: (layer, 0, j)),
            pl.BlockSpec((None, d, tf), lambda i, j: (layer, 0, j)),
            pl.BlockSpec((None, tf, d), lambda i, j: (layer, j, 0)),
        ],
        out_specs=pl.BlockSpec((tm, d), lambda i, j: (i, 0)),
        out_shape=jax.ShapeDtypeStruct((t, d), F32),
        scratch_shapes=[pltpu.VMEM((tm, d), BF16), pltpu.VMEM((tm, d), F32)],
        compiler_params=_cparams(("parallel", "arbitrary")),
    )(h, gpre, gpost, wg, wu, wd)


def _inproj_kernel(h_ref, g_ref, w_ref, b_ref, o32_ref, o16_ref, xn_ref, *, nj, tn, tail):
    j = pl.program_id(1)

    @pl.when(j == 0)
    def _():
        xn_ref[...] = _rms(h_ref[...], g_ref[...]).astype(BF16)

    p = jnp.dot(xn_ref[...], w_ref[...], preferred_element_type=F32)
    o32_ref[...] = p
    o16_ref[...] = p.astype(BF16)

    @pl.when(j == nj - 1)
    def _():
        x = p[:, tn - PROJ_TN:] + b_ref[...]
        r = -_softplus(-x) if tail == "log_sigmoid" else _softplus(x)
        o32_ref[:, tn - PROJ_TN:] = r
        o16_ref[:, tn - PROJ_TN:] = r.astype(BF16)


def _inproj(h, g, w, gate_bias, layer, tm, tail):
    t, d = h.shape
    n = w.shape[-1]
    nj = 2
    tn = n // nj
    assert tn % LANES == 0 and tn >= PROJ_TN
    return pl.pallas_call(
        functools.partial(_inproj_kernel, nj=nj, tn=tn, tail=tail),
        name="mix_inproj",
        grid=(t // tm, nj),
        in_specs=[
            pl.BlockSpec((tm, d), lambda i, j: (i, 0)),
            pl.BlockSpec((None, 1, d), lambda i, j: (layer, 0, 0)),
            pl.BlockSpec((d, tn), lambda i, j: (0, j)),
            pl.BlockSpec((1, PROJ_TN), lambda i, j: (0, 0)),
        ],
        out_specs=[pl.BlockSpec((tm, tn), lambda i, j: (i, j)),
                   pl.BlockSpec((tm, tn), lambda i, j: (i, j))],
        out_shape=[jax.ShapeDtypeStruct((t, n), F32), jax.ShapeDtypeStruct((t, n), BF16)],
        scratch_shapes=[pltpu.VMEM((tm, d), BF16)],
        compiler_params=_cparams(("parallel", "arbitrary")),
    )(h, g, w, gate_bias)


def _gelu_tanh(x):
    return 0.5 * x * (1.0 + jnp.tanh(0.7978845608028654 * (x + 0.044715 * (x * x * x))))


def _mixout_kernel(*refs, s5_glu):
    if s5_glu:
        (ya_ref, u_ref, dsk_ref, wglu_ref, bglu_ref, yb_ref, wo_ref, h_ref, gpost_ref, o_ref) = refs
        y = ya_ref[...] + dsk_ref[...] * u_ref[...]
        g = _gelu_tanh(y)
        gate = _sigmoid(jnp.dot(g.astype(BF16), wglu_ref[...], preferred_element_type=F32) + bglu_ref[...])
        a = (g * gate).astype(BF16)
    else:
        (ya_ref, yb_ref, wo_ref, h_ref, gpost_ref, o_ref) = refs
        a = ya_ref[...]
    half = a.shape[-1]
    out = (jnp.dot(a, wo_ref[:half, :], preferred_element_type=F32)
           + jnp.dot(yb_ref[...], wo_ref[half:, :], preferred_element_type=F32))
    o_ref[...] = h_ref[...] + _rms(out, gpost_ref[...])


def _mixout(h, gpost, layer, wo, yb, tm, *, ya, u=None, dsk=None, wglu=None, bglu=None):
    t, d = h.shape
    half = yb.shape[-1]
    s5_glu = u is not None
    row = lambda i: (i, 0)
    const = lambda i: (0, 0)
    args, specs = [ya], [pl.BlockSpec((tm, half), row)]
    if s5_glu:
        args += [u, dsk, wglu, bglu]
        specs += [pl.BlockSpec((tm, half), row), pl.BlockSpec((1, half), const),
                  pl.BlockSpec((half, half), const), pl.BlockSpec((1, half), const)]
    args += [yb, wo, h, gpost]
    specs += [pl.BlockSpec((tm, half), row), pl.BlockSpec((2 * half, d), const),
              pl.BlockSpec((tm, d), row), pl.BlockSpec((None, 1, d), lambda i: (layer, 0, 0))]
    return pl.pallas_call(
        functools.partial(_mixout_kernel, s5_glu=s5_glu),
        name="mix_out",
        grid=(t // tm,),
        in_specs=specs,
        out_specs=pl.BlockSpec((tm, d), row),
        out_shape=jax.ShapeDtypeStruct((t, d), F32),
        compiler_params=_cparams(("parallel",)),
    )(*args)


def _s5_prep_kernel(are_ref, aim_ref, ldt_ref, bre_ref, bim_ref, cre_ref, cim_ref,
                    tab_ref, bf_ref, cf_ref):
    a_re = are_ref[...]
    a_im = aim_ref[...]
    dt = jnp.exp(ldt_ref[...])
    mag = jnp.exp(dt * a_re)
    ab_re = mag * jnp.cos(dt * a_im)
    ab_im = mag * jnp.sin(dt * a_im)
    den = a_re * a_re + a_im * a_im
    nr = ab_re - 1.0
    coef_re = (nr * a_re + ab_im * a_im) / den
    coef_im = (ab_im * a_re - nr * a_im) / den

    pw_re, pw_im = [ab_re], [ab_im]
    for _ in range(SUBLANES - 1):
        pr, pi = pw_re[-1], pw_im[-1]
        pw_re.append(pr * ab_re - pi * ab_im)
        pw_im.append(pr * ab_im + pi * ab_re)
    sub = lax.broadcasted_iota(jnp.int32, (SUBLANES, S5_W), 0)
    for k in range(3):
        sh = 1 << k
        tab_ref[k] = jnp.where(sub >= sh, pw_re[sh - 1], 0.0)
        tab_ref[3 + k] = jnp.where(sub >= sh, pw_im[sh - 1], 0.0)
    q_re = jnp.zeros((SUBLANES, S5_W), F32)
    q_im = jnp.zeros((SUBLANES, S5_W), F32)
    for r in range(SUBLANES):
        q_re = jnp.where(sub == r, pw_re[r], q_re)
        q_im = jnp.where(sub == r, pw_im[r], q_im)
    tab_ref[6] = q_re
    tab_ref[7] = q_im

    rg = lax.broadcasted_iota(jnp.int32, (D_S5, S5_W), 0) // S5_GROUP
    cg = lax.broadcasted_iota(jnp.int32, (D_S5, S5_W), 1) // N_S5
    b_re = bre_ref[...]
    b_im = bim_ref[...]
    bb_re = coef_re * b_re - coef_im * b_im
    bb_im = coef_re * b_im + coef_im * b_re
    bf_ref[:, :S5_W] = jnp.where(rg == cg, bb_re, 0.0).astype(BF16)
    bf_ref[:, S5_W:] = jnp.where(rg == cg, bb_im, 0.0).astype(BF16)
    rg2 = lax.broadcasted_iota(jnp.int32, (S5_W, D_S5), 0) // N_S5
    cg2 = lax.broadcasted_iota(jnp.int32, (S5_W, D_S5), 1) // S5_GROUP
    cf_ref[:S5_W, :] = jnp.where(rg2 == cg2, cre_ref[...], 0.0).astype(BF16)
    cf_ref[S5_W:, :] = jnp.where(rg2 == cg2, -cim_ref[...], 0.0).astype(BF16)


def _s5_prep(a_re, a_im, log_dt, b_re, b_im, c_re, c_im):
    flat = lambda x: x.reshape(1, S5_W)
    ldt = jnp.broadcast_to(log_dt[:, None], (G_S5, N_S5))
    b_t = lambda b: jnp.tile(jnp.transpose(b, (2, 0, 1)).reshape(S5_GROUP, S5_W), (G_S5, 1))
    c_t = lambda c: jnp.tile(jnp.transpose(c, (0, 2, 1)).reshape(S5_W, S5_GROUP), (1, G_S5))
    return pl.pallas_call(
        _s5_prep_kernel,
        name="s5_prep",
        out_shape=[jax.ShapeDtypeStruct((8, SUBLANES, S5_W), F32),
                   jax.ShapeDtypeStruct((D_S5, 2 * S5_W), BF16),
                   jax.ShapeDtypeStruct((2 * S5_W, D_S5), BF16)],
        compiler_params=pltpu.CompilerParams(vmem_limit_bytes=VMEM_LIMIT),
    )(flat(a_re), flat(a_im), flat(ldt), b_t(b_re), b_t(b_im), c_t(c_re), c_t(c_im))


S5_CB = 256


def _s5_kernel(*refs, rows, seg, chain, last_tile, last_row, aliased):
    u_ref, bf_ref, cf_ref, tab_ref, h0_ref = refs[:5]
    y_ref, st_ref, bu_ref, car_ref = refs[6:] if aliased else refs[5:]
    i = pl.program_id(1)
    w = S5_W
    bu_ref[...] = jnp.dot(u_ref[...], bf_ref[...], preferred_element_type=F32)
    if chain:
        @pl.when(i == 0)
        def _():
            car_ref[...] = h0_ref[...]

    for c in range(0, w, S5_CB):
        re_cols = slice(c, c + S5_CB)
        im_cols = slice(w + c, w + c + S5_CB)
        pr = [tab_ref[k, :, re_cols] for k in range(3)]
        pi = [tab_ref[3 + k, :, re_cols] for k in range(3)]
        qr = tab_ref[6, :, re_cols]
        qi = tab_ref[7, :, re_cols]
        for sg in range(rows // seg):
            src = car_ref if chain else h0_ref
            srow = 0 if chain else sg
            cr0 = src[srow:srow + 1, re_cols]
            ci0 = src[srow:srow + 1, im_cols]

            def body(a, carry, sg=sg, re_cols=re_cols, im_cols=im_cols, pr=pr, pi=pi, qr=qr, qi=qi):
                cr, ci = carry
                r0 = pl.multiple_of(sg * seg + a * SUBLANES, SUBLANES)
                xr = bu_ref[pl.ds(r0, SUBLANES), re_cols]
                xi = bu_ref[pl.ds(r0, SUBLANES), im_cols]
                for k in range(3):
                    sr = pltpu.roll(xr, 1 << k, 0)
                    si = pltpu.roll(xi, 1 << k, 0)
                    xr, xi = xr + pr[k] * sr - pi[k] * si, xi + pr[k] * si + pi[k] * sr
                xr, xi = xr + qr * cr - qi * ci, xi + qr * ci + qi * cr
                bu_ref[pl.ds(r0, SUBLANES), re_cols] = xr
                bu_ref[pl.ds(r0, SUBLANES), im_cols] = xi
                return xr[SUBLANES - 1:SUBLANES, :], xi[SUBLANES - 1:SUBLANES, :]

            cr, ci = lax.fori_loop(0, seg // SUBLANES, body, (cr0, ci0), unroll=2)
            if chain:
                car_ref[0:1, re_cols] = cr
                car_ref[0:1, im_cols] = ci
            else:
                st_ref[sg:sg + 1, re_cols] = cr
                st_ref[sg:sg + 1, im_cols] = ci

    y_ref[...] = jnp.dot(bu_ref[...].astype(BF16), cf_ref[...], preferred_element_type=F32)
    if chain:
        @pl.when(i == last_tile)
        def _():
            st_ref[...] = bu_ref[last_row:last_row + 1, :]


def _s5_scan(proj16, bfull, cfull, tab, h0, dst=None, *, row_block0, n_groups, n_tiles, seg, chain, l_valid):
    rows = SEQ_TILE
    s = h0.shape[1]
    last = l_valid - 1
    kern = functools.partial(_s5_kernel, rows=rows, seg=seg, chain=chain,
                             last_tile=last // rows, last_row=last % rows, aliased=dst is not None)
    if chain:
        umap = lambda b, i: (row_block0 + b * n_tiles + i, 0)
        hmap = lambda b, i: (b, 0, 0)
    else:
        umap = lambda b, i: (row_block0 + i, 0)
        hmap = lambda b, i: (i, 0, 0)
    const2 = lambda b, i: (0, 0)
    in_specs = [
        pl.BlockSpec((rows, D_S5), umap),
        pl.BlockSpec((D_S5, 2 * S5_W), const2),
        pl.BlockSpec((2 * S5_W, D_S5), const2),
        pl.BlockSpec((8, SUBLANES, S5_W), lambda b, i: (0, 0, 0)),
        pl.BlockSpec((None, s, 2 * S5_W), hmap),
    ]
    args = [proj16, bfull, cfull, tab, h0]
    aliases = {}
    if dst is not None:
        in_specs.append(pl.BlockSpec(memory_space=pl.ANY))
        args.append(dst)
        aliases = {len(args) - 1: 0}
    return pl.pallas_call(
        kern,
        name="s5_scan",
        grid=(n_groups, n_tiles),
        in_specs=in_specs,
        out_specs=[pl.BlockSpec((rows, D_S5), umap),
                   pl.BlockSpec((None, s, 2 * S5_W), hmap)],
        out_shape=[jax.ShapeDtypeStruct((proj16.shape[0], D_S5), F32),
                   jax.ShapeDtypeStruct(h0.shape, F32)],
        input_output_aliases=aliases,
        scratch_shapes=[pltpu.VMEM((rows, 2 * S5_W), F32), pltpu.VMEM((1, 2 * S5_W), F32)],
        compiler_params=_cparams(("arbitrary", "arbitrary")),
    )(*args)


def _cumsum_kernel(x_ref, o_ref, pmin_ref, car_ref, min_ref, *, tk):
    j = pl.program_id(0)

    @pl.when(j == 0)
    def _():
        car_ref[...] = jnp.zeros_like(car_ref)
        min_ref[...] = jnp.full_like(min_ref, jnp.inf)

    r = lax.broadcasted_iota(jnp.int32, (tk, tk), 0)
    c = lax.broadcasted_iota(jnp.int32, (tk, tk), 1)
    tri = _ones_where(r <= c)
    acc = sum(jnp.dot(part, tri, preferred_element_type=F32) for part in _split3(x_ref[...])) + car_ref[...]
    o_ref[...] = acc
    car_ref[...] = acc[:, tk - 1:tk]
    run_min = jnp.minimum(min_ref[...], jnp.min(acc, axis=-1, keepdims=True))
    min_ref[...] = run_min
    pmin_ref[...] = run_min


def _time_cumsum(x):
    b, hh, length = x.shape
    tk = SEQ_TILE
    rows = b * hh
    fcum, pmin = pl.pallas_call(
        functools.partial(_cumsum_kernel, tk=tk),
        name="time_cumsum",
        grid=(length // tk,),
        in_specs=[pl.BlockSpec((rows, tk), lambda j: (0, j))],
        out_specs=[pl.BlockSpec((rows, tk), lambda j: (0, j)),
                   pl.BlockSpec((None, rows, 1), lambda j: (j, 0, 0))],
        out_shape=[jax.ShapeDtypeStruct((rows, length), F32),
                   jax.ShapeDtypeStruct((length // tk, rows, 1), F32)],
        scratch_shapes=[pltpu.VMEM((rows, 1), F32), pltpu.VMEM((rows, 1), F32)],
        compiler_params=_cparams(("arbitrary",)),
    )(x.reshape(rows, length))
    return fcum.reshape(b, hh, length), pmin.reshape(length // tk, b, hh, 1)


def _head_pair_queries(q_ref):
    lane = lax.broadcasted_iota(jnp.int32, (1, LANES), 1)
    q = q_ref[...] * jnp.asarray(DH ** -0.5, BF16)
    zero = jnp.zeros_like(q)
    return lane, (jnp.where(lane < DH, q, zero), jnp.where(lane >= DH, q, zero))


def _fox_sample_kernel(q_ref, kn_ref, vn_ref, kp_ref, vp_ref, fq_ref, fk_ref, dst_ref, o_ref, *, ls, past):
    del dst_ref
    lane, qs = _head_pair_queries(q_ref)
    kp = kp_ref[...].astype(BF16)
    vp = vp_ref[...].astype(BF16)
    kn = kn_ref[...]
    vn = vn_ref[...]
    rr = lax.broadcasted_iota(jnp.int32, (ls, ls), 0)
    cc = lax.broadcasted_iota(jnp.int32, (ls, ls), 1)
    outs = []
    for h in range(2):
        fq = fq_ref[:, h:h + 1]
        s_old = lax.dot_general(qs[h], kp, NT_DIMS, preferred_element_type=F32) + fq - fk_ref[h:h + 1, :past]
        s_new = lax.dot_general(qs[h], kn, NT_DIMS, preferred_element_type=F32) + fq \
            - fk_ref[h:h + 1, past:past + ls]
        s_new = jnp.where(cc <= rr, s_new, -jnp.inf)
        m = jnp.maximum(jnp.max(s_old, axis=-1, keepdims=True), jnp.max(s_new, axis=-1, keepdims=True))
        p_old = jnp.exp(s_old - m)
        p_new = jnp.exp(s_new - m)
        l = jnp.sum(p_old, axis=-1, keepdims=True) + jnp.sum(p_new, axis=-1, keepdims=True)
        o = (jnp.dot(p_old.astype(BF16), vp, preferred_element_type=F32)
             + jnp.dot(p_new.astype(BF16), vn, preferred_element_type=F32))
        outs.append(o / l)
    o_ref[...] = jnp.where(lane < DH, outs[0], outs[1]).astype(BF16)


def _fox_t_kernel(q_ref, k_ref, v_ref, fqr_ref, fkc_ref, fmin_ref, o_ref,
                  m_ref, l_ref, acc_ref, vt_ref, fkb_ref, kabs_ref, s_ref, p_ref, *, tq, tk, n_kt):
    assert tq == tk
    i = pl.program_id(2)
    lane, qs = _head_pair_queries(q_ref)
    qpos0 = i * tq
    rep = tq // LANES

    @pl.when(i == 0)
    def _():
        kabs_ref[...] = jnp.broadcast_to(jnp.max(jnp.abs(k_ref[...]), axis=0, keepdims=True), kabs_ref.shape)
        er = lax.broadcasted_iota(jnp.int32, (LANES, LANES), 0)
        ec = lax.broadcasted_iota(jnp.int32, (LANES, LANES), 1)
        eye = _ones_where(er == ec)
        for h in range(2):
            fkb_ref[h] = jnp.broadcast_to(fkc_ref[:, h:h + 1], fkb_ref.shape[1:])

        def transpose_tile(j, c):
            vj = v_ref[pl.ds(pl.multiple_of(j * tk, tk), tk), :]
            vt_ref[j] = lax.dot_general(eye, vj, NT_DIMS, preferred_element_type=F32).astype(BF16)
            return c

        lax.fori_loop(0, n_kt, transpose_tile, 0)

    m_ref[...] = jnp.full_like(m_ref, M_INIT)
    l_ref[...] = jnp.zeros_like(l_ref)
    acc_ref[...] = jnp.zeros_like(acc_ref)
    kabs = kabs_ref[...]
    slack = [lax.dot_general(kabs, jnp.abs(qs[h]), NT_DIMS, preferred_element_type=F32)[0:1, :]
             + fqr_ref[h:h + 1, :] for h in range(2)]

    def worth_visiting(j):
        jc = jnp.maximum(j, 0)
        gap = jnp.maximum(jnp.max(slack[0] - m_ref[0] - fmin_ref[jc, 0:1, :]),
                          jnp.max(slack[1] - m_ref[1] - fmin_ref[jc, 1:2, :]))
        return gap > -EXP_ZERO

    def scores(j):
        kj = k_ref[pl.ds(pl.multiple_of(j * tk, tk), tk), :]
        return [lax.dot_general(kj, qs[h], NT_DIMS, preferred_element_type=F32) for h in range(2)]

    def weights(j, s, h, vis):
        fk = fkb_ref[h, pl.ds(pl.multiple_of(j * tk, tk), tk), :]
        s = s + fqr_ref[h:h + 1, :] - jnp.concatenate([fk] * rep, axis=1)
        if vis is not None:
            s = jnp.where(vis, s, -jnp.inf)
        m_prev = m_ref[h]
        m_new = jnp.maximum(m_prev, jnp.max(s, axis=0, keepdims=True))
        alpha = jnp.exp(m_prev - m_new)
        p = jnp.exp(s - m_new)
        l_ref[h] = alpha * l_ref[h] + jnp.sum(p, axis=0, keepdims=True)
        m_ref[h] = m_new
        p_ref[h] = p.astype(BF16)
        return alpha

    def weighted_values(j, h):
        return jnp.dot(vt_ref[j, h * DH:(h + 1) * DH, :], p_ref[h], preferred_element_type=F32)

    def stage_scores(slot, j):
        nxt = scores(jnp.maximum(j, 0))
        s_ref[slot, 0] = nxt[0]
        s_ref[slot, 1] = nxt[1]

    first = scores(i)
    stage_scores(1, i - 1)
    kpos = i * tk + lax.broadcasted_iota(jnp.int32, (tk, tq), 0)
    qpos = qpos0 + lax.broadcasted_iota(jnp.int32, (tk, tq), 1)
    for h in range(2):
        weights(i, first[h], h, kpos <= qpos)

    def more(c):
        return c[1]

    def trip(c):
        t, _ = c
        j = i - t
        slot = t % 2
        for h in range(2):
            rows = slice(h * DH, (h + 1) * DH)
            pv = weighted_values(j + 1, h)
            alpha = weights(j, s_ref[slot, h], h, None)
            acc_ref[rows, :] = alpha * (acc_ref[rows, :] + pv)
        stage_scores(1 - slot, j - 1)
        return t + 1, jnp.logical_and(j >= 1, worth_visiting(j - 1))

    visited, _ = lax.while_loop(more, trip, (1, jnp.logical_and(i >= 1, worth_visiting(i - 1))))
    j_fin = i - (visited - 1)
    for h in range(2):
        rows = slice(h * DH, (h + 1) * DH)
        acc_ref[rows, :] = acc_ref[rows, :] + weighted_values(j_fin, h)
    o_t = jnp.concatenate([acc_ref[0:DH, :] / l_ref[0], acc_ref[DH:, :] / l_ref[1]], axis=0).astype(BF16)
    qr = lax.broadcasted_iota(jnp.int32, (tq, tq), 0)
    qc = lax.broadcasted_iota(jnp.int32, (tq, tq), 1)
    o_ref[...] = lax.dot_general(_ones_where(qr == qc), o_t, NT_DIMS,
                                 preferred_element_type=F32).astype(BF16)


def _fox_prompt(p16, fqr, fkc, fmin, *, n_seq, n_qt, lk, q_col_block0, k_col_block0, v_col_block0):
    tq = tk = SEQ_TILE
    n_hg = H_ATT // 2
    n_kt = lk // tk
    return pl.pallas_call(
        functools.partial(_fox_t_kernel, tq=tq, tk=tk, n_kt=n_kt),
        name="fox_attention_t",
        grid=(n_seq, n_hg, n_qt),
        in_specs=[
            pl.BlockSpec((tq, LANES), lambda b, g, i: (b * n_qt + i, q_col_block0 + g)),
            pl.BlockSpec((lk, LANES), lambda b, g, i: (b, k_col_block0 + g)),
            pl.BlockSpec((lk, LANES), lambda b, g, i: (b, v_col_block0 + g)),
            pl.BlockSpec((None, None, 2, tq), lambda b, g, i: (b, g, 0, i)),
            pl.BlockSpec((None, None, lk, 2), lambda b, g, i: (b, g, 0, 0)),
            pl.BlockSpec((None, None, n_kt, 2, 1), lambda b, g, i: (b, g, 0, 0, 0)),
        ],
        out_specs=pl.BlockSpec((tq, LANES), lambda b, g, i: (b * n_qt + i, g)),
        out_shape=jax.ShapeDtypeStruct((p16.shape[0], D_ATT), BF16),
        scratch_shapes=[pltpu.VMEM((2, 1, tq), F32), pltpu.VMEM((2, 1, tq), F32), pltpu.VMEM((LANES, tq), F32),
                        pltpu.VMEM((n_kt, LANES, tk), BF16), pltpu.VMEM((2, lk, LANES), F32),
                        pltpu.VMEM((2 * SUBLANES, LANES), BF16),
                        pltpu.VMEM((2, 2, tk, tq), F32), pltpu.VMEM((2, tk, tq), BF16)],
        compiler_params=_cparams(("parallel", "parallel", "arbitrary")),
    )(p16, p16, p16, fqr, fkc, fmin)


def _sb_weights(z, tri, vis, run):
    sp = jnp.maximum(z, 0.0) + jnp.log(1.0 + jnp.exp(-jnp.abs(z)))
    log_keep = -sp
    if vis is not None:
        log_keep = jnp.where(vis, log_keep, 0.0)
    hi = log_keep.astype(BF16)
    lo = (log_keep - hi.astype(F32)).astype(BF16)
    within = jnp.dot(hi, tri, preferred_element_type=F32) + jnp.dot(lo, tri, preferred_element_type=F32)
    wgt = jnp.exp((z - sp) + (within + run))
    if vis is not None:
        wgt = jnp.where(vis, wgt, 0.0)
    return wgt.astype(BF16), run + within[:, 0:1] + log_keep[:, 0:1]


def _strict_upper_ones(n):
    rr = lax.broadcasted_iota(jnp.int32, (n, n), 0)
    cc = lax.broadcasted_iota(jnp.int32, (n, n), 1)
    return _ones_where(rr > cc), rr, cc


def _sb_kernel(q_ref, k_ref, v_ref, o_ref, r_ref, acc_ref, z_ref, w_ref, *, tq, tk):
    assert tq == tk
    i = pl.program_id(2)
    lane, qs = _head_pair_queries(q_ref)
    r_ref[...] = jnp.zeros_like(r_ref)
    acc_ref[...] = jnp.zeros_like(acc_ref)
    tri, rr, cc = _strict_upper_ones(tk)

    def logits(j):
        kj = k_ref[pl.ds(pl.multiple_of(j * tk, tk), tk), :]
        return [lax.dot_general(qs[h], kj, NT_DIMS, preferred_element_type=F32) for h in range(2)]

    def stage_logits(slot, j):
        nxt = logits(jnp.maximum(j, 0))
        z_ref[slot, 0] = nxt[0]
        z_ref[slot, 1] = nxt[1]

    def weights(z, h, vis):
        w_ref[h], r_ref[h] = _sb_weights(z, tri, vis, r_ref[h])

    def weighted_values(j, h):
        vj = v_ref[pl.ds(pl.multiple_of(j * tk, tk), tk), :]
        return jnp.dot(w_ref[h], vj, preferred_element_type=F32)

    def worth_visiting():
        return jnp.max(jnp.maximum(r_ref[0], r_ref[1])) > -EXP_ZERO

    first = logits(i)
    stage_logits(1, i - 1)
    for h in range(2):
        weights(first[h], h, cc < rr)

    def more(c):
        return c[1]

    def trip(c):
        t, _ = c
        j = i - t
        slot = t % 2
        for h in range(2):
            acc_ref[h] += weighted_values(j + 1, h)
            weights(z_ref[slot, h], h, None)
        stage_logits(1 - slot, j - 1)
        return t + 1, jnp.logical_and(j >= 1, worth_visiting())

    visited, _ = lax.while_loop(more, trip, (1, jnp.logical_and(i >= 1, worth_visiting())))
    j_fin = i - (visited - 1)
    o_ref[...] = jnp.where(lane < DH, acc_ref[0] + weighted_values(j_fin, 0),
                           acc_ref[1] + weighted_values(j_fin, 1)).astype(BF16)


def _sb_prompt(c16, *, n_seq, n_qt, lk, q_col_block0, k_col_block0, v_col_block0):
    tq = tk = SEQ_TILE
    n_hg = H_ATT // 2
    return pl.pallas_call(
        functools.partial(_sb_kernel, tq=tq, tk=tk),
        name="sb_attention",
        grid=(n_seq, n_hg, n_qt),
        in_specs=[
            pl.BlockSpec((tq, LANES), lambda b, g, i: (b * n_qt + i, q_col_block0 + g)),
            pl.BlockSpec((lk, LANES), lambda b, g, i: (b, k_col_block0 + g)),
            pl.BlockSpec((lk, LANES), lambda b, g, i: (b, v_col_block0 + g)),
        ],
        out_specs=pl.BlockSpec((tq, LANES), lambda b, g, i: (b * n_qt + i, g)),
        out_shape=jax.ShapeDtypeStruct((c16.shape[0], D_ATT), BF16),
        scratch_shapes=[pltpu.VMEM((2, tq, 1), F32), pltpu.VMEM((2, tq, LANES), F32),
                        pltpu.VMEM((2, 2, tq, tk), F32), pltpu.VMEM((2, tq, tk), BF16)],
        compiler_params=_cparams(("parallel", "parallel", "arbitrary")),
    )(c16, c16, c16)


def _sb_sample_kernel(q_ref, kn_ref, vn_ref, kp_ref, vp_ref, dst_ref, o_ref, r_ref, acc_ref, *, ls, past, tk):
    del dst_ref
    lane, qs = _head_pair_queries(q_ref)
    tri_new, rr, cc = _strict_upper_ones(ls)
    tri, _, _ = _strict_upper_ones(tk)
    kn = kn_ref[...]
    vn = vn_ref[...]
    for h in range(2):
        z = lax.dot_general(qs[h], kn, NT_DIMS, preferred_element_type=F32)
        wgt, r_ref[h] = _sb_weights(z, tri_new, cc < rr, jnp.zeros((ls, 1), F32))
        acc_ref[h] = jnp.dot(wgt, vn, preferred_element_type=F32)

    def worth_visiting():
        return jnp.max(jnp.maximum(r_ref[0], r_ref[1])) > -EXP_ZERO

    n_tiles = past // tk

    def more(c):
        t, go = c
        return jnp.logical_and(t < n_tiles, go)

    def trip(c):
        t, _ = c
        rows = pl.ds(pl.multiple_of((n_tiles - 1 - t) * tk, tk), tk)
        kj = kp_ref[rows, :].astype(BF16)
        vj = vp_ref[rows, :].astype(BF16)
        for h in range(2):
            z = lax.dot_general(qs[h], kj, NT_DIMS, preferred_element_type=F32)
            wgt, r_ref[h] = _sb_weights(z, tri, None, r_ref[h])
            acc_ref[h] += jnp.dot(wgt, vj, preferred_element_type=F32)
        return t + 1, worth_visiting()

    lax.while_loop(more, trip, (0, worth_visiting()))
    o_ref[...] = jnp.where(lane < DH, acc_ref[0], acc_ref[1]).astype(BF16)


def _sample_attention(kind, proj16, past_k, past_v, dst, fq=None, fk=None, *, n_seq, ls, row_block0,
                      q_col_block0, k_col_block0, v_col_block0):
    n_hg = H_ATT // 2
    past = past_k.shape[1]
    row = lambda col0: (lambda b, g: (row_block0 + b, col0 + g))
    new_spec = lambda col0: pl.BlockSpec((ls, LANES), row(col0))
    past_spec = pl.BlockSpec((None, past, LANES), lambda b, g: (b, 0, g))
    in_specs = [new_spec(q_col_block0), new_spec(k_col_block0), new_spec(v_col_block0), past_spec, past_spec]
    args = [proj16, proj16, proj16, past_k, past_v]
    if kind == "fox":
        in_specs += [pl.BlockSpec((None, None, ls, 2), lambda b, g: (b, g, 0, 0)),
                     pl.BlockSpec((None, None, 2, fk.shape[-1]), lambda b, g: (b, g, 0, 0))]
        args += [fq, fk]
        kern = functools.partial(_fox_sample_kernel, ls=ls, past=past)
        scratch = []
    else:
        assert past % SEQ_TILE == 0
        kern = functools.partial(_sb_sample_kernel, ls=ls, past=past, tk=SEQ_TILE)
        scratch = [pltpu.VMEM((2, ls, 1), F32), pltpu.VMEM((2, ls, LANES), F32)]
    in_specs.append(pl.BlockSpec(memory_space=pl.ANY))
    args.append(dst)
    return pl.pallas_call(
        kern,
        name=kind + "_attention_sample",
        grid=(n_seq, n_hg),
        in_specs=in_specs,
        out_specs=pl.BlockSpec((ls, LANES), lambda b, g: (row_block0 + b, g)),
        out_shape=jax.ShapeDtypeStruct(dst.shape, dst.dtype),
        input_output_aliases={len(args) - 1: 0},
        scratch_shapes=scratch,
        compiler_params=_cparams(("parallel", "parallel")),
    )(*args)


def _ssd_kernel(*refs, q, l_valid, n_chunks, aliased):
    z_ref, xbc_ref, dt_ref, cst_ref, s0_ref, cw_ref, cb_ref, alog_ref, dsk_ref, ng_ref = refs[:10]
    y_ref, sout_ref, cout_ref, xw_ref, s_ref, ysc_ref = refs[11:] if aliased else refs[10:]
    c = pl.program_id(1)
    hist = CONV_W - 1
    base = SUBLANES

    @pl.when(c == 0)
    def _():
        xw_ref[base - hist:base, :] = cst_ref[...]
        s_ref[...] = s0_ref[...]

    xw_ref[base:base + q, :] = xbc_ref[...]
    conv = cb_ref[...]
    for w in range(CONV_W):
        conv = conv + xw_ref[base - hist + w:base - hist + w + q, :] * cw_ref[w:w + 1, :]
    last = l_valid - 1

    @pl.when(c == last // q)
    def _():
        lr = base + last % q
        cout_ref[...] = xw_ref[lr - hist + 1:lr + 1, :]

    xw_ref[base - hist:base, :] = xw_ref[base + q - hist:base + q, :]

    act = conv * _sigmoid(conv)
    xs = act[:, :D_SSM]
    gw = G_SSM * N_SSM
    bm = act[:, D_SSM:D_SSM + gw].astype(BF16)
    cm = act[:, D_SSM + gw:].astype(BF16)

    rowg = c * q + lax.broadcasted_iota(jnp.int32, (q, LANES), 0)
    dt = jnp.where(rowg < l_valid, dt_ref[...], 0.0)
    adt = dt * (-jnp.exp(alog_ref[...]))
    rr = lax.broadcasted_iota(jnp.int32, (q, q), 0)
    cc = lax.broadcasted_iota(jnp.int32, (q, q), 1)
    causal = rr >= cc
    acs = _split3_dot(_ones_where(causal), adt)
    er = lax.broadcasted_iota(jnp.int32, (LANES, LANES), 0)
    ec = lax.broadcasted_iota(jnp.int32, (LANES, LANES), 1)
    eye = _ones_where(er == ec)
    acs_t = _split3_transpose(eye, acs)
    lane = lax.broadcasted_iota(jnp.int32, (1, LANES), 1)
    first = lane < P_SSM
    srow_first = lax.broadcasted_iota(jnp.int32, (LANES, 1), 0) < P_SSM
    heads_per_group = H_SSM // G_SSM

    cb_mats = []
    for g in range(G_SSM):
        cg = cm[:, g * N_SSM:(g + 1) * N_SSM]
        bg = bm[:, g * N_SSM:(g + 1) * N_SSM]
        cb_mats.append(lax.dot_general(cg, bg, NT_DIMS, preferred_element_type=F32))

    for pr in range(H_SSM // 2):
        h0, h1 = 2 * pr, 2 * pr + 1
        g = h0 // heads_per_group
        cg = cm[:, g * N_SSM:(g + 1) * N_SSM]
        bg = bm[:, g * N_SSM:(g + 1) * N_SSM]
        cols = slice(pr * LANES, (pr + 1) * LANES)
        xs_p = xs[:, cols]
        a0, a1 = acs[:, h0:h0 + 1], acs[:, h1:h1 + 1]
        xdt = xs_p * jnp.where(first, dt[:, h0:h0 + 1], dt[:, h1:h1 + 1])
        y = jnp.zeros((q, LANES), F32)
        for hh, a_col, keep in ((h0, a0, first), (h1, a1, jnp.logical_not(first))):
            seg = a_col - acs_t[hh:hh + 1, :]
            lmat = jnp.where(causal, jnp.exp(jnp.where(causal, seg, 0.0)), 0.0)
            m = (cb_mats[g] * lmat).astype(BF16)
            y = y + jnp.dot(m, jnp.where(keep, xdt, 0.0).astype(BF16), preferred_element_type=F32)
        st = s_ref[cols, :]
        y_off = lax.dot_general(cg, st.astype(BF16), NT_DIMS, preferred_element_type=F32)
        y = y + y_off * jnp.where(first, jnp.exp(a0), jnp.exp(a1))
        e0, e1 = a0[q - 1:q, :], a1[q - 1:q, :]
        wdec = jnp.where(first, jnp.exp(e0 - a0), jnp.exp(e1 - a1))
        xw = lax.dot_general(eye, (xdt * wdec).astype(BF16), NT_DIMS,
                             preferred_element_type=F32).astype(BF16)
        dec_rows = jnp.where(srow_first, jnp.exp(e0), jnp.exp(e1))
        s_ref[cols, :] = dec_rows * st + jnp.dot(xw, bg, preferred_element_type=F32)
        ysc_ref[:, cols] = y + dsk_ref[:, cols] * xs_p

    zt = z_ref[...]
    yg = ysc_ref[...] * (zt * _sigmoid(zt))
    gwid = D_SSM // G_SSM
    for g in range(G_SSM):
        blk = yg[:, g * gwid:(g + 1) * gwid]
        nrm = blk * lax.rsqrt(jnp.mean(blk * blk, axis=-1, keepdims=True) + EPS)
        y_ref[:, g * gwid:(g + 1) * gwid] = (nrm * ng_ref[:, g * gwid:(g + 1) * gwid]).astype(BF16)

    @pl.when(c == n_chunks - 1)
    def _():
        sout_ref[...] = s_ref[...]


def _ssd(proj32, cst, s0, cw, cb, alog, dsk, ng, dst=None, *, n_seq, n_chunks, q, l_valid, row_block0,
         z_col, xbc_col, dt_col):
    rmap = lambda col: (lambda b, c: (row_block0 + b * n_chunks + c, col))
    const2 = lambda b, c: (0, 0)
    in_specs = [
        pl.BlockSpec((q, D_SSM), rmap(z_col)),
        pl.BlockSpec((q, CONV_DIM), rmap(xbc_col)),
        pl.BlockSpec((q, LANES), rmap(dt_col)),
        pl.BlockSpec((None, CONV_W - 1, CONV_DIM), lambda b, c: (b, 0, 0)),
        pl.BlockSpec((None, H_SSM * P_SSM, N_SSM), lambda b, c: (b, 0, 0)),
        pl.BlockSpec((CONV_W, CONV_DIM), const2),
        pl.BlockSpec((1, CONV_DIM), const2),
        pl.BlockSpec((1, LANES), const2),
        pl.BlockSpec((1, D_SSM), const2),
        pl.BlockSpec((1, D_SSM), const2),
    ]
    args = [proj32, proj32, proj32, cst, s0, cw, cb, alog, dsk, ng]
    aliases = {}
    if dst is not None:
        in_specs.append(pl.BlockSpec(memory_space=pl.ANY))
        args.append(dst)
        aliases = {len(args) - 1: 0}
    return pl.pallas_call(
        functools.partial(_ssd_kernel, q=q, l_valid=l_valid, n_chunks=n_chunks, aliased=dst is not None),
        name="ssd_mixer",
        grid=(n_seq, n_chunks),
        in_specs=in_specs,
        out_specs=[pl.BlockSpec((q, D_SSM), rmap(0)),
                   pl.BlockSpec((None, H_SSM * P_SSM, N_SSM), lambda b, c: (b, 0, 0)),
                   pl.BlockSpec((None, CONV_W - 1, CONV_DIM), lambda b, c: (b, 0, 0))],
        out_shape=[jax.ShapeDtypeStruct((proj32.shape[0], D_SSM), BF16),
                   jax.ShapeDtypeStruct((n_seq, H_SSM * P_SSM, N_SSM), F32),
                   jax.ShapeDtypeStruct((n_seq, CONV_W - 1, CONV_DIM), F32)],
        input_output_aliases=aliases,
        scratch_shapes=[pltpu.VMEM((q + SUBLANES, CONV_DIM), F32),
                        pltpu.VMEM((H_SSM * P_SSM, N_SSM), F32),
                        pltpu.VMEM((q, D_SSM), F32)],
        compiler_params=_cparams(("arbitrary", "arbitrary")),
    )(*args)


def _round_up(x, m):
    return (x + m - 1) // m * m


def _pad_cols(w, n):
    return jnp.pad(w, ((0, 0), (0, n - w.shape[1])))


def kernel(x_prompt, x_sample, state_s5_re, state_s5_im, cache_fox_k, cache_fox_v, cache_fox_logf, state_ssd, state_conv, cache_sb_k, cache_sb_v, meta_tokens, norm_ffn1_pre, norm_ffn1_post, norm_mix_pre, norm_mix_post, norm_ffn2_pre, norm_ffn2_post, ffn1_w_gate, ffn1_w_up, ffn1_w_down, ffn2_w_gate, ffn2_w_up, ffn2_w_down, ab_w_in, fox_b_f, s5_a_re, s5_a_im, s5_log_dt, s5_b_re, s5_b_im, s5_c_re, s5_c_im, s5_d, s5_w_glu, s5_b_glu, ab_w_out, cd_w_in, ssd_conv_w, ssd_conv_b, ssd_dt_bias, ssd_a_log, ssd_d, ssd_norm, cd_w_out):
    bp, seq, d = x_prompt.shape
    bs, ls, _ = x_sample.shape
    past = cache_fox_k.shape[2]
    depth = norm_ffn1_pre.shape[0]
    assert d == D_MODEL and depth == 2 and ab_w_in.shape[0] == 1 and cd_w_in.shape[0] == 1
    l0 = N_META + seq
    lp = _round_up(l0, SEQ_TILE)
    n_pt = lp // SEQ_TILE
    rows_s = bs * ls
    assert rows_s % SEQ_TILE == 0 and SEQ_TILE % ls == 0 and ls % 16 == 0 and l0 >= CONV_W
    t = bp * lp + rows_s
    tm = next(c for c in (1024, 512, 256) if t % c == 0)
    tf = 512
    s_blk0 = bp * n_pt
    lk_s = _round_up(past + ls, SEQ_TILE)

    meta = meta_tokens.astype(F32)
    zpad = jnp.zeros((lp - l0, d), F32)
    pieces = []
    for b in range(bp):
        pieces += [meta, x_prompt[b], zpad]
    pieces.append(x_sample.reshape(rows_s, d))
    h = jnp.concatenate(pieces, axis=0)

    g3 = lambda g: g.reshape(depth, 1, d)
    w16 = lambda w: w.astype(BF16)
    ffn1 = (g3(norm_ffn1_pre), g3(norm_ffn1_post), ffn1_w_gate, ffn1_w_up, ffn1_w_down)
    ffn2 = (g3(norm_ffn2_pre), g3(norm_ffn2_post), ffn2_w_gate, ffn2_w_up, ffn2_w_down)
    g_mix_pre, g_mix_post = g3(norm_mix_pre), g3(norm_mix_post)

    def prompt_rows(x):
        return x[:bp * lp].reshape(bp, lp, -1)[:, :l0]

    def sample_rows(x):
        return x[bp * lp:].reshape(bs, ls, -1)

    h = _ffn_half(h, *ffn1, 0, tm, tf)

    n_main = D_S5 + 3 * D_ATT
    n_ab = n_main + PROJ_TN
    w_ab = w16(_pad_cols(ab_w_in[0], n_ab))
    bias_ab = jnp.zeros((1, PROJ_TN), F32).at[0, :H_ATT].set(fox_b_f[0])
    p32, p16 = _inproj(h, g_mix_pre, w_ab, bias_ab, 0, tm, "log_sigmoid")

    logf = p32[:, n_main:n_main + H_ATT]
    kf = p32[:, D_S5 + D_ATT:D_S5 + 2 * D_ATT]
    vf = p32[:, D_S5 + 2 * D_ATT:n_main]

    tab, bfull, cfull = _s5_prep(s5_a_re[0], s5_a_im[0], s5_log_dt[0], s5_b_re[0], s5_b_im[0],
                                 s5_c_re[0], s5_c_im[0])
    h0_p = jnp.zeros((bp, 1, 2 * S5_W), F32)
    y_s5, st_p = _s5_scan(p16, bfull, cfull, tab, h0_p, row_block0=0, n_groups=bp, n_tiles=n_pt,
                          seg=SEQ_TILE, chain=True, l_valid=l0)
    spt = SEQ_TILE // ls
    h0_s = jnp.concatenate([state_s5_re[0].reshape(bs, S5_W), state_s5_im[0].reshape(bs, S5_W)], axis=-1)
    y_s5, st_s = _s5_scan(p16, bfull, cfull, tab, h0_s.reshape(bs // spt, spt, 2 * S5_W), y_s5,
                          row_block0=s_blk0, n_groups=1, n_tiles=rows_s // SEQ_TILE,
                          seg=ls, chain=False, l_valid=ls)

    n_hg = H_ATT // 2

    def head_major(x):
        return jnp.transpose(x, (0, 2, 1))

    logf_p = p32[:bp * lp, n_main:n_main + H_ATT].reshape(bp, lp, H_ATT)
    fcum_p, fmin_p = _time_cumsum(head_major(logf_p))
    fqr_p = fcum_p.reshape(bp, n_hg, 2, lp)
    fkc_p = jnp.transpose(fqr_p, (0, 1, 3, 2))
    fmin_p = jnp.transpose(fmin_p.reshape(n_pt, bp, n_hg, 2, 1), (1, 2, 0, 3, 4))
    qcol = D_S5 // LANES
    att_cols = dict(q_col_block0=qcol, k_col_block0=qcol + D_ATT // LANES, v_col_block0=qcol + 2 * D_ATT // LANES)
    fox = _fox_prompt(p16, fqr_p, fkc_p, fmin_p, n_seq=bp, n_qt=n_pt, lk=lp, **att_cols)

    logf_s = jnp.concatenate([cache_fox_logf[0].astype(F32), logf[bp * lp:].reshape(bs, ls, H_ATT),
                              jnp.zeros((bs, lk_s - past - ls, H_ATT), F32)], axis=1)
    fcum_s = _time_cumsum(head_major(logf_s))[0].reshape(bs, n_hg, 2, lk_s)
    fq_s = jnp.transpose(fcum_s[:, :, :, past:past + ls], (0, 1, 3, 2))
    fox = _sample_attention("fox", p16, cache_fox_k[0].reshape(bs, past, D_ATT),
                            cache_fox_v[0].reshape(bs, past, D_ATT), fox, fq_s, fcum_s,
                            n_seq=bs, ls=ls, row_block0=bp * lp // ls, **att_cols)

    h = _mixout(h, g_mix_post, 0, w16(ab_w_out[0]), fox, tm, ya=y_s5, u=p32,
                dsk=s5_d[0].reshape(1, D_S5), wglu=w16(s5_w_glu[0]), bglu=s5_b_glu[0].reshape(1, D_S5))
    h = _ffn_half(h, *ffn2, 0, tm, tf)

    def split_state(st):
        return (st[:, :S5_W].reshape(1, -1, G_S5, N_S5), st[:, S5_W:].reshape(1, -1, G_S5, N_S5))

    s5_re_p, s5_im_p = split_state(st_p.reshape(bp, 2 * S5_W))
    s5_re_s, s5_im_s = split_state(st_s.reshape(bs, 2 * S5_W))
    heads = lambda x: x.reshape(1, x.shape[0], x.shape[1], H_ATT, DH)
    fox_k_p, fox_v_p = heads(prompt_rows(kf)), heads(prompt_rows(vf))
    fox_k_s, fox_v_s = heads(sample_rows(kf)), heads(sample_rows(vf))
    fox_logf_p, fox_logf_s = prompt_rows(logf)[None], sample_rows(logf)[None]

    h = _ffn_half(h, *ffn1, 1, tm, tf)

    wc = cd_w_in[0]
    o_xbc, o_dt = D_SSM, D_SSM + CONV_DIM
    o_q = CONV_DIM + D_SSM
    n_main_cd = o_q + 3 * D_ATT
    n_cd = n_main_cd + PROJ_TN
    w_cd = w16(_pad_cols(jnp.concatenate([wc[:, o_xbc:o_dt], wc[:, :o_xbc], wc[:, o_dt + H_SSM:],
                                          wc[:, o_dt:o_dt + H_SSM]], axis=1), n_cd))
    bias_cd = jnp.zeros((1, PROJ_TN), F32).at[0, :H_SSM].set(ssd_dt_bias[0])
    c32, c16 = _inproj(h, g_mix_pre, w_cd, bias_cd, 1, tm, "softplus")
    kc = c32[:, o_q + D_ATT:o_q + 2 * D_ATT]
    vc = c32[:, o_q + 2 * D_ATT:n_main_cd]

    cw = ssd_conv_w[0]
    cb = ssd_conv_b[0].reshape(1, CONV_DIM)
    alog = jnp.zeros((1, LANES), F32).at[0, :H_SSM].set(ssd_a_log[0])
    dsk = jnp.repeat(ssd_d[0], P_SSM).reshape(1, D_SSM)
    ng = ssd_norm[0].reshape(1, D_SSM)
    ssd_cols = dict(z_col=CONV_DIM // D_SSM, xbc_col=0, dt_col=n_main_cd // LANES)
    ssd_out, ssd_st_p, conv_p = _ssd(c32, jnp.zeros((bp, CONV_W - 1, CONV_DIM), F32),
                                     jnp.zeros((bp, H_SSM * P_SSM, N_SSM), F32), cw, cb, alog, dsk, ng,
                                     n_seq=bp, n_chunks=n_pt, q=SEQ_TILE, l_valid=l0, row_block0=0, **ssd_cols)
    ssd_out, ssd_st_s, conv_s = _ssd(c32, state_conv[0], state_ssd[0].reshape(bs, H_SSM * P_SSM, N_SSM),
                                     cw, cb, alog, dsk, ng, ssd_out, n_seq=bs, n_chunks=1, q=ls, l_valid=ls,
                                     row_block0=bp * lp // ls, **ssd_cols)

    qcol_cd = o_q // LANES
    att_cols = dict(q_col_block0=qcol_cd, k_col_block0=qcol_cd + D_ATT // LANES,
                    v_col_block0=qcol_cd + 2 * D_ATT // LANES)
    sb = _sb_prompt(c16, n_seq=bp, n_qt=n_pt, lk=lp, **att_cols)
    sb = _sample_attention("sb", c16, cache_sb_k[0].reshape(bs, past, D_ATT),
                           cache_sb_v[0].reshape(bs, past, D_ATT), sb,
                           n_seq=bs, ls=ls, row_block0=bp * lp // ls, **att_cols)

    h = _mixout(h, g_mix_post, 1, w16(cd_w_out[0]), sb, tm, ya=ssd_out)
    h = _ffn_half(h, *ffn2, 1, tm, tf)

    ssd_p = ssd_st_p.reshape(1, bp, H_SSM, P_SSM, N_SSM)
    ssd_s = ssd_st_s.reshape(1, bs, H_SSM, P_SSM, N_SSM)
    sb_k_p, sb_v_p = heads(prompt_rows(kc)), heads(prompt_rows(vc))
    sb_k_s, sb_v_s = heads(sample_rows(kc)), heads(sample_rows(vc))

    y_prompt = h[:bp * lp].reshape(bp, lp, d)[:, N_META:l0]
    y_sample = h[bp * lp:].reshape(bs, ls, d)
    return (y_prompt, y_sample,
            s5_re_p, s5_im_p, fox_k_p, fox_v_p, fox_logf_p, ssd_p, conv_p[None], sb_k_p, sb_v_p,
            s5_re_s, s5_im_s, fox_k_s, fox_v_s, fox_logf_s, ssd_s, conv_s[None], sb_k_s, sb_v_s)
```

```python
import functools

import jax
import jax.numpy as jnp
from jax import lax
from jax.experimental import pallas as pl
from jax.experimental.pallas import tpu as pltpu

F32 = jnp.float32
BF16 = jnp.bfloat16

EPS = 1e-6
D_MODEL = 1024
DH = 64
N_META = 16
D_S5 = 512
S5_GROUP = 16
G_S5 = 32
N_S5 = 64
S5_W = G_S5 * N_S5
H_ATT = 8
D_ATT = H_ATT * DH
D_SSM = 512
P_SSM = 64
H_SSM = 8
G_SSM = 2
N_SSM = 128
CONV_W = 4
CONV_DIM = D_SSM + 2 * G_SSM * N_SSM

LANES = 128
SUBLANES = 8
SEQ_TILE = 256
PROJ_TN = 256
VMEM_LIMIT = 56 * 1024 * 1024

NT_DIMS = (((1,), (1,)), ((), ()))
EXP_ZERO = 110.0
M_INIT = -1e30


def _cparams(sem):
    return pltpu.CompilerParams(dimension_semantics=sem, vmem_limit_bytes=VMEM_LIMIT)


def _rms(x, g):
    return x * lax.rsqrt(jnp.mean(x * x, axis=-1, keepdims=True) + EPS) * g


def _sigmoid(x):
    return 1.0 / (1.0 + jnp.exp(-x))


def _softplus(x):
    return jnp.maximum(x, 0.0) + jnp.log1p(jnp.exp(-jnp.abs(x)))


def _ones_where(mask):
    return jnp.where(mask, 1.0, 0.0).astype(BF16)


def _split3(x):
    hi = x.astype(BF16)
    r1 = x - hi.astype(F32)
    mid = r1.astype(BF16)
    lo = (r1 - mid.astype(F32)).astype(BF16)
    return hi, mid, lo


def _split3_dot(tri, x):
    return sum(jnp.dot(tri, part, preferred_element_type=F32) for part in _split3(x))


def _split3_transpose(eye, x):
    return sum(lax.dot_general(eye, part, NT_DIMS, preferred_element_type=F32) for part in _split3(x))


def _ffn_kernel(h_ref, gpre_ref, gpost_ref, wg_ref, wu_ref, wd_ref, o_ref, xn_ref, acc_ref, *, nj):
    j = pl.program_id(1)

    @pl.when(j == 0)
    def _():
        xn_ref[...] = _rms(h_ref[...], gpre_ref[...]).astype(BF16)
        acc_ref[...] = jnp.zeros_like(acc_ref)

    xn = xn_ref[...]
    g = jnp.dot(xn, wg_ref[...].astype(BF16), preferred_element_type=F32)
    u = jnp.dot(xn, wu_ref[...].astype(BF16), preferred_element_type=F32)
    a = (g * _sigmoid(g)) * u
    acc_ref[...] += jnp.dot(a.astype(BF16), wd_ref[...].astype(BF16), preferred_element_type=F32)

    @pl.when(j == nj - 1)
    def _():
        o_ref[...] = h_ref[...] + 0.5 * _rms(acc_ref[...], gpost_ref[...])


def _ffn_half(h, gpre, gpost, wg, wu, wd, layer, tm, tf):
    t, d = h.shape
    ff = wg.shape[-1]
    nj = ff // tf
    return pl.pallas_call(
        functools.partial(_ffn_kernel, nj=nj),
        name="ffn_half",
        grid=(t // tm, nj),
        in_specs=[
            pl.BlockSpec((tm, d), lambda i, j: (i, 0)),
            pl.BlockSpec((None, 1, d), lambda i, j: (layer, 0, 0)),
            pl.BlockSpec((None, 1, d), lambda i, j: (layer, 0, 0)),
            pl.BlockSpec((None, d, tf), lambda i, j: (layer, 0, j)),
            pl.BlockSpec((None, d, tf), lambda i, j: (layer, 0, j)),
            pl.BlockSpec((None, tf, d), lambda i, j: (layer, j, 0)),
        ],
        out_specs=pl.BlockSpec((tm, d), lambda i, j: (i, 0)),
        out_shape=jax.ShapeDtypeStruct((t, d), F32),
        scratch_shapes=[pltpu.VMEM((tm, d), BF16), pltpu.VMEM((tm, d), F32)],
        compiler_params=_cparams(("parallel", "arbitrary")),
    )(h, gpre, gpost, wg, wu, wd)


def _inproj_kernel(h_ref, g_ref, w_ref, b_ref, o32_ref, o16_ref, xn_ref, *, nj, tn, tail):
    j = pl.program_id(1)

    @pl.when(j == 0)
    def _():
        xn_ref[...] = _rms(h_ref[...], g_ref[...]).astype(BF16)

    p = jnp.dot(xn_ref[...], w_ref[...], preferred_element_type=F32)
    o32_ref[...] = p
    o16_ref[...] = p.astype(BF16)

    @pl.when(j == nj - 1)
    def _():
        x = p[:, tn - PROJ_TN:] + b_ref[...]
        r = -_softplus(-x) if tail == "log_sigmoid" else _softplus(x)
        o32_ref[:, tn - PROJ_TN:] = r
        o16_ref[:, tn - PROJ_TN:] = r.astype(BF16)


def _inproj(h, g, w, gate_bias, layer, tm, tail):
    t, d = h.shape
    n = w.shape[-1]
    nj = 2
    tn = n // nj
    assert tn % LANES == 0 and tn >= PROJ_TN
    return pl.pallas_call(
        functools.partial(_inproj_kernel, nj=nj, tn=tn, tail=tail),
        name="mix_inproj",
        grid=(t // tm, nj),
        in_specs=[
            pl.BlockSpec((tm, d), lambda i, j: (i, 0)),
            pl.BlockSpec((None, 1, d), lambda i, j: (layer, 0, 0)),
            pl.BlockSpec((d, tn), lambda i, j: (0, j)),
            pl.BlockSpec((1, PROJ_TN), lambda i, j: (0, 0)),
        ],
        out_specs=[pl.BlockSpec((tm, tn), lambda i, j: (i, j)),
                   pl.BlockSpec((tm, tn), lambda i, j: (i, j))],
        out_shape=[jax.ShapeDtypeStruct((t, n), F32), jax.ShapeDtypeStruct((t, n), BF16)],
        scratch_shapes=[pltpu.VMEM((tm, d), BF16)],
        compiler_params=_cparams(("parallel", "arbitrary")),
    )(h, g, w, gate_bias)


def _gelu_tanh(x):
    return 0.5 * x * (1.0 + jnp.tanh(0.7978845608028654 * (x + 0.044715 * (x * x * x))))


def _mixout_kernel(*refs, s5_glu, n_prompt_tiles):
    from_prompt = pl.program_id(0) < n_prompt_tiles
    pick = lambda p_ref, s_ref: jnp.where(from_prompt, p_ref[...], s_ref[...])
    if s5_glu:
        (yap_ref, yas_ref, u_ref, dsk_ref, wglu_ref, bglu_ref, ybp_ref, ybs_ref,
         wo_ref, h_ref, gpost_ref, o_ref) = refs
        y = pick(yap_ref, yas_ref) + dsk_ref[...] * u_ref[...]
        g = _gelu_tanh(y)
        gate = _sigmoid(jnp.dot(g.astype(BF16), wglu_ref[...], preferred_element_type=F32) + bglu_ref[...])
        a = (g * gate).astype(BF16)
    else:
        (yap_ref, yas_ref, ybp_ref, ybs_ref, wo_ref, h_ref, gpost_ref, o_ref) = refs
        a = pick(yap_ref, yas_ref)
    half = a.shape[-1]
    out = (jnp.dot(a, wo_ref[:half, :], preferred_element_type=F32)
           + jnp.dot(pick(ybp_ref, ybs_ref), wo_ref[half:, :], preferred_element_type=F32))
    o_ref[...] = h_ref[...] + _rms(out, gpost_ref[...])


def _mixout(h, gpost, layer, wo, yb, tm, *, ya, u=None, dsk=None, wglu=None, bglu=None):
    t, d = h.shape
    half = yb[0].shape[-1]
    n_p = ya[0].shape[0] // tm
    assert ya[0].shape[0] % tm == 0 and ya[1].shape[0] % tm == 0 and ya[0].shape[0] + ya[1].shape[0] == t
    s5_glu = u is not None
    row = lambda i: (i, 0)
    const = lambda i: (0, 0)
    pair_specs = [pl.BlockSpec((tm, half), lambda i: (jnp.minimum(i, n_p - 1), 0)),
                  pl.BlockSpec((tm, half), lambda i: (jnp.maximum(i - n_p, 0), 0))]
    args, specs = list(ya), list(pair_specs)
    if s5_glu:
        args += [u, dsk, wglu, bglu]
        specs += [pl.BlockSpec((tm, half), row), pl.BlockSpec((1, half), const),
                  pl.BlockSpec((half, half), const), pl.BlockSpec((1, half), const)]
    args += [*yb, wo, h, gpost]
    specs += [*pair_specs, pl.BlockSpec((2 * half, d), const),
              pl.BlockSpec((tm, d), row), pl.BlockSpec((None, 1, d), lambda i: (layer, 0, 0))]
    return pl.pallas_call(
        functools.partial(_mixout_kernel, s5_glu=s5_glu, n_prompt_tiles=n_p),
        name="mix_out",
        grid=(t // tm,),
        in_specs=specs,
        out_specs=pl.BlockSpec((tm, d), row),
        out_shape=jax.ShapeDtypeStruct((t, d), F32),
        compiler_params=_cparams(("parallel",)),
    )(*args)


def _s5_prep_kernel(are_ref, aim_ref, ldt_ref, bre_ref, bim_ref, cre_ref, cim_ref,
                    tab_ref, bf_ref, cf_ref):
    a_re = are_ref[...]
    a_im = aim_ref[...]
    dt = jnp.exp(ldt_ref[...])
    mag = jnp.exp(dt * a_re)
    ab_re = mag * jnp.cos(dt * a_im)
    ab_im = mag * jnp.sin(dt * a_im)
    den = a_re * a_re + a_im * a_im
    nr = ab_re - 1.0
    coef_re = (nr * a_re + ab_im * a_im) / den
    coef_im = (ab_im * a_re - nr * a_im) / den

    pw_re, pw_im = [ab_re], [ab_im]
    for _ in range(SUBLANES - 1):
        pr, pi = pw_re[-1], pw_im[-1]
        pw_re.append(pr * ab_re - pi * ab_im)
        pw_im.append(pr * ab_im + pi * ab_re)
    sub = lax.broadcasted_iota(jnp.int32, (SUBLANES, S5_W), 0)
    for k in range(3):
        sh = 1 << k
        tab_ref[k] = jnp.where(sub >= sh, pw_re[sh - 1], 0.0)
        tab_ref[3 + k] = jnp.where(sub >= sh, pw_im[sh - 1], 0.0)
    q_re = jnp.zeros((SUBLANES, S5_W), F32)
    q_im = jnp.zeros((SUBLANES, S5_W), F32)
    for r in range(SUBLANES):
        q_re = jnp.where(sub == r, pw_re[r], q_re)
        q_im = jnp.where(sub == r, pw_im[r], q_im)
    tab_ref[6] = q_re
    tab_ref[7] = q_im

    rg = lax.broadcasted_iota(jnp.int32, (D_S5, S5_W), 0) // S5_GROUP
    cg = lax.broadcasted_iota(jnp.int32, (D_S5, S5_W), 1) // N_S5
    b_re = bre_ref[...]
    b_im = bim_ref[...]
    bb_re = coef_re * b_re - coef_im * b_im
    bb_im = coef_re * b_im + coef_im * b_re
    bf_ref[:, :S5_W] = jnp.where(rg == cg, bb_re, 0.0).astype(BF16)
    bf_ref[:, S5_W:] = jnp.where(rg == cg, bb_im, 0.0).astype(BF16)
    rg2 = lax.broadcasted_iota(jnp.int32, (S5_W, D_S5), 0) // N_S5
    cg2 = lax.broadcasted_iota(jnp.int32, (S5_W, D_S5), 1) // S5_GROUP
    cf_ref[:S5_W, :] = jnp.where(rg2 == cg2, cre_ref[...], 0.0).astype(BF16)
    cf_ref[S5_W:, :] = jnp.where(rg2 == cg2, -cim_ref[...], 0.0).astype(BF16)


def _s5_prep(a_re, a_im, log_dt, b_re, b_im, c_re, c_im):
    flat = lambda x: x.reshape(1, S5_W)
    ldt = jnp.broadcast_to(log_dt[:, None], (G_S5, N_S5))
    b_t = lambda b: jnp.tile(jnp.transpose(b, (2, 0, 1)).reshape(S5_GROUP, S5_W), (G_S5, 1))
    c_t = lambda c: jnp.tile(jnp.transpose(c, (0, 2, 1)).reshape(S5_W, S5_GROUP), (1, G_S5))
    return pl.pallas_call(
        _s5_prep_kernel,
        name="s5_prep",
        out_shape=[jax.ShapeDtypeStruct((8, SUBLANES, S5_W), F32),
                   jax.ShapeDtypeStruct((D_S5, 2 * S5_W), BF16),
                   jax.ShapeDtypeStruct((2 * S5_W, D_S5), BF16)],
        compiler_params=pltpu.CompilerParams(vmem_limit_bytes=VMEM_LIMIT),
    )(flat(a_re), flat(a_im), flat(ldt), b_t(b_re), b_t(b_im), c_t(c_re), c_t(c_im))


S5_CB = 256
S5_GROUPS_PER_TRIP = 4


def _s5_kernel(u_ref, bf_ref, cf_ref, tab_ref, h0_ref, y_ref, st_ref, bu_ref, car_ref,
               *, rows, seg, chain, last_tile, last_row):
    i = pl.program_id(1)
    w = S5_W
    bu_ref[...] = jnp.dot(u_ref[...], bf_ref[...], preferred_element_type=F32)
    if chain:
        @pl.when(i == 0)
        def _():
            car_ref[...] = h0_ref[...]

    for c in range(0, w, S5_CB):
        re_cols = slice(c, c + S5_CB)
        im_cols = slice(w + c, w + c + S5_CB)
        pr = [tab_ref[k, :, re_cols] for k in range(3)]
        pi = [tab_ref[3 + k, :, re_cols] for k in range(3)]
        qr = tab_ref[6, :, re_cols]
        qi = tab_ref[7, :, re_cols]
        for sg in range(rows // seg):
            src = car_ref if chain else h0_ref
            srow = 0 if chain else sg
            cr0 = src[srow:srow + 1, re_cols]
            ci0 = src[srow:srow + 1, im_cols]

            def body(a, carry, sg=sg, re_cols=re_cols, im_cols=im_cols, pr=pr, pi=pi, qr=qr, qi=qi):
                cr, ci = carry
                starts = [pl.multiple_of(sg * seg + (a * S5_GROUPS_PER_TRIP + u) * SUBLANES, SUBLANES)
                          for u in range(S5_GROUPS_PER_TRIP)]
                xs = [(bu_ref[pl.ds(r0, SUBLANES), re_cols], bu_ref[pl.ds(r0, SUBLANES), im_cols])
                      for r0 in starts]
                scanned = []
                for xr, xi in xs:
                    for k in range(3):
                        sr = pltpu.roll(xr, 1 << k, 0)
                        si = pltpu.roll(xi, 1 << k, 0)
                        xr, xi = xr + pr[k] * sr - pi[k] * si, xi + pr[k] * si + pi[k] * sr
                    scanned.append((xr, xi))
                done = []
                for xr, xi in scanned:
                    xr, xi = xr + qr * cr - qi * ci, xi + qr * ci + qi * cr
                    cr, ci = xr[SUBLANES - 1:SUBLANES, :], xi[SUBLANES - 1:SUBLANES, :]
                    done.append((xr, xi))
                for r0, (xr, xi) in zip(starts, done):
                    bu_ref[pl.ds(r0, SUBLANES), re_cols] = xr
                    bu_ref[pl.ds(r0, SUBLANES), im_cols] = xi
                return cr, ci

            cr, ci = lax.fori_loop(0, seg // (SUBLANES * S5_GROUPS_PER_TRIP), body, (cr0, ci0))
            if chain:
                car_ref[0:1, re_cols] = cr
                car_ref[0:1, im_cols] = ci
            else:
                st_ref[sg:sg + 1, re_cols] = cr
                st_ref[sg:sg + 1, im_cols] = ci

    y_ref[...] = jnp.dot(bu_ref[...].astype(BF16), cf_ref[...], preferred_element_type=F32)
    if chain:
        @pl.when(i == last_tile)
        def _():
            st_ref[...] = bu_ref[last_row:last_row + 1, :]


def _s5_scan(proj16, bfull, cfull, tab, h0, *, row_block0, n_groups, n_tiles, seg, chain, l_valid):
    rows = SEQ_TILE
    s = h0.shape[1]
    last = l_valid - 1
    assert seg % (SUBLANES * S5_GROUPS_PER_TRIP) == 0
    kern = functools.partial(_s5_kernel, rows=rows, seg=seg, chain=chain,
                             last_tile=last // rows, last_row=last % rows)
    if chain:
        umap = lambda b, i: (row_block0 + b * n_tiles + i, 0)
        hmap = lambda b, i: (b, 0, 0)
    else:
        umap = lambda b, i: (row_block0 + i, 0)
        hmap = lambda b, i: (i, 0, 0)
    const2 = lambda b, i: (0, 0)
    in_specs = [
        pl.BlockSpec((rows, D_S5), umap),
        pl.BlockSpec((D_S5, 2 * S5_W), const2),
        pl.BlockSpec((2 * S5_W, D_S5), const2),
        pl.BlockSpec((8, SUBLANES, S5_W), lambda b, i: (0, 0, 0)),
        pl.BlockSpec((None, s, 2 * S5_W), hmap),
    ]
    return pl.pallas_call(
        kern,
        name="s5_scan",
        grid=(n_groups, n_tiles),
        in_specs=in_specs,
        out_specs=[pl.BlockSpec((rows, D_S5), lambda b, i: (b * n_tiles + i, 0)),
                   pl.BlockSpec((None, s, 2 * S5_W), hmap)],
        out_shape=[jax.ShapeDtypeStruct((n_groups * n_tiles * rows, D_S5), F32),
                   jax.ShapeDtypeStruct(h0.shape, F32)],
        scratch_shapes=[pltpu.VMEM((rows, 2 * S5_W), F32), pltpu.VMEM((1, 2 * S5_W), F32)],
        compiler_params=_cparams(("arbitrary", "arbitrary")),
    )(proj16, bfull, cfull, tab, h0)


def _cumsum_kernel(x_ref, o_ref, pmin_ref, car_ref, min_ref, *, tk):
    j = pl.program_id(0)

    @pl.when(j == 0)
    def _():
        car_ref[...] = jnp.zeros_like(car_ref)
        min_ref[...] = jnp.full_like(min_ref, jnp.inf)

    r = lax.broadcasted_iota(jnp.int32, (tk, tk), 0)
    c = lax.broadcasted_iota(jnp.int32, (tk, tk), 1)
    tri = _ones_where(r <= c)
    acc = sum(jnp.dot(part, tri, preferred_element_type=F32) for part in _split3(x_ref[...])) + car_ref[...]
    o_ref[...] = acc
    car_ref[...] = acc[:, tk - 1:tk]
    run_min = jnp.minimum(min_ref[...], jnp.min(acc, axis=-1, keepdims=True))
    min_ref[...] = run_min
    pmin_ref[...] = run_min


def _time_cumsum(x):
    b, hh, length = x.shape
    tk = SEQ_TILE
    rows = b * hh
    fcum, pmin = pl.pallas_call(
        functools.partial(_cumsum_kernel, tk=tk),
        name="time_cumsum",
        grid=(length // tk,),
        in_specs=[pl.BlockSpec((rows, tk), lambda j: (0, j))],
        out_specs=[pl.BlockSpec((rows, tk), lambda j: (0, j)),
                   pl.BlockSpec((None, rows, 1), lambda j: (j, 0, 0))],
        out_shape=[jax.ShapeDtypeStruct((rows, length), F32),
                   jax.ShapeDtypeStruct((length // tk, rows, 1), F32)],
        scratch_shapes=[pltpu.VMEM((rows, 1), F32), pltpu.VMEM((rows, 1), F32)],
        compiler_params=_cparams(("arbitrary",)),
    )(x.reshape(rows, length))
    return fcum.reshape(b, hh, length), pmin.reshape(length // tk, b, hh, 1)


def _head_pair_queries(q_ref):
    lane = lax.broadcasted_iota(jnp.int32, (1, LANES), 1)
    q = q_ref[...] * jnp.asarray(DH ** -0.5, BF16)
    zero = jnp.zeros_like(q)
    return lane, (jnp.where(lane < DH, q, zero), jnp.where(lane >= DH, q, zero))


def _fox_sample_kernel(q_ref, kn_ref, vn_ref, kp_ref, vp_ref, fq_ref, fk_ref, o_ref, *, ls, past):
    lane, qs = _head_pair_queries(q_ref)
    kp = kp_ref[...].astype(BF16)
    vp = vp_ref[...].astype(BF16)
    kn = kn_ref[...]
    vn = vn_ref[...]
    rr = lax.broadcasted_iota(jnp.int32, (ls, ls), 0)
    cc = lax.broadcasted_iota(jnp.int32, (ls, ls), 1)
    outs = []
    for h in range(2):
        fq = fq_ref[:, h:h + 1]
        s_old = lax.dot_general(qs[h], kp, NT_DIMS, preferred_element_type=F32) + fq - fk_ref[h:h + 1, :past]
        s_new = lax.dot_general(qs[h], kn, NT_DIMS, preferred_element_type=F32) + fq \
            - fk_ref[h:h + 1, past:past + ls]
        s_new = jnp.where(cc <= rr, s_new, -jnp.inf)
        m = jnp.maximum(jnp.max(s_old, axis=-1, keepdims=True), jnp.max(s_new, axis=-1, keepdims=True))
        p_old = jnp.exp(s_old - m)
        p_new = jnp.exp(s_new - m)
        l = jnp.sum(p_old, axis=-1, keepdims=True) + jnp.sum(p_new, axis=-1, keepdims=True)
        o = (jnp.dot(p_old.astype(BF16), vp, preferred_element_type=F32)
             + jnp.dot(p_new.astype(BF16), vn, preferred_element_type=F32))
        outs.append(o / l)
    o_ref[...] = jnp.where(lane < DH, outs[0], outs[1]).astype(BF16)


def _fox_t_kernel(q_ref, k_ref, v_ref, fqr_ref, fkc_ref, fmin_ref, o_ref,
                  m_ref, l_ref, acc_ref, vt_ref, fkb_ref, kabs_ref, s_ref, p_ref, *, tq, tk, n_kt):
    assert tq == tk
    i = pl.program_id(2)
    lane, qs = _head_pair_queries(q_ref)
    qpos0 = i * tq
    rep = tq // LANES

    @pl.when(i == 0)
    def _():
        kabs_ref[...] = jnp.broadcast_to(jnp.max(jnp.abs(k_ref[...]), axis=0, keepdims=True), kabs_ref.shape)
        er = lax.broadcasted_iota(jnp.int32, (LANES, LANES), 0)
        ec = lax.broadcasted_iota(jnp.int32, (LANES, LANES), 1)
        eye = _ones_where(er == ec)
        for h in range(2):
            fkb_ref[h] = jnp.broadcast_to(fkc_ref[:, h:h + 1], fkb_ref.shape[1:])

        def transpose_tile(j, c):
            vj = v_ref[pl.ds(pl.multiple_of(j * tk, tk), tk), :]
            vt_ref[j] = lax.dot_general(eye, vj, NT_DIMS, preferred_element_type=F32).astype(BF16)
            return c

        lax.fori_loop(0, n_kt, transpose_tile, 0)

    m_ref[...] = jnp.full_like(m_ref, M_INIT)
    l_ref[...] = jnp.zeros_like(l_ref)
    acc_ref[...] = jnp.zeros_like(acc_ref)
    kabs = kabs_ref[...]
    slack = [lax.dot_general(kabs, jnp.abs(qs[h]), NT_DIMS, preferred_element_type=F32)[0:1, :]
             + fqr_ref[h:h + 1, :] for h in range(2)]

    def worth_visiting(j):
        jc = jnp.maximum(j, 0)
        gap = jnp.maximum(jnp.max(slack[0] - m_ref[0] - fmin_ref[jc, 0:1, :]),
                          jnp.max(slack[1] - m_ref[1] - fmin_ref[jc, 1:2, :]))
        return gap > -EXP_ZERO

    def scores(j):
        kj = k_ref[pl.ds(pl.multiple_of(j * tk, tk), tk), :]
        return [lax.dot_general(kj, qs[h], NT_DIMS, preferred_element_type=F32) for h in range(2)]

    def weights(j, s, h, vis):
        fk = fkb_ref[h, pl.ds(pl.multiple_of(j * tk, tk), tk), :]
        s = s + fqr_ref[h:h + 1, :] - jnp.concatenate([fk] * rep, axis=1)
        if vis is not None:
            s = jnp.where(vis, s, -jnp.inf)
        m_prev = m_ref[h]
        m_new = jnp.maximum(m_prev, jnp.max(s, axis=0, keepdims=True))
        alpha = jnp.exp(m_prev - m_new)
        p = jnp.exp(s - m_new)
        l_ref[h] = alpha * l_ref[h] + jnp.sum(p, axis=0, keepdims=True)
        m_ref[h] = m_new
        p_ref[h] = p.astype(BF16)
        return alpha

    def weighted_values(j, h):
        return jnp.dot(vt_ref[j, h * DH:(h + 1) * DH, :], p_ref[h], preferred_element_type=F32)

    def stage_scores(slot, j):
        nxt = scores(jnp.maximum(j, 0))
        s_ref[slot, 0] = nxt[0]
        s_ref[slot, 1] = nxt[1]

    first = scores(i)
    stage_scores(1, i - 1)
    kpos = i * tk + lax.broadcasted_iota(jnp.int32, (tk, tq), 0)
    qpos = qpos0 + lax.broadcasted_iota(jnp.int32, (tk, tq), 1)
    for h in range(2):
        weights(i, first[h], h, kpos <= qpos)

    def more(c):
        return c[1]

    def trip(c):
        t, _ = c
        j = i - t
        slot = t % 2
        for h in range(2):
            rows = slice(h * DH, (h + 1) * DH)
            pv = weighted_values(j + 1, h)
            alpha = weights(j, s_ref[slot, h], h, None)
            acc_ref[rows, :] = alpha * (acc_ref[rows, :] + pv)
        stage_scores(1 - slot, j - 1)
        return t + 1, jnp.logical_and(j >= 1, worth_visiting(j - 1))

    visited, _ = lax.while_loop(more, trip, (1, jnp.logical_and(i >= 1, worth_visiting(i - 1))))
    j_fin = i - (visited - 1)
    for h in range(2):
        rows = slice(h * DH, (h + 1) * DH)
        acc_ref[rows, :] = acc_ref[rows, :] + weighted_values(j_fin, h)
    o_t = jnp.concatenate([acc_ref[0:DH, :] / l_ref[0], acc_ref[DH:, :] / l_ref[1]], axis=0).astype(BF16)
    qr = lax.broadcasted_iota(jnp.int32, (tq, tq), 0)
    qc = lax.broadcasted_iota(jnp.int32, (tq, tq), 1)
    o_ref[...] = lax.dot_general(_ones_where(qr == qc), o_t, NT_DIMS,
                                 preferred_element_type=F32).astype(BF16)


def _fox_prompt(p16, fqr, fkc, fmin, *, n_seq, n_qt, lk, q_col_block0, k_col_block0, v_col_block0):
    tq = tk = SEQ_TILE
    n_hg = H_ATT // 2
    n_kt = lk // tk
    return pl.pallas_call(
        functools.partial(_fox_t_kernel, tq=tq, tk=tk, n_kt=n_kt),
        name="fox_attention_t",
        grid=(n_seq, n_hg, n_qt),
        in_specs=[
            pl.BlockSpec((tq, LANES), lambda b, g, i: (b * n_qt + i, q_col_block0 + g)),
            pl.BlockSpec((lk, LANES), lambda b, g, i: (b, k_col_block0 + g)),
            pl.BlockSpec((lk, LANES), lambda b, g, i: (b, v_col_block0 + g)),
            pl.BlockSpec((None, None, 2, tq), lambda b, g, i: (b, g, 0, i)),
            pl.BlockSpec((None, None, lk, 2), lambda b, g, i: (b, g, 0, 0)),
            pl.BlockSpec((None, None, n_kt, 2, 1), lambda b, g, i: (b, g, 0, 0, 0)),
        ],
        out_specs=pl.BlockSpec((tq, LANES), lambda b, g, i: (b * n_qt + i, g)),
        out_shape=jax.ShapeDtypeStruct((n_seq * n_qt * tq, D_ATT), BF16),
        scratch_shapes=[pltpu.VMEM((2, 1, tq), F32), pltpu.VMEM((2, 1, tq), F32), pltpu.VMEM((LANES, tq), F32),
                        pltpu.VMEM((n_kt, LANES, tk), BF16), pltpu.VMEM((2, lk, LANES), F32),
                        pltpu.VMEM((2 * SUBLANES, LANES), BF16),
                        pltpu.VMEM((2, 2, tk, tq), F32), pltpu.VMEM((2, tk, tq), BF16)],
        compiler_params=_cparams(("parallel", "parallel", "arbitrary")),
    )(p16, p16, p16, fqr, fkc, fmin)


def _sb_weights(z, tri, vis, run):
    sp = jnp.maximum(z, 0.0) + jnp.log(1.0 + jnp.exp(-jnp.abs(z)))
    log_keep = -sp
    if vis is not None:
        log_keep = jnp.where(vis, log_keep, 0.0)
    hi = log_keep.astype(BF16)
    lo = (log_keep - hi.astype(F32)).astype(BF16)
    within = jnp.dot(hi, tri, preferred_element_type=F32) + jnp.dot(lo, tri, preferred_element_type=F32)
    wgt = jnp.exp((z - sp) + (within + run))
    if vis is not None:
        wgt = jnp.where(vis, wgt, 0.0)
    return wgt.astype(BF16), run + within[:, 0:1] + log_keep[:, 0:1]


def _strict_upper_ones(n):
    rr = lax.broadcasted_iota(jnp.int32, (n, n), 0)
    cc = lax.broadcasted_iota(jnp.int32, (n, n), 1)
    return _ones_where(rr > cc), rr, cc


def _sb_kernel(q_ref, k_ref, v_ref, o_ref, r_ref, acc_ref, z_ref, w_ref, *, tq, tk):
    assert tq == tk
    i = pl.program_id(2)
    lane, qs = _head_pair_queries(q_ref)
    r_ref[...] = jnp.zeros_like(r_ref)
    acc_ref[...] = jnp.zeros_like(acc_ref)
    tri, rr, cc = _strict_upper_ones(tk)

    def logits(j):
        kj = k_ref[pl.ds(pl.multiple_of(j * tk, tk), tk), :]
        return [lax.dot_general(qs[h], kj, NT_DIMS, preferred_element_type=F32) for h in range(2)]

    def stage_logits(slot, j):
        nxt = logits(jnp.maximum(j, 0))
        z_ref[slot, 0] = nxt[0]
        z_ref[slot, 1] = nxt[1]

    def weights(z, h, vis):
        w_ref[h], r_ref[h] = _sb_weights(z, tri, vis, r_ref[h])

    def weighted_values(j, h):
        vj = v_ref[pl.ds(pl.multiple_of(j * tk, tk), tk), :]
        return jnp.dot(w_ref[h], vj, preferred_element_type=F32)

    def worth_visiting():
        return jnp.max(jnp.maximum(r_ref[0], r_ref[1])) > -EXP_ZERO

    first = logits(i)
    stage_logits(1, i - 1)
    for h in range(2):
        weights(first[h], h, cc < rr)

    def more(c):
        return c[1]

    def trip(c):
        t, _ = c
        j = i - t
        slot = t % 2
        for h in range(2):
            acc_ref[h] += weighted_values(j + 1, h)
            weights(z_ref[slot, h], h, None)
        stage_logits(1 - slot, j - 1)
        return t + 1, jnp.logical_and(j >= 1, worth_visiting())

    visited, _ = lax.while_loop(more, trip, (1, jnp.logical_and(i >= 1, worth_visiting())))
    j_fin = i - (visited - 1)
    o_ref[...] = jnp.where(lane < DH, acc_ref[0] + weighted_values(j_fin, 0),
                           acc_ref[1] + weighted_values(j_fin, 1)).astype(BF16)


def _sb_prompt(c16, *, n_seq, n_qt, lk, q_col_block0, k_col_block0, v_col_block0):
    tq = tk = SEQ_TILE
    n_hg = H_ATT // 2
    return pl.pallas_call(
        functools.partial(_sb_kernel, tq=tq, tk=tk),
        name="sb_attention",
        grid=(n_seq, n_hg, n_qt),
        in_specs=[
            pl.BlockSpec((tq, LANES), lambda b, g, i: (b * n_qt + i, q_col_block0 + g)),
            pl.BlockSpec((lk, LANES), lambda b, g, i: (b, k_col_block0 + g)),
            pl.BlockSpec((lk, LANES), lambda b, g, i: (b, v_col_block0 + g)),
        ],
        out_specs=pl.BlockSpec((tq, LANES), lambda b, g, i: (b * n_qt + i, g)),
        out_shape=jax.ShapeDtypeStruct((n_seq * n_qt * tq, D_ATT), BF16),
        scratch_shapes=[pltpu.VMEM((2, tq, 1), F32), pltpu.VMEM((2, tq, LANES), F32),
                        pltpu.VMEM((2, 2, tq, tk), F32), pltpu.VMEM((2, tq, tk), BF16)],
        compiler_params=_cparams(("parallel", "parallel", "arbitrary")),
    )(c16, c16, c16)


def _sb_sample_kernel(q_ref, kn_ref, vn_ref, kp_ref, vp_ref, o_ref, r_ref, acc_ref, *, ls, past, tk):
    lane, qs = _head_pair_queries(q_ref)
    tri_new, rr, cc = _strict_upper_ones(ls)
    tri, _, _ = _strict_upper_ones(tk)
    kn = kn_ref[...]
    vn = vn_ref[...]
    for h in range(2):
        z = lax.dot_general(qs[h], kn, NT_DIMS, preferred_element_type=F32)
        wgt, r_ref[h] = _sb_weights(z, tri_new, cc < rr, jnp.zeros((ls, 1), F32))
        acc_ref[h] = jnp.dot(wgt, vn, preferred_element_type=F32)

    def worth_visiting():
        return jnp.max(jnp.maximum(r_ref[0], r_ref[1])) > -EXP_ZERO

    n_tiles = past // tk

    def more(c):
        t, go = c
        return jnp.logical_and(t < n_tiles, go)

    def trip(c):
        t, _ = c
        rows = pl.ds(pl.multiple_of((n_tiles - 1 - t) * tk, tk), tk)
        kj = kp_ref[rows, :].astype(BF16)
        vj = vp_ref[rows, :].astype(BF16)
        for h in range(2):
            z = lax.dot_general(qs[h], kj, NT_DIMS, preferred_element_type=F32)
            wgt, r_ref[h] = _sb_weights(z, tri, None, r_ref[h])
            acc_ref[h] += jnp.dot(wgt, vj, preferred_element_type=F32)
        return t + 1, worth_visiting()

    lax.while_loop(more, trip, (0, worth_visiting()))
    o_ref[...] = jnp.where(lane < DH, acc_ref[0], acc_ref[1]).astype(BF16)


def _sample_attention(kind, proj16, past_k, past_v, fq=None, fk=None, *, n_seq, ls, row_block0,
                      q_col_block0, k_col_block0, v_col_block0):
    n_hg = H_ATT // 2
    past = past_k.shape[1]
    row = lambda col0: (lambda b, g: (row_block0 + b, col0 + g))
    new_spec = lambda col0: pl.BlockSpec((ls, LANES), row(col0))
    past_spec = pl.BlockSpec((None, past, LANES), lambda b, g: (b, 0, g))
    in_specs = [new_spec(q_col_block0), new_spec(k_col_block0), new_spec(v_col_block0), past_spec, past_spec]
    args = [proj16, proj16, proj16, past_k, past_v]
    if kind == "fox":
        in_specs += [pl.BlockSpec((None, None, ls, 2), lambda b, g: (b, g, 0, 0)),
                     pl.BlockSpec((None, None, 2, fk.shape[-1]), lambda b, g: (b, g, 0, 0))]
        args += [fq, fk]
        kern = functools.partial(_fox_sample_kernel, ls=ls, past=past)
        scratch = []
    else:
        assert past % SEQ_TILE == 0
        kern = functools.partial(_sb_sample_kernel, ls=ls, past=past, tk=SEQ_TILE)
        scratch = [pltpu.VMEM((2, ls, 1), F32), pltpu.VMEM((2, ls, LANES), F32)]
    return pl.pallas_call(
        kern,
        name=kind + "_attention_sample",
        grid=(n_seq, n_hg),
        in_specs=in_specs,
        out_specs=pl.BlockSpec((ls, LANES), lambda b, g: (b, g)),
        out_shape=jax.ShapeDtypeStruct((n_seq * ls, D_ATT), BF16),
        scratch_shapes=scratch,
        compiler_params=_cparams(("parallel", "parallel")),
    )(*args)


def _ssd_kernel(z_ref, xbc_ref, dt_ref, cst_ref, s0_ref, cw_ref, cb_ref, alog_ref, dsk_ref, ng_ref,
                y_ref, sout_ref, cout_ref, xw_ref, s_ref, ysc_ref, *, q, l_valid, n_chunks):
    c = pl.program_id(1)
    hist = CONV_W - 1
    base = SUBLANES

    @pl.when(c == 0)
    def _():
        xw_ref[base - hist:base, :] = cst_ref[...]
        s_ref[...] = s0_ref[...]

    xw_ref[base:base + q, :] = xbc_ref[...]
    conv = cb_ref[...]
    for w in range(CONV_W):
        conv = conv + xw_ref[base - hist + w:base - hist + w + q, :] * cw_ref[w:w + 1, :]
    last = l_valid - 1

    @pl.when(c == last // q)
    def _():
        lr = base + last % q
        cout_ref[...] = xw_ref[lr - hist + 1:lr + 1, :]

    xw_ref[base - hist:base, :] = xw_ref[base + q - hist:base + q, :]

    act = conv * _sigmoid(conv)
    xs = act[:, :D_SSM]
    gw = G_SSM * N_SSM
    bm = act[:, D_SSM:D_SSM + gw].astype(BF16)
    cm = act[:, D_SSM + gw:].astype(BF16)

    rowg = c * q + lax.broadcasted_iota(jnp.int32, (q, LANES), 0)
    dt = jnp.where(rowg < l_valid, dt_ref[...], 0.0)
    adt = dt * (-jnp.exp(alog_ref[...]))
    rr = lax.broadcasted_iota(jnp.int32, (q, q), 0)
    cc = lax.broadcasted_iota(jnp.int32, (q, q), 1)
    causal = rr >= cc
    acs = _split3_dot(_ones_where(causal), adt)
    er = lax.broadcasted_iota(jnp.int32, (LANES, LANES), 0)
    ec = lax.broadcasted_iota(jnp.int32, (LANES, LANES), 1)
    eye = _ones_where(er == ec)
    acs_t = _split3_transpose(eye, acs)
    lane = lax.broadcasted_iota(jnp.int32, (1, LANES), 1)
    first = lane < P_SSM
    srow_first = lax.broadcasted_iota(jnp.int32, (LANES, 1), 0) < P_SSM
    heads_per_group = H_SSM // G_SSM

    cb_mats = []
    for g in range(G_SSM):
        cg = cm[:, g * N_SSM:(g + 1) * N_SSM]
        bg = bm[:, g * N_SSM:(g + 1) * N_SSM]
        cb_mats.append(lax.dot_general(cg, bg, NT_DIMS, preferred_element_type=F32))

    for pr in range(H_SSM // 2):
        h0, h1 = 2 * pr, 2 * pr + 1
        g = h0 // heads_per_group
        cg = cm[:, g * N_SSM:(g + 1) * N_SSM]
        bg = bm[:, g * N_SSM:(g + 1) * N_SSM]
        cols = slice(pr * LANES, (pr + 1) * LANES)
        xs_p = xs[:, cols]
        a0, a1 = acs[:, h0:h0 + 1], acs[:, h1:h1 + 1]
        xdt = xs_p * jnp.where(first, dt[:, h0:h0 + 1], dt[:, h1:h1 + 1])
        y = jnp.zeros((q, LANES), F32)
        for hh, a_col, keep in ((h0, a0, first), (h1, a1, jnp.logical_not(first))):
            seg = a_col - acs_t[hh:hh + 1, :]
            lmat = jnp.where(causal, jnp.exp(jnp.where(causal, seg, 0.0)), 0.0)
            m = (cb_mats[g] * lmat).astype(BF16)
            y = y + jnp.dot(m, jnp.where(keep, xdt, 0.0).astype(BF16), preferred_element_type=F32)
        st = s_ref[cols, :]
        y_off = lax.dot_general(cg, st.astype(BF16), NT_DIMS, preferred_element_type=F32)
        y = y + y_off * jnp.where(first, jnp.exp(a0), jnp.exp(a1))
        e0, e1 = a0[q - 1:q, :], a1[q - 1:q, :]
        wdec = jnp.where(first, jnp.exp(e0 - a0), jnp.exp(e1 - a1))
        xw = lax.dot_general(eye, (xdt * wdec).astype(BF16), NT_DIMS,
                             preferred_element_type=F32).astype(BF16)
        dec_rows = jnp.where(srow_first, jnp.exp(e0), jnp.exp(e1))
        s_ref[cols, :] = dec_rows * st + jnp.dot(xw, bg, preferred_element_type=F32)
        ysc_ref[:, cols] = y + dsk_ref[:, cols] * xs_p

    zt = z_ref[...]
    yg = ysc_ref[...] * (zt * _sigmoid(zt))
    gwid = D_SSM // G_SSM
    for g in range(G_SSM):
        blk = yg[:, g * gwid:(g + 1) * gwid]
        nrm = blk * lax.rsqrt(jnp.mean(blk * blk, axis=-1, keepdims=True) + EPS)
        y_ref[:, g * gwid:(g + 1) * gwid] = (nrm * ng_ref[:, g * gwid:(g + 1) * gwid]).astype(BF16)

    @pl.when(c == n_chunks - 1)
    def _():
        sout_ref[...] = s_ref[...]


def _ssd(proj32, cst, s0, cw, cb, alog, dsk, ng, *, n_seq, n_chunks, q, l_valid, row_block0,
         z_col, xbc_col, dt_col):
    rmap = lambda col: (lambda b, c: (row_block0 + b * n_chunks + c, col))
    const2 = lambda b, c: (0, 0)
    in_specs = [
        pl.BlockSpec((q, D_SSM), rmap(z_col)),
        pl.BlockSpec((q, CONV_DIM), rmap(xbc_col)),
        pl.BlockSpec((q, LANES), rmap(dt_col)),
        pl.BlockSpec((None, CONV_W - 1, CONV_DIM), lambda b, c: (b, 0, 0)),
        pl.BlockSpec((None, H_SSM * P_SSM, N_SSM), lambda b, c: (b, 0, 0)),
        pl.BlockSpec((CONV_W, CONV_DIM), const2),
        pl.BlockSpec((1, CONV_DIM), const2),
        pl.BlockSpec((1, LANES), const2),
        pl.BlockSpec((1, D_SSM), const2),
        pl.BlockSpec((1, D_SSM), const2),
    ]
    args = [proj32, proj32, proj32, cst, s0, cw, cb, alog, dsk, ng]
    return pl.pallas_call(
        functools.partial(_ssd_kernel, q=q, l_valid=l_valid, n_chunks=n_chunks),
        name="ssd_mixer",
        grid=(n_seq, n_chunks),
        in_specs=in_specs,
        out_specs=[pl.BlockSpec((q, D_SSM), lambda b, c: (b * n_chunks + c, 0)),
                   pl.BlockSpec((None, H_SSM * P_SSM, N_SSM), lambda b, c: (b, 0, 0)),
                   pl.BlockSpec((None, CONV_W - 1, CONV_DIM), lambda b, c: (b, 0, 0))],
        out_shape=[jax.ShapeDtypeStruct((n_seq * n_chunks * q, D_SSM), BF16),
                   jax.ShapeDtypeStruct((n_seq, H_SSM * P_SSM, N_SSM), F32),
                   jax.ShapeDtypeStruct((n_seq, CONV_W - 1, CONV_DIM), F32)],
        scratch_shapes=[pltpu.VMEM((q + SUBLANES, CONV_DIM), F32),
                        pltpu.VMEM((H_SSM * P_SSM, N_SSM), F32),
                        pltpu.VMEM((q, D_SSM), F32)],
        compiler_params=_cparams(("arbitrary", "arbitrary")),
    )(*args)


def _round_up(x, m):
    return (x + m - 1) // m * m


def _pad_cols(w, n):
    return jnp.pad(w, ((0, 0), (0, n - w.shape[1])))


def kernel(x_prompt, x_sample, state_s5_re, state_s5_im, cache_fox_k, cache_fox_v, cache_fox_logf, state_ssd, state_conv, cache_sb_k, cache_sb_v, meta_tokens, norm_ffn1_pre, norm_ffn1_post, norm_mix_pre, norm_mix_post, norm_ffn2_pre, norm_ffn2_post, ffn1_w_gate, ffn1_w_up, ffn1_w_down, ffn2_w_gate, ffn2_w_up, ffn2_w_down, ab_w_in, fox_b_f, s5_a_re, s5_a_im, s5_log_dt, s5_b_re, s5_b_im, s5_c_re, s5_c_im, s5_d, s5_w_glu, s5_b_glu, ab_w_out, cd_w_in, ssd_conv_w, ssd_conv_b, ssd_dt_bias, ssd_a_log, ssd_d, ssd_norm, cd_w_out):
    bp, seq, d = x_prompt.shape
    bs, ls, _ = x_sample.shape
    past = cache_fox_k.shape[2]
    depth = norm_ffn1_pre.shape[0]
    assert d == D_MODEL and depth == 2 and ab_w_in.shape[0] == 1 and cd_w_in.shape[0] == 1
    l0 = N_META + seq
    lp = _round_up(l0, SEQ_TILE)
    n_pt = lp // SEQ_TILE
    rows_s = bs * ls
    assert rows_s % SEQ_TILE == 0 and SEQ_TILE % ls == 0 and ls % 16 == 0 and l0 >= CONV_W
    t = bp * lp + rows_s
    tm = next(c for c in (1024, 512, 256) if t % c == 0)
    tm_mix = next(c for c in (1024, 512, 256) if (bp * lp) % c == 0 and rows_s % c == 0)
    tf = 512
    s_blk0 = bp * n_pt
    lk_s = _round_up(past + ls, SEQ_TILE)

    meta = meta_tokens.astype(F32)
    zpad = jnp.zeros((lp - l0, d), F32)
    pieces = []
    for b in range(bp):
        pieces += [meta, x_prompt[b], zpad]
    pieces.append(x_sample.reshape(rows_s, d))
    h = jnp.concatenate(pieces, axis=0)

    g3 = lambda g: g.reshape(depth, 1, d)
    w16 = lambda w: w.astype(BF16)
    ffn1 = (g3(norm_ffn1_pre), g3(norm_ffn1_post), ffn1_w_gate, ffn1_w_up, ffn1_w_down)
    ffn2 = (g3(norm_ffn2_pre), g3(norm_ffn2_post), ffn2_w_gate, ffn2_w_up, ffn2_w_down)
    g_mix_pre, g_mix_post = g3(norm_mix_pre), g3(norm_mix_post)

    def prompt_rows(x):
        return x[:bp * lp].reshape(bp, lp, -1)[:, :l0]

    def sample_rows(x):
        return x[bp * lp:].reshape(bs, ls, -1)

    h = _ffn_half(h, *ffn1, 0, tm, tf)

    n_main = D_S5 + 3 * D_ATT
    n_ab = n_main + PROJ_TN
    w_ab = w16(_pad_cols(ab_w_in[0], n_ab))
    bias_ab = jnp.zeros((1, PROJ_TN), F32).at[0, :H_ATT].set(fox_b_f[0])
    p32, p16 = _inproj(h, g_mix_pre, w_ab, bias_ab, 0, tm, "log_sigmoid")

    logf = p32[:, n_main:n_main + H_ATT]
    kf = p32[:, D_S5 + D_ATT:D_S5 + 2 * D_ATT]
    vf = p32[:, D_S5 + 2 * D_ATT:n_main]

    tab, bfull, cfull = _s5_prep(s5_a_re[0], s5_a_im[0], s5_log_dt[0], s5_b_re[0], s5_b_im[0],
                                 s5_c_re[0], s5_c_im[0])
    h0_p = jnp.zeros((bp, 1, 2 * S5_W), F32)
    y_s5_p, st_p = _s5_scan(p16, bfull, cfull, tab, h0_p, row_block0=0, n_groups=bp, n_tiles=n_pt,
                            seg=SEQ_TILE, chain=True, l_valid=l0)
    spt = SEQ_TILE // ls
    h0_s = jnp.concatenate([state_s5_re[0].reshape(bs, S5_W), state_s5_im[0].reshape(bs, S5_W)], axis=-1)
    y_s5_s, st_s = _s5_scan(p16, bfull, cfull, tab, h0_s.reshape(bs // spt, spt, 2 * S5_W),
                            row_block0=s_blk0, n_groups=1, n_tiles=rows_s // SEQ_TILE,
                            seg=ls, chain=False, l_valid=ls)

    n_hg = H_ATT // 2

    def head_major(x):
        return jnp.transpose(x, (0, 2, 1))

    logf_p = p32[:bp * lp, n_main:n_main + H_ATT].reshape(bp, lp, H_ATT)
    fcum_p, fmin_p = _time_cumsum(head_major(logf_p))
    fqr_p = fcum_p.reshape(bp, n_hg, 2, lp)
    fkc_p = jnp.transpose(fqr_p, (0, 1, 3, 2))
    fmin_p = jnp.transpose(fmin_p.reshape(n_pt, bp, n_hg, 2, 1), (1, 2, 0, 3, 4))
    qcol = D_S5 // LANES
    att_cols = dict(q_col_block0=qcol, k_col_block0=qcol + D_ATT // LANES, v_col_block0=qcol + 2 * D_ATT // LANES)
    fox_p = _fox_prompt(p16, fqr_p, fkc_p, fmin_p, n_seq=bp, n_qt=n_pt, lk=lp, **att_cols)

    logf_s = jnp.concatenate([cache_fox_logf[0].astype(F32), logf[bp * lp:].reshape(bs, ls, H_ATT),
                              jnp.zeros((bs, lk_s - past - ls, H_ATT), F32)], axis=1)
    fcum_s = _time_cumsum(head_major(logf_s))[0].reshape(bs, n_hg, 2, lk_s)
    fq_s = jnp.transpose(fcum_s[:, :, :, past:past + ls], (0, 1, 3, 2))
    fox_s = _sample_attention("fox", p16, cache_fox_k[0].reshape(bs, past, D_ATT),
                              cache_fox_v[0].reshape(bs, past, D_ATT), fq_s, fcum_s,
                              n_seq=bs, ls=ls, row_block0=bp * lp // ls, **att_cols)

    h = _mixout(h, g_mix_post, 0, w16(ab_w_out[0]), (fox_p, fox_s), tm_mix, ya=(y_s5_p, y_s5_s), u=p32,
                dsk=s5_d[0].reshape(1, D_S5), wglu=w16(s5_w_glu[0]), bglu=s5_b_glu[0].reshape(1, D_S5))
    h = _ffn_half(h, *ffn2, 0, tm, tf)

    def split_state(st):
        return (st[:, :S5_W].reshape(1, -1, G_S5, N_S5), st[:, S5_W:].reshape(1, -1, G_S5, N_S5))

    s5_re_p, s5_im_p = split_state(st_p.reshape(bp, 2 * S5_W))
    s5_re_s, s5_im_s = split_state(st_s.reshape(bs, 2 * S5_W))
    heads = lambda x: x.reshape(1, x.shape[0], x.shape[1], H_ATT, DH)
    fox_k_p, fox_v_p = heads(prompt_rows(kf)), heads(prompt_rows(vf))
    fox_k_s, fox_v_s = heads(sample_rows(kf)), heads(sample_rows(vf))
    fox_logf_p, fox_logf_s = prompt_rows(logf)[None], sample_rows(logf)[None]

    h = _ffn_half(h, *ffn1, 1, tm, tf)

    wc = cd_w_in[0]
    o_xbc, o_dt = D_SSM, D_SSM + CONV_DIM
    o_q = CONV_DIM + D_SSM
    n_main_cd = o_q + 3 * D_ATT
    n_cd = n_main_cd + PROJ_TN
    w_cd = w16(_pad_cols(jnp.concatenate([wc[:, o_xbc:o_dt], wc[:, :o_xbc], wc[:, o_dt + H_SSM:],
                                          wc[:, o_dt:o_dt + H_SSM]], axis=1), n_cd))
    bias_cd = jnp.zeros((1, PROJ_TN), F32).at[0, :H_SSM].set(ssd_dt_bias[0])
    c32, c16 = _inproj(h, g_mix_pre, w_cd, bias_cd, 1, tm, "softplus")
    kc = c32[:, o_q + D_ATT:o_q + 2 * D_ATT]
    vc = c32[:, o_q + 2 * D_ATT:n_main_cd]

    cw = ssd_conv_w[0]
    cb = ssd_conv_b[0].reshape(1, CONV_DIM)
    alog = jnp.zeros((1, LANES), F32).at[0, :H_SSM].set(ssd_a_log[0])
    dsk = jnp.repeat(ssd_d[0], P_SSM).reshape(1, D_SSM)
    ng = ssd_norm[0].reshape(1, D_SSM)
    ssd_cols = dict(z_col=CONV_DIM // D_SSM, xbc_col=0, dt_col=n_main_cd // LANES)
    ssd_out_p, ssd_st_p, conv_p = _ssd(c32, jnp.zeros((bp, CONV_W - 1, CONV_DIM), F32),
                                       jnp.zeros((bp, H_SSM * P_SSM, N_SSM), F32), cw, cb, alog, dsk, ng,
                                       n_seq=bp, n_chunks=n_pt, q=SEQ_TILE, l_valid=l0, row_block0=0, **ssd_cols)
    ssd_out_s, ssd_st_s, conv_s = _ssd(c32, state_conv[0], state_ssd[0].reshape(bs, H_SSM * P_SSM, N_SSM),
                                       cw, cb, alog, dsk, ng, n_seq=bs, n_chunks=1, q=ls, l_valid=ls,
                                       row_block0=bp * lp // ls, **ssd_cols)

    qcol_cd = o_q // LANES
    att_cols = dict(q_col_block0=qcol_cd, k_col_block0=qcol_cd + D_ATT // LANES,
                    v_col_block0=qcol_cd + 2 * D_ATT // LANES)
    sb_p = _sb_prompt(c16, n_seq=bp, n_qt=n_pt, lk=lp, **att_cols)
    sb_s = _sample_attention("sb", c16, cache_sb_k[0].reshape(bs, past, D_ATT),
                             cache_sb_v[0].reshape(bs, past, D_ATT),
                             n_seq=bs, ls=ls, row_block0=bp * lp // ls, **att_cols)

    h = _mixout(h, g_mix_post, 1, w16(cd_w_out[0]), (sb_p, sb_s), tm_mix, ya=(ssd_out_p, ssd_out_s))
    h = _ffn_half(h, *ffn2, 1, tm, tf)

    ssd_p = ssd_st_p.reshape(1, bp, H_SSM, P_SSM, N_SSM)
    ssd_s = ssd_st_s.reshape(1, bs, H_SSM, P_SSM, N_SSM)
    sb_k_p, sb_v_p = heads(prompt_rows(kc)), heads(prompt_rows(vc))
    sb_k_s, sb_v_s = heads(sample_rows(kc)), heads(sample_rows(vc))

    y_prompt = h[:bp * lp].reshape(bp, lp, d)[:, N_META:l0]
    y_sample = h[bp * lp:].reshape(bs, ls, d)
    return (y_prompt, y_sample,
            s5_re_p, s5_im_p, fox_k_p, fox_v_p, fox_logf_p, ssd_p, conv_p[None], sb_k_p, sb_v_p,
            s5_re_s, s5_im_s, fox_k_s, fox_v_s, fox_logf_s, ssd_s, conv_s[None], sb_k_s, sb_v_s)
```

```python
import functools

import jax
import jax.numpy as jnp
from jax import lax
from jax.experimental import pallas as pl
from jax.experimental.pallas import tpu as pltpu

F32 = jnp.float32
BF16 = jnp.bfloat16

EPS = 1e-6
D_MODEL = 1024
DH = 64
N_META = 16
D_S5 = 512
S5_GROUP = 16
G_S5 = 32
N_S5 = 64
S5_W = G_S5 * N_S5
H_ATT = 8
D_ATT = H_ATT * DH
D_SSM = 512
P_SSM = 64
H_SSM = 8
G_SSM = 2
N_SSM = 128
CONV_W = 4
CONV_DIM = D_SSM + 2 * G_SSM * N_SSM

LANES = 128
SUBLANES = 8
SEQ_TILE = 256
PROJ_TN = 256
VMEM_LIMIT = 56 * 1024 * 1024

NT_DIMS = (((1,), (1,)), ((), ()))
EXP_ZERO = 110.0
M_INIT = -1e30


def _cparams(sem):
    return pltpu.CompilerParams(dimension_semantics=sem, vmem_limit_bytes=VMEM_LIMIT)


def _rms(x, g):
    return x * lax.rsqrt(jnp.mean(x * x, axis=-1, keepdims=True) + EPS) * g


def _sigmoid(x):
    return 1.0 / (1.0 + jnp.exp(-x))


def _softplus(x):
    return jnp.maximum(x, 0.0) + jnp.log1p(jnp.exp(-jnp.abs(x)))


def _ones_where(mask):
    return jnp.where(mask, 1.0, 0.0).astype(BF16)


def _split3(x):
    hi = x.astype(BF16)
    r1 = x - hi.astype(F32)
    mid = r1.astype(BF16)
    lo = (r1 - mid.astype(F32)).astype(BF16)
    return hi, mid, lo


def _split3_dot(tri, x):
    return sum(jnp.dot(tri, part, preferred_element_type=F32) for part in _split3(x))


def _split3_transpose(eye, x):
    return sum(lax.dot_general(eye, part, NT_DIMS, preferred_element_type=F32) for part in _split3(x))


def _ffn_kernel(h_ref, gpre_ref, gpost_ref, wg_ref, wu_ref, wd_ref, o_ref, xn_ref, acc_ref, *, nj):
    j = pl.program_id(1)

    @pl.when(j == 0)
    def _():
        xn_ref[...] = _rms(h_ref[...], gpre_ref[...]).astype(BF16)
        acc_ref[...] = jnp.zeros_like(acc_ref)

    xn = xn_ref[...]
    g = jnp.dot(xn, wg_ref[...].astype(BF16), preferred_element_type=F32)
    u = jnp.dot(xn, wu_ref[...].astype(BF16), preferred_element_type=F32)
    a = (g * _sigmoid(g)) * u
    acc_ref[...] += jnp.dot(a.astype(BF16), wd_ref[...].astype(BF16), preferred_element_type=F32)

    @pl.when(j == nj - 1)
    def _():
        o_ref[...] = h_ref[...] + 0.5 * _rms(acc_ref[...], gpost_ref[...])


def _ffn_half(h, gpre, gpost, wg, wu, wd, layer, tm, tf):
    t, d = h.shape
    ff = wg.shape[-1]
    nj = ff // tf
    return pl.pallas_call(
        functools.partial(_ffn_kernel, nj=nj),
        name="ffn_half",
        grid=(t // tm, nj),
        in_specs=[
            pl.BlockSpec((tm, d), lambda i, j: (i, 0)),
            pl.BlockSpec((None, 1, d), lambda i, j: (layer, 0, 0)),
            pl.BlockSpec((None, 1, d), lambda i, j: (layer, 0, 0)),
            pl.BlockSpec((None, d, tf), lambda i, j: (layer, 0, j)),
            pl.BlockSpec((None, d, tf), lambda i, j: (layer, 0, j)),
            pl.BlockSpec((None, tf, d), lambda i, j: (layer, j, 0)),
        ],
        out_specs=pl.BlockSpec((tm, d), lambda i, j: (i, 0)),
        out_shape=jax.ShapeDtypeStruct((t, d), F32),
        scratch_shapes=[pltpu.VMEM((tm, d), BF16), pltpu.VMEM((tm, d), F32)],
        compiler_params=_cparams(("parallel", "arbitrary")),
    )(h, gpre, gpost, wg, wu, wd)


def _inproj_kernel(h_ref, g_ref, w_ref, b_ref, o32_ref, o16_ref, xn_ref, *, nj, tn, tail):
    j = pl.program_id(1)

    @pl.when(j == 0)
    def _():
        xn_ref[...] = _rms(h_ref[...], g_ref[...]).astype(BF16)

    p = jnp.dot(xn_ref[...], w_ref[...], preferred_element_type=F32)
    o32_ref[...] = p
    o16_ref[...] = p.astype(BF16)

    @pl.when(j == nj - 1)
    def _():
        x = p[:, tn - PROJ_TN:] + b_ref[...]
        r = -_softplus(-x) if tail == "log_sigmoid" else _softplus(x)
        o32_ref[:, tn - PROJ_TN:] = r
        o16_ref[:, tn - PROJ_TN:] = r.astype(BF16)


def _inproj(h, g, w, gate_bias, layer, tm, tail):
    t, d = h.shape
    n = w.shape[-1]
    nj = 2
    tn = n // nj
    assert tn % LANES == 0 and tn >= PROJ_TN
    return pl.pallas_call(
        functools.partial(_inproj_kernel, nj=nj, tn=tn, tail=tail),
        name="mix_inproj",
        grid=(t // tm, nj),
        in_specs=[
            pl.BlockSpec((tm, d), lambda i, j: (i, 0)),
            pl.BlockSpec((None, 1, d), lambda i, j: (layer, 0, 0)),
            pl.BlockSpec((d, tn), lambda i, j: (0, j)),
            pl.BlockSpec((1, PROJ_TN), lambda i, j: (0, 0)),
        ],
        out_specs=[pl.BlockSpec((tm, tn), lambda i, j: (i, j)),
                   pl.BlockSpec((tm, tn), lambda i, j: (i, j))],
        out_shape=[jax.ShapeDtypeStruct((t, n), F32), jax.ShapeDtypeStruct((t, n), BF16)],
        scratch_shapes=[pltpu.VMEM((tm, d), BF16)],
        compiler_params=_cparams(("parallel", "arbitrary")),
    )(h, g, w, gate_bias)


def _gelu_tanh(x):
    return 0.5 * x * (1.0 + jnp.tanh(0.7978845608028654 * (x + 0.044715 * (x * x * x))))


def _mixout_kernel(*refs, s5_glu, n_prompt_tiles):
    from_prompt = pl.program_id(0) < n_prompt_tiles
    pick = lambda p_ref, s_ref: jnp.where(from_prompt, p_ref[...], s_ref[...])
    if s5_glu:
        (yap_ref, yas_ref, u_ref, dsk_ref, wglu_ref, bglu_ref, ybp_ref, ybs_ref,
         wo_ref, h_ref, gpost_ref, o_ref) = refs
        y = pick(yap_ref, yas_ref) + dsk_ref[...] * u_ref[...]
        g = _gelu_tanh(y)
        gate = _sigmoid(jnp.dot(g.astype(BF16), wglu_ref[...], preferred_element_type=F32) + bglu_ref[...])
        a = (g * gate).astype(BF16)
    else:
        (yap_ref, yas_ref, ybp_ref, ybs_ref, wo_ref, h_ref, gpost_ref, o_ref) = refs
        a = pick(yap_ref, yas_ref)
    half = a.shape[-1]
    out = (jnp.dot(a, wo_ref[:half, :], preferred_element_type=F32)
           + jnp.dot(pick(ybp_ref, ybs_ref), wo_ref[half:, :], preferred_element_type=F32))
    o_ref[...] = h_ref[...] + _rms(out, gpost_ref[...])


def _mixout(h, gpost, layer, wo, yb, tm, *, ya, u=None, dsk=None, wglu=None, bglu=None):
    t, d = h.shape
    half = yb[0].shape[-1]
    n_p = ya[0].shape[0] // tm
    assert ya[0].shape[0] % tm == 0 and ya[1].shape[0] % tm == 0 and ya[0].shape[0] + ya[1].shape[0] == t
    s5_glu = u is not None
    row = lambda i: (i, 0)
    const = lambda i: (0, 0)
    pair_specs = [pl.BlockSpec((tm, half), lambda i: (jnp.minimum(i, n_p - 1), 0)),
                  pl.BlockSpec((tm, half), lambda i: (jnp.maximum(i - n_p, 0), 0))]
    args, specs = list(ya), list(pair_specs)
    if s5_glu:
        args += [u, dsk, wglu, bglu]
        specs += [pl.BlockSpec((tm, half), row), pl.BlockSpec((1, half), const),
                  pl.BlockSpec((half, half), const), pl.BlockSpec((1, half), const)]
    args += [*yb, wo, h, gpost]
    specs += [*pair_specs, pl.BlockSpec((2 * half, d), const),
              pl.BlockSpec((tm, d), row), pl.BlockSpec((None, 1, d), lambda i: (layer, 0, 0))]
    return pl.pallas_call(
        functools.partial(_mixout_kernel, s5_glu=s5_glu, n_prompt_tiles=n_p),
        name="mix_out",
        grid=(t // tm,),
        in_specs=specs,
        out_specs=pl.BlockSpec((tm, d), row),
        out_shape=jax.ShapeDtypeStruct((t, d), F32),
        compiler_params=_cparams(("parallel",)),
    )(*args)


def _s5_prep_kernel(are_ref, aim_ref, ldt_ref, bre_ref, bim_ref, cre_ref, cim_ref,
                    tab_ref, bf_ref, cf_ref):
    a_re = are_ref[...]
    a_im = aim_ref[...]
    dt = jnp.exp(ldt_ref[...])
    mag = jnp.exp(dt * a_re)
    ab_re = mag * jnp.cos(dt * a_im)
    ab_im = mag * jnp.sin(dt * a_im)
    den = a_re * a_re + a_im * a_im
    nr = ab_re - 1.0
    coef_re = (nr * a_re + ab_im * a_im) / den
    coef_im = (ab_im * a_re - nr * a_im) / den

    pw_re, pw_im = [ab_re], [ab_im]
    for _ in range(SUBLANES - 1):
        pr, pi = pw_re[-1], pw_im[-1]
        pw_re.append(pr * ab_re - pi * ab_im)
        pw_im.append(pr * ab_im + pi * ab_re)
    sub = lax.broadcasted_iota(jnp.int32, (SUBLANES, S5_W), 0)
    for k in range(3):
        sh = 1 << k
        tab_ref[k] = jnp.where(sub >= sh, pw_re[sh - 1], 0.0)
        tab_ref[3 + k] = jnp.where(sub >= sh, pw_im[sh - 1], 0.0)
    q_re = jnp.zeros((SUBLANES, S5_W), F32)
    q_im = jnp.zeros((SUBLANES, S5_W), F32)
    for r in range(SUBLANES):
        q_re = jnp.where(sub == r, pw_re[r], q_re)
        q_im = jnp.where(sub == r, pw_im[r], q_im)
    tab_ref[6] = q_re
    tab_ref[7] = q_im

    rg = lax.broadcasted_iota(jnp.int32, (D_S5, S5_W), 0) // S5_GROUP
    cg = lax.broadcasted_iota(jnp.int32, (D_S5, S5_W), 1) // N_S5
    b_re = bre_ref[...]
    b_im = bim_ref[...]
    bb_re = coef_re * b_re - coef_im * b_im
    bb_im = coef_re * b_im + coef_im * b_re
    bf_ref[:, :S5_W] = jnp.where(rg == cg, bb_re, 0.0).astype(BF16)
    bf_ref[:, S5_W:] = jnp.where(rg == cg, bb_im, 0.0).astype(BF16)
    rg2 = lax.broadcasted_iota(jnp.int32, (S5_W, D_S5), 0) // N_S5
    cg2 = lax.broadcasted_iota(jnp.int32, (S5_W, D_S5), 1) // S5_GROUP
    cf_ref[:S5_W, :] = jnp.where(rg2 == cg2, cre_ref[...], 0.0).astype(BF16)
    cf_ref[S5_W:, :] = jnp.where(rg2 == cg2, -cim_ref[...], 0.0).astype(BF16)


def _s5_prep(a_re, a_im, log_dt, b_re, b_im, c_re, c_im):
    flat = lambda x: x.reshape(1, S5_W)
    ldt = jnp.broadcast_to(log_dt[:, None], (G_S5, N_S5))
    b_t = lambda b: jnp.tile(jnp.transpose(b, (2, 0, 1)).reshape(S5_GROUP, S5_W), (G_S5, 1))
    c_t = lambda c: jnp.tile(jnp.transpose(c, (0, 2, 1)).reshape(S5_W, S5_GROUP), (1, G_S5))
    return pl.pallas_call(
        _s5_prep_kernel,
        name="s5_prep",
        out_shape=[jax.ShapeDtypeStruct((8, SUBLANES, S5_W), F32),
                   jax.ShapeDtypeStruct((D_S5, 2 * S5_W), BF16),
                   jax.ShapeDtypeStruct((2 * S5_W, D_S5), BF16)],
        compiler_params=pltpu.CompilerParams(vmem_limit_bytes=VMEM_LIMIT),
    )(flat(a_re), flat(a_im), flat(ldt), b_t(b_re), b_t(b_im), c_t(c_re), c_t(c_im))


S5_CB = 256
S5_GROUPS_PER_TRIP = 4


def _s5_kernel(u_ref, bf_ref, cf_ref, tab_ref, h0_ref, y_ref, st_ref, bu_ref, car_ref,
               *, rows, seg, chain, last_tile, last_row):
    i = pl.program_id(1)
    w = S5_W
    n_blk = D_S5 // LANES
    sw = w // n_blk
    for m in range(n_blk):
        u_blk = u_ref[:, m * LANES:(m + 1) * LANES]
        for part in (0, w):
            cols = slice(part + m * sw, part + (m + 1) * sw)
            bu_ref[:, cols] = jnp.dot(u_blk, bf_ref[m * LANES:(m + 1) * LANES, cols], preferred_element_type=F32)
    if chain:
        @pl.when(i == 0)
        def _():
            car_ref[...] = h0_ref[...]

    for c in range(0, w, S5_CB):
        re_cols = slice(c, c + S5_CB)
        im_cols = slice(w + c, w + c + S5_CB)
        pr = [tab_ref[k, :, re_cols] for k in range(3)]
        pi = [tab_ref[3 + k, :, re_cols] for k in range(3)]
        qr = tab_ref[6, :, re_cols]
        qi = tab_ref[7, :, re_cols]
        for sg in range(rows // seg):
            src = car_ref if chain else h0_ref
            srow = 0 if chain else sg
            cr0 = src[srow:srow + 1, re_cols]
            ci0 = src[srow:srow + 1, im_cols]

            def body(a, carry, sg=sg, re_cols=re_cols, im_cols=im_cols, pr=pr, pi=pi, qr=qr, qi=qi):
                cr, ci = carry
                starts = [pl.multiple_of(sg * seg + (a * S5_GROUPS_PER_TRIP + u) * SUBLANES, SUBLANES)
                          for u in range(S5_GROUPS_PER_TRIP)]
                xs = [(bu_ref[pl.ds(r0, SUBLANES), re_cols], bu_ref[pl.ds(r0, SUBLANES), im_cols])
                      for r0 in starts]
                scanned = []
                for xr, xi in xs:
                    for k in range(3):
                        sr = pltpu.roll(xr, 1 << k, 0)
                        si = pltpu.roll(xi, 1 << k, 0)
                        xr, xi = xr + pr[k] * sr - pi[k] * si, xi + pr[k] * si + pi[k] * sr
                    scanned.append((xr, xi))
                done = []
                for xr, xi in scanned:
                    xr, xi = xr + qr * cr - qi * ci, xi + qr * ci + qi * cr
                    cr, ci = xr[SUBLANES - 1:SUBLANES, :], xi[SUBLANES - 1:SUBLANES, :]
                    done.append((xr, xi))
                for r0, (xr, xi) in zip(starts, done):
                    bu_ref[pl.ds(r0, SUBLANES), re_cols] = xr
                    bu_ref[pl.ds(r0, SUBLANES), im_cols] = xi
                return cr, ci

            cr, ci = lax.fori_loop(0, seg // (SUBLANES * S5_GROUPS_PER_TRIP), body, (cr0, ci0))
            if chain:
                car_ref[0:1, re_cols] = cr
                car_ref[0:1, im_cols] = ci
            else:
                st_ref[sg:sg + 1, re_cols] = cr
                st_ref[sg:sg + 1, im_cols] = ci

    for m in range(n_blk):
        ch = slice(m * LANES, (m + 1) * LANES)
        y_ref[:, ch] = sum(jnp.dot(bu_ref[:, part + m * sw:part + (m + 1) * sw].astype(BF16),
                                   cf_ref[part + m * sw:part + (m + 1) * sw, ch], preferred_element_type=F32)
                           for part in (0, w))
    if chain:
        @pl.when(i == last_tile)
        def _():
            st_ref[...] = bu_ref[last_row:last_row + 1, :]


def _s5_scan(proj16, bfull, cfull, tab, h0, *, row_block0, n_groups, n_tiles, seg, chain, l_valid):
    rows = SEQ_TILE
    s = h0.shape[1]
    last = l_valid - 1
    assert seg % (SUBLANES * S5_GROUPS_PER_TRIP) == 0
    kern = functools.partial(_s5_kernel, rows=rows, seg=seg, chain=chain,
                             last_tile=last // rows, last_row=last % rows)
    if chain:
        umap = lambda b, i: (row_block0 + b * n_tiles + i, 0)
        hmap = lambda b, i: (b, 0, 0)
    else:
        umap = lambda b, i: (row_block0 + i, 0)
        hmap = lambda b, i: (i, 0, 0)
    const2 = lambda b, i: (0, 0)
    in_specs = [
        pl.BlockSpec((rows, D_S5), umap),
        pl.BlockSpec((D_S5, 2 * S5_W), const2),
        pl.BlockSpec((2 * S5_W, D_S5), const2),
        pl.BlockSpec((8, SUBLANES, S5_W), lambda b, i: (0, 0, 0)),
        pl.BlockSpec((None, s, 2 * S5_W), hmap),
    ]
    return pl.pallas_call(
        kern,
        name="s5_scan",
        grid=(n_groups, n_tiles),
        in_specs=in_specs,
        out_specs=[pl.BlockSpec((rows, D_S5), lambda b, i: (b * n_tiles + i, 0)),
                   pl.BlockSpec((None, s, 2 * S5_W), hmap)],
        out_shape=[jax.ShapeDtypeStruct((n_groups * n_tiles * rows, D_S5), F32),
                   jax.ShapeDtypeStruct(h0.shape, F32)],
        scratch_shapes=[pltpu.VMEM((rows, 2 * S5_W), F32), pltpu.VMEM((1, 2 * S5_W), F32)],
        compiler_params=_cparams(("arbitrary", "arbitrary")),
    )(proj16, bfull, cfull, tab, h0)


def _cumsum_kernel(x_ref, o_ref, pmin_ref, car_ref, min_ref, *, tk):
    j = pl.program_id(0)

    @pl.when(j == 0)
    def _():
        car_ref[...] = jnp.zeros_like(car_ref)
        min_ref[...] = jnp.full_like(min_ref, jnp.inf)

    r = lax.broadcasted_iota(jnp.int32, (tk, tk), 0)
    c = lax.broadcasted_iota(jnp.int32, (tk, tk), 1)
    tri = _ones_where(r <= c)
    acc = sum(jnp.dot(part, tri, preferred_element_type=F32) for part in _split3(x_ref[...])) + car_ref[...]
    o_ref[...] = acc
    car_ref[...] = acc[:, tk - 1:tk]
    run_min = jnp.minimum(min_ref[...], jnp.min(acc, axis=-1, keepdims=True))
    min_ref[...] = run_min
    pmin_ref[...] = run_min


def _time_cumsum(x):
    b, hh, length = x.shape
    tk = SEQ_TILE
    rows = b * hh
    fcum, pmin = pl.pallas_call(
        functools.partial(_cumsum_kernel, tk=tk),
        name="time_cumsum",
        grid=(length // tk,),
        in_specs=[pl.BlockSpec((rows, tk), lambda j: (0, j))],
        out_specs=[pl.BlockSpec((rows, tk), lambda j: (0, j)),
                   pl.BlockSpec((None, rows, 1), lambda j: (j, 0, 0))],
        out_shape=[jax.ShapeDtypeStruct((rows, length), F32),
                   jax.ShapeDtypeStruct((length // tk, rows, 1), F32)],
        scratch_shapes=[pltpu.VMEM((rows, 1), F32), pltpu.VMEM((rows, 1), F32)],
        compiler_params=_cparams(("arbitrary",)),
    )(x.reshape(rows, length))
    return fcum.reshape(b, hh, length), pmin.reshape(length // tk, b, hh, 1)


def _head_pair_queries(q_ref):
    lane = lax.broadcasted_iota(jnp.int32, (1, LANES), 1)
    q = q_ref[...] * jnp.asarray(DH ** -0.5, BF16)
    zero = jnp.zeros_like(q)
    return lane, (jnp.where(lane < DH, q, zero), jnp.where(lane >= DH, q, zero))


def _fox_sample_kernel(q_ref, kn_ref, vn_ref, kp_ref, vp_ref, fq_ref, fk_ref, o_ref, *, ls, past):
    lane, qs = _head_pair_queries(q_ref)
    kp = kp_ref[...].astype(BF16)
    vp = vp_ref[...].astype(BF16)
    kn = kn_ref[...]
    vn = vn_ref[...]
    rr = lax.broadcasted_iota(jnp.int32, (ls, ls), 0)
    cc = lax.broadcasted_iota(jnp.int32, (ls, ls), 1)
    outs = []
    for h in range(2):
        fq = fq_ref[:, h:h + 1]
        s_old = lax.dot_general(qs[h], kp, NT_DIMS, preferred_element_type=F32) + fq - fk_ref[h:h + 1, :past]
        s_new = lax.dot_general(qs[h], kn, NT_DIMS, preferred_element_type=F32) + fq \
            - fk_ref[h:h + 1, past:past + ls]
        s_new = jnp.where(cc <= rr, s_new, -jnp.inf)
        m = jnp.maximum(jnp.max(s_old, axis=-1, keepdims=True), jnp.max(s_new, axis=-1, keepdims=True))
        p_old = jnp.exp(s_old - m)
        p_new = jnp.exp(s_new - m)
        l = jnp.sum(p_old, axis=-1, keepdims=True) + jnp.sum(p_new, axis=-1, keepdims=True)
        o = (jnp.dot(p_old.astype(BF16), vp, preferred_element_type=F32)
             + jnp.dot(p_new.astype(BF16), vn, preferred_element_type=F32))
        outs.append(o / l)
    o_ref[...] = jnp.where(lane < DH, outs[0], outs[1]).astype(BF16)


def _fox_t_kernel(q_ref, k_ref, v_ref, fqr_ref, fkc_ref, fmin_ref, o_ref,
                  m_ref, l_ref, acc_ref, vt_ref, fkb_ref, kabs_ref, s_ref, p_ref, *, tq, tk, n_kt):
    assert tq == tk
    i = pl.program_id(2)
    lane, qs = _head_pair_queries(q_ref)
    qpos0 = i * tq
    rep = tq // LANES

    @pl.when(i == 0)
    def _():
        kabs_ref[...] = jnp.broadcast_to(jnp.max(jnp.abs(k_ref[...]), axis=0, keepdims=True), kabs_ref.shape)
        er = lax.broadcasted_iota(jnp.int32, (LANES, LANES), 0)
        ec = lax.broadcasted_iota(jnp.int32, (LANES, LANES), 1)
        eye = _ones_where(er == ec)
        for h in range(2):
            fkb_ref[h] = jnp.broadcast_to(fkc_ref[:, h:h + 1], fkb_ref.shape[1:])

        def transpose_tile(j, c):
            vj = v_ref[pl.ds(pl.multiple_of(j * tk, tk), tk), :]
            vt_ref[j] = lax.dot_general(eye, vj, NT_DIMS, preferred_element_type=F32).astype(BF16)
            return c

        lax.fori_loop(0, n_kt, transpose_tile, 0)

    m_ref[...] = jnp.full_like(m_ref, M_INIT)
    l_ref[...] = jnp.zeros_like(l_ref)
    acc_ref[...] = jnp.zeros_like(acc_ref)
    kabs = kabs_ref[...]
    slack = [lax.dot_general(kabs, jnp.abs(qs[h]), NT_DIMS, preferred_element_type=F32)[0:1, :]
             + fqr_ref[h:h + 1, :] for h in range(2)]

    def worth_visiting(j):
        jc = jnp.maximum(j, 0)
        gap = jnp.maximum(jnp.max(slack[0] - m_ref[0] - fmin_ref[jc, 0:1, :]),
                          jnp.max(slack[1] - m_ref[1] - fmin_ref[jc, 1:2, :]))
        return gap > -EXP_ZERO

    def scores(j):
        kj = k_ref[pl.ds(pl.multiple_of(j * tk, tk), tk), :]
        return [lax.dot_general(kj, qs[h], NT_DIMS, preferred_element_type=F32) for h in range(2)]

    def weights(j, s, h, vis):
        fk = fkb_ref[h, pl.ds(pl.multiple_of(j * tk, tk), tk), :]
        s = s + fqr_ref[h:h + 1, :] - jnp.concatenate([fk] * rep, axis=1)
        if vis is not None:
            s = jnp.where(vis, s, -jnp.inf)
        m_prev = m_ref[h]
        m_new = jnp.maximum(m_prev, jnp.max(s, axis=0, keepdims=True))
        alpha = jnp.exp(m_prev - m_new)
        p = jnp.exp(s - m_new)
        l_ref[h] = alpha * l_ref[h] + jnp.sum(p, axis=0, keepdims=True)
        m_ref[h] = m_new
        p_ref[h] = p.astype(BF16)
        return alpha

    def weighted_values(j, h):
        return jnp.dot(vt_ref[j, h * DH:(h + 1) * DH, :], p_ref[h], preferred_element_type=F32)

    def stage_scores(slot, j):
        nxt = scores(jnp.maximum(j, 0))
        s_ref[slot, 0] = nxt[0]
        s_ref[slot, 1] = nxt[1]

    first = scores(i)
    stage_scores(1, i - 1)
    kpos = i * tk + lax.broadcasted_iota(jnp.int32, (tk, tq), 0)
    qpos = qpos0 + lax.broadcasted_iota(jnp.int32, (tk, tq), 1)
    for h in range(2):
        weights(i, first[h], h, kpos <= qpos)

    def more(c):
        return c[1]

    def trip(c):
        t, _ = c
        j = i - t
        slot = t % 2
        for h in range(2):
            rows = slice(h * DH, (h + 1) * DH)
            pv = weighted_values(j + 1, h)
            alpha = weights(j, s_ref[slot, h], h, None)
            acc_ref[rows, :] = alpha * (acc_ref[rows, :] + pv)
        stage_scores(1 - slot, j - 1)
        return t + 1, jnp.logical_and(j >= 1, worth_visiting(j - 1))

    visited, _ = lax.while_loop(more, trip, (1, jnp.logical_and(i >= 1, worth_visiting(i - 1))))
    j_fin = i - (visited - 1)
    for h in range(2):
        rows = slice(h * DH, (h + 1) * DH)
        acc_ref[rows, :] = acc_ref[rows, :] + weighted_values(j_fin, h)
    o_t = jnp.concatenate([acc_ref[0:DH, :] / l_ref[0], acc_ref[DH:, :] / l_ref[1]], axis=0).astype(BF16)
    qr = lax.broadcasted_iota(jnp.int32, (tq, tq), 0)
    qc = lax.broadcasted_iota(jnp.int32, (tq, tq), 1)
    o_ref[...] = lax.dot_general(_ones_where(qr == qc), o_t, NT_DIMS,
                                 preferred_element_type=F32).astype(BF16)


def _fox_prompt(p16, fqr, fkc, fmin, *, n_seq, n_qt, lk, q_col_block0, k_col_block0, v_col_block0):
    tq = tk = SEQ_TILE
    n_hg = H_ATT // 2
    n_kt = lk // tk
    return pl.pallas_call(
        functools.partial(_fox_t_kernel, tq=tq, tk=tk, n_kt=n_kt),
        name="fox_attention_t",
        grid=(n_seq, n_hg, n_qt),
        in_specs=[
            pl.BlockSpec((tq, LANES), lambda b, g, i: (b * n_qt + i, q_col_block0 + g)),
            pl.BlockSpec((lk, LANES), lambda b, g, i: (b, k_col_block0 + g)),
            pl.BlockSpec((lk, LANES), lambda b, g, i: (b, v_col_block0 + g)),
            pl.BlockSpec((None, None, 2, tq), lambda b, g, i: (b, g, 0, i)),
            pl.BlockSpec((None, None, lk, 2), lambda b, g, i: (b, g, 0, 0)),
            pl.BlockSpec((None, None, n_kt, 2, 1), lambda b, g, i: (b, g, 0, 0, 0)),
        ],
        out_specs=pl.BlockSpec((tq, LANES), lambda b, g, i: (b * n_qt + i, g)),
        out_shape=jax.ShapeDtypeStruct((n_seq * n_qt * tq, D_ATT), BF16),
        scratch_shapes=[pltpu.VMEM((2, 1, tq), F32), pltpu.VMEM((2, 1, tq), F32), pltpu.VMEM((LANES, tq), F32),
                        pltpu.VMEM((n_kt, LANES, tk), BF16), pltpu.VMEM((2, lk, LANES), F32),
                        pltpu.VMEM((2 * SUBLANES, LANES), BF16),
                        pltpu.VMEM((2, 2, tk, tq), F32), pltpu.VMEM((2, tk, tq), BF16)],
        compiler_params=_cparams(("parallel", "parallel", "arbitrary")),
    )(p16, p16, p16, fqr, fkc, fmin)


def _sb_weights(z, tri, vis, run):
    sp = jnp.maximum(z, 0.0) + jnp.log(1.0 + jnp.exp(-jnp.abs(z)))
    log_keep = -sp
    if vis is not None:
        log_keep = jnp.where(vis, log_keep, 0.0)
    hi = log_keep.astype(BF16)
    lo = (log_keep - hi.astype(F32)).astype(BF16)
    within = jnp.dot(hi, tri, preferred_element_type=F32) + jnp.dot(lo, tri, preferred_element_type=F32)
    wgt = jnp.exp((z - sp) + (within + run))
    if vis is not None:
        wgt = jnp.where(vis, wgt, 0.0)
    return wgt.astype(BF16), run + within[:, 0:1] + log_keep[:, 0:1]


def _strict_upper_ones(n):
    rr = lax.broadcasted_iota(jnp.int32, (n, n), 0)
    cc = lax.broadcasted_iota(jnp.int32, (n, n), 1)
    return _ones_where(rr > cc), rr, cc


def _sb_kernel(q_ref, k_ref, v_ref, o_ref, r_ref, acc_ref, z_ref, w_ref, *, tq, tk):
    assert tq == tk
    i = pl.program_id(2)
    lane, qs = _head_pair_queries(q_ref)
    r_ref[...] = jnp.zeros_like(r_ref)
    acc_ref[...] = jnp.zeros_like(acc_ref)
    tri, rr, cc = _strict_upper_ones(tk)

    def logits(j):
        kj = k_ref[pl.ds(pl.multiple_of(j * tk, tk), tk), :]
        return [lax.dot_general(qs[h], kj, NT_DIMS, preferred_element_type=F32) for h in range(2)]

    def stage_logits(slot, j):
        nxt = logits(jnp.maximum(j, 0))
        z_ref[slot, 0] = nxt[0]
        z_ref[slot, 1] = nxt[1]

    def weights(z, h, vis):
        w_ref[h], r_ref[h] = _sb_weights(z, tri, vis, r_ref[h])

    def weighted_values(j, h):
        vj = v_ref[pl.ds(pl.multiple_of(j * tk, tk), tk), :]
        return jnp.dot(w_ref[h], vj, preferred_element_type=F32)

    def worth_visiting():
        return jnp.max(jnp.maximum(r_ref[0], r_ref[1])) > -EXP_ZERO

    first = logits(i)
    stage_logits(1, i - 1)
    for h in range(2):
        weights(first[h], h, cc < rr)

    def more(c):
        return c[1]

    def trip(c):
        t, _ = c
        j = i - t
        slot = t % 2
        for h in range(2):
            acc_ref[h] += weighted_values(j + 1, h)
            weights(z_ref[slot, h], h, None)
        stage_logits(1 - slot, j - 1)
        return t + 1, jnp.logical_and(j >= 1, worth_visiting())

    visited, _ = lax.while_loop(more, trip, (1, jnp.logical_and(i >= 1, worth_visiting())))
    j_fin = i - (visited - 1)
    o_ref[...] = jnp.where(lane < DH, acc_ref[0] + weighted_values(j_fin, 0),
                           acc_ref[1] + weighted_values(j_fin, 1)).astype(BF16)


def _sb_prompt(c16, *, n_seq, n_qt, lk, q_col_block0, k_col_block0, v_col_block0):
    tq = tk = SEQ_TILE
    n_hg = H_ATT // 2
    return pl.pallas_call(
        functools.partial(_sb_kernel, tq=tq, tk=tk),
        name="sb_attention",
        grid=(n_seq, n_hg, n_qt),
        in_specs=[
            pl.BlockSpec((tq, LANES), lambda b, g, i: (b * n_qt + i, q_col_block0 + g)),
            pl.BlockSpec((lk, LANES), lambda b, g, i: (b, k_col_block0 + g)),
            pl.BlockSpec((lk, LANES), lambda b, g, i: (b, v_col_block0 + g)),
        ],
        out_specs=pl.BlockSpec((tq, LANES), lambda b, g, i: (b * n_qt + i, g)),
        out_shape=jax.ShapeDtypeStruct((n_seq * n_qt * tq, D_ATT), BF16),
        scratch_shapes=[pltpu.VMEM((2, tq, 1), F32), pltpu.VMEM((2, tq, LANES), F32),
                        pltpu.VMEM((2, 2, tq, tk), F32), pltpu.VMEM((2, tq, tk), BF16)],
        compiler_params=_cparams(("parallel", "parallel", "arbitrary")),
    )(c16, c16, c16)


def _sb_sample_kernel(q_ref, kn_ref, vn_ref, kp_ref, vp_ref, o_ref, r_ref, acc_ref, *, ls, past, tk):
    lane, qs = _head_pair_queries(q_ref)
    tri_new, rr, cc = _strict_upper_ones(ls)
    tri, _, _ = _strict_upper_ones(tk)
    kn = kn_ref[...]
    vn = vn_ref[...]
    for h in range(2):
        z = lax.dot_general(qs[h], kn, NT_DIMS, preferred_element_type=F32)
        wgt, r_ref[h] = _sb_weights(z, tri_new, cc < rr, jnp.zeros((ls, 1), F32))
        acc_ref[h] = jnp.dot(wgt, vn, preferred_element_type=F32)

    def worth_visiting():
        return jnp.max(jnp.maximum(r_ref[0], r_ref[1])) > -EXP_ZERO

    n_tiles = past // tk

    def more(c):
        t, go = c
        return jnp.logical_and(t < n_tiles, go)

    def trip(c):
        t, _ = c
        rows = pl.ds(pl.multiple_of((n_tiles - 1 - t) * tk, tk), tk)
        kj = kp_ref[rows, :].astype(BF16)
        vj = vp_ref[rows, :].astype(BF16)
        for h in range(2):
            z = lax.dot_general(qs[h], kj, NT_DIMS, preferred_element_type=F32)
            wgt, r_ref[h] = _sb_weights(z, tri, None, r_ref[h])
            acc_ref[h] += jnp.dot(wgt, vj, preferred_element_type=F32)
        return t + 1, worth_visiting()

    lax.while_loop(more, trip, (0, worth_visiting()))
    o_ref[...] = jnp.where(lane < DH, acc_ref[0], acc_ref[1]).astype(BF16)


def _sample_attention(kind, proj16, past_k, past_v, fq=None, fk=None, *, n_seq, ls, row_block0,
                      q_col_block0, k_col_block0, v_col_block0):
    n_hg = H_ATT // 2
    past = past_k.shape[1]
    row = lambda col0: (lambda b, g: (row_block0 + b, col0 + g))
    new_spec = lambda col0: pl.BlockSpec((ls, LANES), row(col0))
    past_spec = pl.BlockSpec((None, past, LANES), lambda b, g: (b, 0, g))
    in_specs = [new_spec(q_col_block0), new_spec(k_col_block0), new_spec(v_col_block0), past_spec, past_spec]
    args = [proj16, proj16, proj16, past_k, past_v]
    if kind == "fox":
        in_specs += [pl.BlockSpec((None, None, ls, 2), lambda b, g: (b, g, 0, 0)),
                     pl.BlockSpec((None, None, 2, fk.shape[-1]), lambda b, g: (b, g, 0, 0))]
        args += [fq, fk]
        kern = functools.partial(_fox_sample_kernel, ls=ls, past=past)
        scratch = []
    else:
        assert past % SEQ_TILE == 0
        kern = functools.partial(_sb_sample_kernel, ls=ls, past=past, tk=SEQ_TILE)
        scratch = [pltpu.VMEM((2, ls, 1), F32), pltpu.VMEM((2, ls, LANES), F32)]
    return pl.pallas_call(
        kern,
        name=kind + "_attention_sample",
        grid=(n_seq, n_hg),
        in_specs=in_specs,
        out_specs=pl.BlockSpec((ls, LANES), lambda b, g: (b, g)),
        out_shape=jax.ShapeDtypeStruct((n_seq * ls, D_ATT), BF16),
        scratch_shapes=scratch,
        compiler_params=_cparams(("parallel", "parallel")),
    )(*args)


def _ssd_kernel(z_ref, xbc_ref, dt_ref, cst_ref, s0_ref, cw_ref, cb_ref, alog_ref, dsk_ref, ng_ref,
                y_ref, sout_ref, cout_ref, xw_ref, s_ref, ysc_ref, *, q, l_valid, n_chunks):
    c = pl.program_id(1)
    hist = CONV_W - 1
    base = SUBLANES

    @pl.when(c == 0)
    def _():
        xw_ref[base - hist:base, :] = cst_ref[...]
        s_ref[...] = s0_ref[...]

    xw_ref[base:base + q, :] = xbc_ref[...]
    conv = cb_ref[...]
    for w in range(CONV_W):
        conv = conv + xw_ref[base - hist + w:base - hist + w + q, :] * cw_ref[w:w + 1, :]
    last = l_valid - 1

    @pl.when(c == last // q)
    def _():
        lr = base + last % q
        cout_ref[...] = xw_ref[lr - hist + 1:lr + 1, :]

    xw_ref[base - hist:base, :] = xw_ref[base + q - hist:base + q, :]

    act = conv * _sigmoid(conv)
    xs = act[:, :D_SSM]
    gw = G_SSM * N_SSM
    bm = act[:, D_SSM:D_SSM + gw].astype(BF16)
    cm = act[:, D_SSM + gw:].astype(BF16)

    rowg = c * q + lax.broadcasted_iota(jnp.int32, (q, LANES), 0)
    dt = jnp.where(rowg < l_valid, dt_ref[...], 0.0)
    adt = dt * (-jnp.exp(alog_ref[...]))
    rr = lax.broadcasted_iota(jnp.int32, (q, q), 0)
    cc = lax.broadcasted_iota(jnp.int32, (q, q), 1)
    causal = rr >= cc
    acs = _split3_dot(_ones_where(causal), adt)
    er = lax.broadcasted_iota(jnp.int32, (LANES, LANES), 0)
    ec = lax.broadcasted_iota(jnp.int32, (LANES, LANES), 1)
    eye = _ones_where(er == ec)
    acs_t = _split3_transpose(eye, acs)
    lane = lax.broadcasted_iota(jnp.int32, (1, LANES), 1)
    first = lane < P_SSM
    srow_first = lax.broadcasted_iota(jnp.int32, (LANES, 1), 0) < P_SSM
    heads_per_group = H_SSM // G_SSM

    cb_mats = []
    for g in range(G_SSM):
        cg = cm[:, g * N_SSM:(g + 1) * N_SSM]
        bg = bm[:, g * N_SSM:(g + 1) * N_SSM]
        cb_mats.append(lax.dot_general(cg, bg, NT_DIMS, preferred_element_type=F32))

    for pr in range(H_SSM // 2):
        h0, h1 = 2 * pr, 2 * pr + 1
        g = h0 // heads_per_group
        cg = cm[:, g * N_SSM:(g + 1) * N_SSM]
        bg = bm[:, g * N_SSM:(g + 1) * N_SSM]
        cols = slice(pr * LANES, (pr + 1) * LANES)
        xs_p = xs[:, cols]
        a0, a1 = acs[:, h0:h0 + 1], acs[:, h1:h1 + 1]
        xdt = xs_p * jnp.where(first, dt[:, h0:h0 + 1], dt[:, h1:h1 + 1])
        y = jnp.zeros((q, LANES), F32)
        for hh, a_col, keep in ((h0, a0, first), (h1, a1, jnp.logical_not(first))):
            seg = a_col - acs_t[hh:hh + 1, :]
            lmat = jnp.where(causal, jnp.exp(jnp.where(causal, seg, 0.0)), 0.0)
            m = (cb_mats[g] * lmat).astype(BF16)
            y = y + jnp.dot(m, jnp.where(keep, xdt, 0.0).astype(BF16), preferred_element_type=F32)
        st = s_ref[cols, :]
        y_off = lax.dot_general(cg, st.astype(BF16), NT_DIMS, preferred_element_type=F32)
        y = y + y_off * jnp.where(first, jnp.exp(a0), jnp.exp(a1))
        e0, e1 = a0[q - 1:q, :], a1[q - 1:q, :]
        wdec = jnp.where(first, jnp.exp(e0 - a0), jnp.exp(e1 - a1))
        xw = lax.dot_general(eye, (xdt * wdec).astype(BF16), NT_DIMS,
                             preferred_element_type=F32).astype(BF16)
        dec_rows = jnp.where(srow_first, jnp.exp(e0), jnp.exp(e1))
        s_ref[cols, :] = dec_rows * st + jnp.dot(xw, bg, preferred_element_type=F32)
        ysc_ref[:, cols] = y + dsk_ref[:, cols] * xs_p

    zt = z_ref[...]
    yg = ysc_ref[...] * (zt * _sigmoid(zt))
    gwid = D_SSM // G_SSM
    for g in range(G_SSM):
        blk = yg[:, g * gwid:(g + 1) * gwid]
        nrm = blk * lax.rsqrt(jnp.mean(blk * blk, axis=-1, keepdims=True) + EPS)
        y_ref[:, g * gwid:(g + 1) * gwid] = (nrm * ng_ref[:, g * gwid:(g + 1) * gwid]).astype(BF16)

    @pl.when(c == n_chunks - 1)
    def _():
        sout_ref[...] = s_ref[...]


def _ssd(proj32, cst, s0, cw, cb, alog, dsk, ng, *, n_seq, n_chunks, q, l_valid, row_block0,
         z_col, xbc_col, dt_col):
    rmap = lambda col: (lambda b, c: (row_block0 + b * n_chunks + c, col))
    const2 = lambda b, c: (0, 0)
    in_specs = [
        pl.BlockSpec((q, D_SSM), rmap(z_col)),
        pl.BlockSpec((q, CONV_DIM), rmap(xbc_col)),
        pl.BlockSpec((q, LANES), rmap(dt_col)),
        pl.BlockSpec((None, CONV_W - 1, CONV_DIM), lambda b, c: (b, 0, 0)),
        pl.BlockSpec((None, H_SSM * P_SSM, N_SSM), lambda b, c: (b, 0, 0)),
        pl.BlockSpec((CONV_W, CONV_DIM), const2),
        pl.BlockSpec((1, CONV_DIM), const2),
        pl.BlockSpec((1, LANES), const2),
        pl.BlockSpec((1, D_SSM), const2),
        pl.BlockSpec((1, D_SSM), const2),
    ]
    args = [proj32, proj32, proj32, cst, s0, cw, cb, alog, dsk, ng]
    return pl.pallas_call(
        functools.partial(_ssd_kernel, q=q, l_valid=l_valid, n_chunks=n_chunks),
        name="ssd_mixer",
        grid=(n_seq, n_chunks),
        in_specs=in_specs,
        out_specs=[pl.BlockSpec((q, D_SSM), lambda b, c: (b * n_chunks + c, 0)),
                   pl.BlockSpec((None, H_SSM * P_SSM, N_SSM), lambda b, c: (b, 0, 0)),
                   pl.BlockSpec((None, CONV_W - 1, CONV_DIM), lambda b, c: (b, 0, 0))],
        out_shape=[jax.ShapeDtypeStruct((n_seq * n_chunks * q, D_SSM), BF16),
                   jax.ShapeDtypeStruct((n_seq, H_SSM * P_SSM, N_SSM), F32),
                   jax.ShapeDtypeStruct((n_seq, CONV_W - 1, CONV_DIM), F32)],
        scratch_shapes=[pltpu.VMEM((q + SUBLANES, CONV_DIM), F32),
                        pltpu.VMEM((H_SSM * P_SSM, N_SSM), F32),
                        pltpu.VMEM((q, D_SSM), F32)],
        compiler_params=_cparams(("arbitrary", "arbitrary")),
    )(*args)


def _round_up(x, m):
    return (x + m - 1) // m * m


def _pad_cols(w, n):
    return jnp.pad(w, ((0, 0), (0, n - w.shape[1])))


def kernel(x_prompt, x_sample, state_s5_re, state_s5_im, cache_fox_k, cache_fox_v, cache_fox_logf, state_ssd, state_conv, cache_sb_k, cache_sb_v, meta_tokens, norm_ffn1_pre, norm_ffn1_post, norm_mix_pre, norm_mix_post, norm_ffn2_pre, norm_ffn2_post, ffn1_w_gate, ffn1_w_up, ffn1_w_down, ffn2_w_gate, ffn2_w_up, ffn2_w_down, ab_w_in, fox_b_f, s5_a_re, s5_a_im, s5_log_dt, s5_b_re, s5_b_im, s5_c_re, s5_c_im, s5_d, s5_w_glu, s5_b_glu, ab_w_out, cd_w_in, ssd_conv_w, ssd_conv_b, ssd_dt_bias, ssd_a_log, ssd_d, ssd_norm, cd_w_out):
    bp, seq, d = x_prompt.shape
    bs, ls, _ = x_sample.shape
    past = cache_fox_k.shape[2]
    depth = norm_ffn1_pre.shape[0]
    assert d == D_MODEL and depth == 2 and ab_w_in.shape[0] == 1 and cd_w_in.shape[0] == 1
    l0 = N_META + seq
    lp = _round_up(l0, SEQ_TILE)
    n_pt = lp // SEQ_TILE
    rows_s = bs * ls
    assert rows_s % SEQ_TILE == 0 and SEQ_TILE % ls == 0 and ls % 16 == 0 and l0 >= CONV_W
    t = bp * lp + rows_s
    tm = next(c for c in (1024, 512, 256) if t % c == 0)
    tm_mix = next(c for c in (1024, 512, 256) if (bp * lp) % c == 0 and rows_s % c == 0)
    tf = 512
    s_blk0 = bp * n_pt
    lk_s = _round_up(past + ls, SEQ_TILE)

    meta = meta_tokens.astype(F32)
    zpad = jnp.zeros((lp - l0, d), F32)
    pieces = []
    for b in range(bp):
        pieces += [meta, x_prompt[b], zpad]
    pieces.append(x_sample.reshape(rows_s, d))
    h = jnp.concatenate(pieces, axis=0)

    g3 = lambda g: g.reshape(depth, 1, d)
    w16 = lambda w: w.astype(BF16)
    ffn1 = (g3(norm_ffn1_pre), g3(norm_ffn1_post), ffn1_w_gate, ffn1_w_up, ffn1_w_down)
    ffn2 = (g3(norm_ffn2_pre), g3(norm_ffn2_post), ffn2_w_gate, ffn2_w_up, ffn2_w_down)
    g_mix_pre, g_mix_post = g3(norm_mix_pre), g3(norm_mix_post)

    def prompt_rows(x):
        return x[:bp * lp].reshape(bp, lp, -1)[:, :l0]

    def sample_rows(x):
        return x[bp * lp:].reshape(bs, ls, -1)

    h = _ffn_half(h, *ffn1, 0, tm, tf)

    n_main = D_S5 + 3 * D_ATT
    n_ab = n_main + PROJ_TN
    w_ab = w16(_pad_cols(ab_w_in[0], n_ab))
    bias_ab = jnp.zeros((1, PROJ_TN), F32).at[0, :H_ATT].set(fox_b_f[0])
    p32, p16 = _inproj(h, g_mix_pre, w_ab, bias_ab, 0, tm, "log_sigmoid")

    logf = p32[:, n_main:n_main + H_ATT]
    kf = p32[:, D_S5 + D_ATT:D_S5 + 2 * D_ATT]
    vf = p32[:, D_S5 + 2 * D_ATT:n_main]

    tab, bfull, cfull = _s5_prep(s5_a_re[0], s5_a_im[0], s5_log_dt[0], s5_b_re[0], s5_b_im[0],
                                 s5_c_re[0], s5_c_im[0])
    h0_p = jnp.zeros((bp, 1, 2 * S5_W), F32)
    y_s5_p, st_p = _s5_scan(p16, bfull, cfull, tab, h0_p, row_block0=0, n_groups=bp, n_tiles=n_pt,
                            seg=SEQ_TILE, chain=True, l_valid=l0)
    spt = SEQ_TILE // ls
    h0_s = jnp.concatenate([state_s5_re[0].reshape(bs, S5_W), state_s5_im[0].reshape(bs, S5_W)], axis=-1)
    y_s5_s, st_s = _s5_scan(p16, bfull, cfull, tab, h0_s.reshape(bs // spt, spt, 2 * S5_W),
                            row_block0=s_blk0, n_groups=1, n_tiles=rows_s // SEQ_TILE,
                            seg=ls, chain=False, l_valid=ls)

    n_hg = H_ATT // 2

    def head_major(x):
        return jnp.transpose(x, (0, 2, 1))

    logf_p = p32[:bp * lp, n_main:n_main + H_ATT].reshape(bp, lp, H_ATT)
    fcum_p, fmin_p = _time_cumsum(head_major(logf_p))
    fqr_p = fcum_p.reshape(bp, n_hg, 2, lp)
    fkc_p = jnp.transpose(fqr_p, (0, 1, 3, 2))
    fmin_p = jnp.transpose(fmin_p.reshape(n_pt, bp, n_hg, 2, 1), (1, 2, 0, 3, 4))
    qcol = D_S5 // LANES
    att_cols = dict(q_col_block0=qcol, k_col_block0=qcol + D_ATT // LANES, v_col_block0=qcol + 2 * D_ATT // LANES)
    fox_p = _fox_prompt(p16, fqr_p, fkc_p, fmin_p, n_seq=bp, n_qt=n_pt, lk=lp, **att_cols)

    logf_s = jnp.concatenate([cache_fox_logf[0].astype(F32), logf[bp * lp:].reshape(bs, ls, H_ATT),
                              jnp.zeros((bs, lk_s - past - ls, H_ATT), F32)], axis=1)
    fcum_s = _time_cumsum(head_major(logf_s))[0].reshape(bs, n_hg, 2, lk_s)
    fq_s = jnp.transpose(fcum_s[:, :, :, past:past + ls], (0, 1, 3, 2))
    fox_s = _sample_attention("fox", p16, cache_fox_k[0].reshape(bs, past, D_ATT),
                              cache_fox_v[0].reshape(bs, past, D_ATT), fq_s, fcum_s,
                              n_seq=bs, ls=ls, row_block0=bp * lp // ls, **att_cols)

    h = _mixout(h, g_mix_post, 0, w16(ab_w_out[0]), (fox_p, fox_s), tm_mix, ya=(y_s5_p, y_s5_s), u=p32,
                dsk=s5_d[0].reshape(1, D_S5), wglu=w16(s5_w_glu[0]), bglu=s5_b_glu[0].reshape(1, D_S5))
    h = _ffn_half(h, *ffn2, 0, tm, tf)

    def split_state(st):
        return (st[:, :S5_W].reshape(1, -1, G_S5, N_S5), st[:, S5_W:].reshape(1, -1, G_S5, N_S5))

    s5_re_p, s5_im_p = split_state(st_p.reshape(bp, 2 * S5_W))
    s5_re_s, s5_im_s = split_state(st_s.reshape(bs, 2 * S5_W))
    heads = lambda x: x.reshape(1, x.shape[0], x.shape[1], H_ATT, DH)
    fox_k_p, fox_v_p = heads(prompt_rows(kf)), heads(prompt_rows(vf))
    fox_k_s, fox_v_s = heads(sample_rows(kf)), heads(sample_rows(vf))
    fox_logf_p, fox_logf_s = prompt_rows(logf)[None], sample_rows(logf)[None]

    h = _ffn_half(h, *ffn1, 1, tm, tf)

    wc = cd_w_in[0]
    o_xbc, o_dt = D_SSM, D_SSM + CONV_DIM
    o_q = CONV_DIM + D_SSM
    n_main_cd = o_q + 3 * D_ATT
    n_cd = n_main_cd + PROJ_TN
    w_cd = w16(_pad_cols(jnp.concatenate([wc[:, o_xbc:o_dt], wc[:, :o_xbc], wc[:, o_dt + H_SSM:],
                                          wc[:, o_dt:o_dt + H_SSM]], axis=1), n_cd))
    bias_cd = jnp.zeros((1, PROJ_TN), F32).at[0, :H_SSM].set(ssd_dt_bias[0])
    c32, c16 = _inproj(h, g_mix_pre, w_cd, bias_cd, 1, tm, "softplus")
    kc = c32[:, o_q + D_ATT:o_q + 2 * D_ATT]
    vc = c32[:, o_q + 2 * D_ATT:n_main_cd]

    cw = ssd_conv_w[0]
    cb = ssd_conv_b[0].reshape(1, CONV_DIM)
    alog = jnp.zeros((1, LANES), F32).at[0, :H_SSM].set(ssd_a_log[0])
    dsk = jnp.repeat(ssd_d[0], P_SSM).reshape(1, D_SSM)
    ng = ssd_norm[0].reshape(1, D_SSM)
    ssd_cols = dict(z_col=CONV_DIM // D_SSM, xbc_col=0, dt_col=n_main_cd // LANES)
    ssd_out_p, ssd_st_p, conv_p = _ssd(c32, jnp.zeros((bp, CONV_W - 1, CONV_DIM), F32),
                                       jnp.zeros((bp, H_SSM * P_SSM, N_SSM), F32), cw, cb, alog, dsk, ng,
                                       n_seq=bp, n_chunks=n_pt, q=SEQ_TILE, l_valid=l0, row_block0=0, **ssd_cols)
    ssd_out_s, ssd_st_s, conv_s = _ssd(c32, state_conv[0], state_ssd[0].reshape(bs, H_SSM * P_SSM, N_SSM),
                                       cw, cb, alog, dsk, ng, n_seq=bs, n_chunks=1, q=ls, l_valid=ls,
                                       row_block0=bp * lp // ls, **ssd_cols)

    qcol_cd = o_q // LANES
    att_cols = dict(q_col_block0=qcol_cd, k_col_block0=qcol_cd + D_ATT // LANES,
                    v_col_block0=qcol_cd + 2 * D_ATT // LANES)
    sb_p = _sb_prompt(c16, n_seq=bp, n_qt=n_pt, lk=lp, **att_cols)
    sb_s = _sample_attention("sb", c16, cache_sb_k[0].reshape(bs, past, D_ATT),
                             cache_sb_v[0].reshape(bs, past, D_ATT),
                             n_seq=bs, ls=ls, row_block0=bp * lp // ls, **att_cols)

    h = _mixout(h, g_mix_post, 1, w16(cd_w_out[0]), (sb_p, sb_s), tm_mix, ya=(ssd_out_p, ssd_out_s))
    h = _ffn_half(h, *ffn2, 1, tm, tf)

    ssd_p = ssd_st_p.reshape(1, bp, H_SSM, P_SSM, N_SSM)
    ssd_s = ssd_st_s.reshape(1, bs, H_SSM, P_SSM, N_SSM)
    sb_k_p, sb_v_p = heads(prompt_rows(kc)), heads(prompt_rows(vc))
    sb_k_s, sb_v_s = heads(sample_rows(kc)), heads(sample_rows(vc))

    y_prompt = h[:bp * lp].reshape(bp, lp, d)[:, N_META:l0]
    y_sample = h[bp * lp:].reshape(bs, ls, d)
    return (y_prompt, y_sample,
            s5_re_p, s5_im_p, fox_k_p, fox_v_p, fox_logf_p, ssd_p, conv_p[None], sb_k_p, sb_v_p,
            s5_re_s, s5_im_s, fox_k_s, fox_v_s, fox_logf_s, ssd_s, conv_s[None], sb_k_s, sb_v_s)
```

```python
import functools

import jax
import jax.numpy as jnp
from jax import lax
from jax.experimental import pallas as pl
from jax.experimental.pallas import tpu as pltpu

F32 = jnp.float32
BF16 = jnp.bfloat16

EPS = 1e-6
D_MODEL = 1024
DH = 64
N_META = 16
D_S5 = 512
S5_GROUP = 16
G_S5 = 32
N_S5 = 64
S5_W = G_S5 * N_S5
H_ATT = 8
D_ATT = H_ATT * DH
D_SSM = 512
P_SSM = 64
H_SSM = 8
G_SSM = 2
N_SSM = 128
CONV_W = 4
CONV_DIM = D_SSM + 2 * G_SSM * N_SSM

LANES = 128
SUBLANES = 8
SEQ_TILE = 256
FOX_TILE = 384
PROMPT_PAD = 768
PROJ_TN = 256
VMEM_LIMIT = 56 * 1024 * 1024

NT_DIMS = (((1,), (1,)), ((), ()))
EXP_ZERO = 110.0
M_INIT = -1e30


def _cparams(sem):
    return pltpu.CompilerParams(dimension_semantics=sem, vmem_limit_bytes=VMEM_LIMIT)


def _rms(x, g):
    return x * lax.rsqrt(jnp.mean(x * x, axis=-1, keepdims=True) + EPS) * g


def _sigmoid(x):
    return 1.0 / (1.0 + jnp.exp(-x))


def _softplus(x):
    return jnp.maximum(x, 0.0) + jnp.log1p(jnp.exp(-jnp.abs(x)))


def _ones_where(mask):
    return jnp.where(mask, 1.0, 0.0).astype(BF16)


def _split3(x):
    hi = x.astype(BF16)
    r1 = x - hi.astype(F32)
    mid = r1.astype(BF16)
    lo = (r1 - mid.astype(F32)).astype(BF16)
    return hi, mid, lo


def _split3_dot(tri, x):
    return sum(jnp.dot(tri, part, preferred_element_type=F32) for part in _split3(x))


def _split3_transpose(eye, x):
    return sum(lax.dot_general(eye, part, NT_DIMS, preferred_element_type=F32) for part in _split3(x))


def _ffn_kernel(h_ref, gpre_ref, gpost_ref, wg_ref, wu_ref, wd_ref, o_ref, xn_ref, acc_ref, *, nj):
    j = pl.program_id(1)

    @pl.when(j == 0)
    def _():
        xn_ref[...] = _rms(h_ref[...], gpre_ref[...]).astype(BF16)
        acc_ref[...] = jnp.zeros_like(acc_ref)

    xn = xn_ref[...]
    g = jnp.dot(xn, wg_ref[...].astype(BF16), preferred_element_type=F32)
    u = jnp.dot(xn, wu_ref[...].astype(BF16), preferred_element_type=F32)
    a = (g * _sigmoid(g)) * u
    acc_ref[...] += jnp.dot(a.astype(BF16), wd_ref[...].astype(BF16), preferred_element_type=F32)

    @pl.when(j == nj - 1)
    def _():
        o_ref[...] = h_ref[...] + 0.5 * _rms(acc_ref[...], gpost_ref[...])


def _ffn_half(h, gpre, gpost, wg, wu, wd, layer, tm, tf):
    t, d = h.shape
    ff = wg.shape[-1]
    nj = ff // tf
    return pl.pallas_call(
        functools.partial(_ffn_kernel, nj=nj),
        name="ffn_half",
        grid=(t // tm, nj),
        in_specs=[
            pl.BlockSpec((tm, d), lambda i, j: (i, 0)),
            pl.BlockSpec((None, 1, d), lambda i, j: (layer, 0, 0)),
            pl.BlockSpec((None, 1, d), lambda i, j: (layer, 0, 0)),
            pl.BlockSpec((None, d, tf), lambda i, j: (layer, 0, j)),
            pl.BlockSpec((None, d, tf), lambda i, j: (layer, 0, j)),
            pl.BlockSpec((None, tf, d), lambda i, j: (layer, j, 0)),
        ],
        out_specs=pl.BlockSpec((tm, d), lambda i, j: (i, 0)),
        out_shape=jax.ShapeDtypeStruct((t, d), F32),
        scratch_shapes=[pltpu.VMEM((tm, d), BF16), pltpu.VMEM((tm, d), F32)],
        compiler_params=_cparams(("parallel", "arbitrary")),
    )(h, gpre, gpost, wg, wu, wd)


def _inproj_kernel(h_ref, g_ref, w_ref, b_ref, o32_ref, o16_ref, xn_ref, *, nj, tn, tail):
    j = pl.program_id(1)

    @pl.when(j == 0)
    def _():
        xn_ref[...] = _rms(h_ref[...], g_ref[...]).astype(BF16)

    p = jnp.dot(xn_ref[...], w_ref[...], preferred_element_type=F32)
    o32_ref[...] = p
    o16_ref[...] = p.astype(BF16)

    @pl.when(j == nj - 1)
    def _():
        x = p[:, tn - PROJ_TN:] + b_ref[...]
        r = -_softplus(-x) if tail == "log_sigmoid" else _softplus(x)
        o32_ref[:, tn - PROJ_TN:] = r
        o16_ref[:, tn - PROJ_TN:] = r.astype(BF16)


def _inproj(h, g, w, gate_bias, layer, tm, tail):
    t, d = h.shape
    n = w.shape[-1]
    nj = 2
    tn = n // nj
    assert tn % LANES == 0 and tn >= PROJ_TN
    return pl.pallas_call(
        functools.partial(_inproj_kernel, nj=nj, tn=tn, tail=tail),
        name="mix_inproj",
        grid=(t // tm, nj),
        in_specs=[
            pl.BlockSpec((tm, d), lambda i, j: (i, 0)),
            pl.BlockSpec((None, 1, d), lambda i, j: (layer, 0, 0)),
            pl.BlockSpec((d, tn), lambda i, j: (0, j)),
            pl.BlockSpec((1, PROJ_TN), lambda i, j: (0, 0)),
        ],
        out_specs=[pl.BlockSpec((tm, tn), lambda i, j: (i, j)),
                   pl.BlockSpec((tm, tn), lambda i, j: (i, j))],
        out_shape=[jax.ShapeDtypeStruct((t, n), F32), jax.ShapeDtypeStruct((t, n), BF16)],
        scratch_shapes=[pltpu.VMEM((tm, d), BF16)],
        compiler_params=_cparams(("parallel", "arbitrary")),
    )(h, g, w, gate_bias)


def _gelu_tanh(x):
    return 0.5 * x * (1.0 + jnp.tanh(0.7978845608028654 * (x + 0.044715 * (x * x * x))))


def _mixout_kernel(*refs, s5_glu, n_prompt_tiles):
    from_prompt = pl.program_id(0) < n_prompt_tiles
    pick = lambda p_ref, s_ref: jnp.where(from_prompt, p_ref[...], s_ref[...])
    if s5_glu:
        (yap_ref, yas_ref, u_ref, dsk_ref, wglu_ref, bglu_ref, ybp_ref, ybs_ref,
         wo_ref, h_ref, gpost_ref, o_ref) = refs
        y = pick(yap_ref, yas_ref) + dsk_ref[...] * u_ref[...]
        g = _gelu_tanh(y)
        gate = _sigmoid(jnp.dot(g.astype(BF16), wglu_ref[...], preferred_element_type=F32) + bglu_ref[...])
        a = (g * gate).astype(BF16)
    else:
        (yap_ref, yas_ref, ybp_ref, ybs_ref, wo_ref, h_ref, gpost_ref, o_ref) = refs
        a = pick(yap_ref, yas_ref)
    half = a.shape[-1]
    out = (jnp.dot(a, wo_ref[:half, :], preferred_element_type=F32)
           + jnp.dot(pick(ybp_ref, ybs_ref), wo_ref[half:, :], preferred_element_type=F32))
    o_ref[...] = h_ref[...] + _rms(out, gpost_ref[...])


def _mixout(h, gpost, layer, wo, yb, tm, *, ya, u=None, dsk=None, wglu=None, bglu=None):
    t, d = h.shape
    half = yb[0].shape[-1]
    n_p = ya[0].shape[0] // tm
    assert ya[0].shape[0] % tm == 0 and ya[1].shape[0] % tm == 0 and ya[0].shape[0] + ya[1].shape[0] == t
    s5_glu = u is not None
    row = lambda i: (i, 0)
    const = lambda i: (0, 0)
    pair_specs = [pl.BlockSpec((tm, half), lambda i: (jnp.minimum(i, n_p - 1), 0)),
                  pl.BlockSpec((tm, half), lambda i: (jnp.maximum(i - n_p, 0), 0))]
    args, specs = list(ya), list(pair_specs)
    if s5_glu:
        args += [u, dsk, wglu, bglu]
        specs += [pl.BlockSpec((tm, half), row), pl.BlockSpec((1, half), const),
                  pl.BlockSpec((half, half), const), pl.BlockSpec((1, half), const)]
    args += [*yb, wo, h, gpost]
    specs += [*pair_specs, pl.BlockSpec((2 * half, d), const),
              pl.BlockSpec((tm, d), row), pl.BlockSpec((None, 1, d), lambda i: (layer, 0, 0))]
    return pl.pallas_call(
        functools.partial(_mixout_kernel, s5_glu=s5_glu, n_prompt_tiles=n_p),
        name="mix_out",
        grid=(t // tm,),
        in_specs=specs,
        out_specs=pl.BlockSpec((tm, d), row),
        out_shape=jax.ShapeDtypeStruct((t, d), F32),
        compiler_params=_cparams(("parallel",)),
    )(*args)


def _s5_prep_kernel(are_ref, aim_ref, ldt_ref, bre_ref, bim_ref, cre_ref, cim_ref,
                    tab_ref, bf_ref, cf_ref):
    a_re = are_ref[...]
    a_im = aim_ref[...]
    dt = jnp.exp(ldt_ref[...])
    mag = jnp.exp(dt * a_re)
    ab_re = mag * jnp.cos(dt * a_im)
    ab_im = mag * jnp.sin(dt * a_im)
    den = a_re * a_re + a_im * a_im
    nr = ab_re - 1.0
    coef_re = (nr * a_re + ab_im * a_im) / den
    coef_im = (ab_im * a_re - nr * a_im) / den

    pw_re, pw_im = [ab_re], [ab_im]
    for _ in range(SUBLANES - 1):
        pr, pi = pw_re[-1], pw_im[-1]
        pw_re.append(pr * ab_re - pi * ab_im)
        pw_im.append(pr * ab_im + pi * ab_re)
    sub = lax.broadcasted_iota(jnp.int32, (SUBLANES, S5_W), 0)
    for k in range(3):
        sh = 1 << k
        tab_ref[k] = jnp.where(sub >= sh, pw_re[sh - 1], 0.0)
        tab_ref[3 + k] = jnp.where(sub >= sh, pw_im[sh - 1], 0.0)
    q_re = jnp.zeros((SUBLANES, S5_W), F32)
    q_im = jnp.zeros((SUBLANES, S5_W), F32)
    for r in range(SUBLANES):
        q_re = jnp.where(sub == r, pw_re[r], q_re)
        q_im = jnp.where(sub == r, pw_im[r], q_im)
    tab_ref[6] = q_re
    tab_ref[7] = q_im

    rg = lax.broadcasted_iota(jnp.int32, (D_S5, S5_W), 0) // S5_GROUP
    cg = lax.broadcasted_iota(jnp.int32, (D_S5, S5_W), 1) // N_S5
    b_re = bre_ref[...]
    b_im = bim_ref[...]
    bb_re = coef_re * b_re - coef_im * b_im
    bb_im = coef_re * b_im + coef_im * b_re
    bf_ref[:, :S5_W] = jnp.where(rg == cg, bb_re, 0.0).astype(BF16)
    bf_ref[:, S5_W:] = jnp.where(rg == cg, bb_im, 0.0).astype(BF16)
    rg2 = lax.broadcasted_iota(jnp.int32, (S5_W, D_S5), 0) // N_S5
    cg2 = lax.broadcasted_iota(jnp.int32, (S5_W, D_S5), 1) // S5_GROUP
    cf_ref[:S5_W, :] = jnp.where(rg2 == cg2, cre_ref[...], 0.0).astype(BF16)
    cf_ref[S5_W:, :] = jnp.where(rg2 == cg2, -cim_ref[...], 0.0).astype(BF16)


def _s5_prep(a_re, a_im, log_dt, b_re, b_im, c_re, c_im):
    flat = lambda x: x.reshape(1, S5_W)
    ldt = jnp.broadcast_to(log_dt[:, None], (G_S5, N_S5))
    b_t = lambda b: jnp.tile(jnp.transpose(b, (2, 0, 1)).reshape(S5_GROUP, S5_W), (G_S5, 1))
    c_t = lambda c: jnp.tile(jnp.transpose(c, (0, 2, 1)).reshape(S5_W, S5_GROUP), (1, G_S5))
    return pl.pallas_call(
        _s5_prep_kernel,
        name="s5_prep",
        out_shape=[jax.ShapeDtypeStruct((8, SUBLANES, S5_W), F32),
                   jax.ShapeDtypeStruct((D_S5, 2 * S5_W), BF16),
                   jax.ShapeDtypeStruct((2 * S5_W, D_S5), BF16)],
        compiler_params=pltpu.CompilerParams(vmem_limit_bytes=VMEM_LIMIT),
    )(flat(a_re), flat(a_im), flat(ldt), b_t(b_re), b_t(b_im), c_t(c_re), c_t(c_im))


S5_CB = 256
S5_GROUPS_PER_TRIP = 4


def _s5_kernel(u_ref, bf_ref, cf_ref, tab_ref, h0_ref, y_ref, st_ref, bu_ref, car_ref,
               *, rows, seg, chain, last_tile, last_row):
    i = pl.program_id(1)
    w = S5_W
    n_blk = D_S5 // LANES
    sw = w // n_blk
    for m in range(n_blk):
        u_blk = u_ref[:, m * LANES:(m + 1) * LANES]
        for part in (0, w):
            cols = slice(part + m * sw, part + (m + 1) * sw)
            bu_ref[:, cols] = jnp.dot(u_blk, bf_ref[m * LANES:(m + 1) * LANES, cols], preferred_element_type=F32)
    if chain:
        @pl.when(i == 0)
        def _():
            car_ref[...] = h0_ref[...]

    for c in range(0, w, S5_CB):
        re_cols = slice(c, c + S5_CB)
        im_cols = slice(w + c, w + c + S5_CB)
        pr = [tab_ref[k, :, re_cols] for k in range(3)]
        pi = [tab_ref[3 + k, :, re_cols] for k in range(3)]
        qr = tab_ref[6, :, re_cols]
        qi = tab_ref[7, :, re_cols]
        for sg in range(rows // seg):
            src = car_ref if chain else h0_ref
            srow = 0 if chain else sg
            cr0 = src[srow:srow + 1, re_cols]
            ci0 = src[srow:srow + 1, im_cols]

            def body(a, carry, sg=sg, re_cols=re_cols, im_cols=im_cols, pr=pr, pi=pi, qr=qr, qi=qi):
                cr, ci = carry
                starts = [pl.multiple_of(sg * seg + (a * S5_GROUPS_PER_TRIP + u) * SUBLANES, SUBLANES)
                          for u in range(S5_GROUPS_PER_TRIP)]
                xs = [(bu_ref[pl.ds(r0, SUBLANES), re_cols], bu_ref[pl.ds(r0, SUBLANES), im_cols])
                      for r0 in starts]
                scanned = []
                for xr, xi in xs:
                    for k in range(3):
                        sr = pltpu.roll(xr, 1 << k, 0)
                        si = pltpu.roll(xi, 1 << k, 0)
                        xr, xi = xr + pr[k] * sr - pi[k] * si, xi + pr[k] * si + pi[k] * sr
                    scanned.append((xr, xi))
                done = []
                for xr, xi in scanned:
                    xr, xi = xr + qr * cr - qi * ci, xi + qr * ci + qi * cr
                    cr, ci = xr[SUBLANES - 1:SUBLANES, :], xi[SUBLANES - 1:SUBLANES, :]
                    done.append((xr, xi))
                for r0, (xr, xi) in zip(starts, done):
                    bu_ref[pl.ds(r0, SUBLANES), re_cols] = xr
                    bu_ref[pl.ds(r0, SUBLANES), im_cols] = xi
                return cr, ci

            cr, ci = lax.fori_loop(0, seg // (SUBLANES * S5_GROUPS_PER_TRIP), body, (cr0, ci0))
            if chain:
                car_ref[0:1, re_cols] = cr
                car_ref[0:1, im_cols] = ci
            else:
                st_ref[sg:sg + 1, re_cols] = cr
                st_ref[sg:sg + 1, im_cols] = ci

    for m in range(n_blk):
        ch = slice(m * LANES, (m + 1) * LANES)
        y_ref[:, ch] = sum(jnp.dot(bu_ref[:, part + m * sw:part + (m + 1) * sw].astype(BF16),
                                   cf_ref[part + m * sw:part + (m + 1) * sw, ch], preferred_element_type=F32)
                           for part in (0, w))
    if chain:
        @pl.when(i == last_tile)
        def _():
            st_ref[...] = bu_ref[last_row:last_row + 1, :]


def _s5_scan(proj16, bfull, cfull, tab, h0, *, row_block0, n_groups, n_tiles, seg, chain, l_valid):
    rows = SEQ_TILE
    s = h0.shape[1]
    last = l_valid - 1
    assert seg % (SUBLANES * S5_GROUPS_PER_TRIP) == 0
    kern = functools.partial(_s5_kernel, rows=rows, seg=seg, chain=chain,
                             last_tile=last // rows, last_row=last % rows)
    if chain:
        umap = lambda b, i: (row_block0 + b * n_tiles + i, 0)
        hmap = lambda b, i: (b, 0, 0)
    else:
        umap = lambda b, i: (row_block0 + i, 0)
        hmap = lambda b, i: (i, 0, 0)
    const2 = lambda b, i: (0, 0)
    in_specs = [
        pl.BlockSpec((rows, D_S5), umap),
        pl.BlockSpec((D_S5, 2 * S5_W), const2),
        pl.BlockSpec((2 * S5_W, D_S5), const2),
        pl.BlockSpec((8, SUBLANES, S5_W), lambda b, i: (0, 0, 0)),
        pl.BlockSpec((None, s, 2 * S5_W), hmap),
    ]
    return pl.pallas_call(
        kern,
        name="s5_scan",
        grid=(n_groups, n_tiles),
        in_specs=in_specs,
        out_specs=[pl.BlockSpec((rows, D_S5), lambda b, i: (b * n_tiles + i, 0)),
                   pl.BlockSpec((None, s, 2 * S5_W), hmap)],
        out_shape=[jax.ShapeDtypeStruct((n_groups * n_tiles * rows, D_S5), F32),
                   jax.ShapeDtypeStruct(h0.shape, F32)],
        scratch_shapes=[pltpu.VMEM((rows, 2 * S5_W), F32), pltpu.VMEM((1, 2 * S5_W), F32)],
        compiler_params=_cparams(("arbitrary", "arbitrary")),
    )(proj16, bfull, cfull, tab, h0)


def _cumsum_kernel(x_ref, o_ref, pmin_ref, car_ref, min_ref, *, tk):
    j = pl.program_id(0)

    @pl.when(j == 0)
    def _():
        car_ref[...] = jnp.zeros_like(car_ref)
        min_ref[...] = jnp.full_like(min_ref, jnp.inf)

    r = lax.broadcasted_iota(jnp.int32, (tk, tk), 0)
    c = lax.broadcasted_iota(jnp.int32, (tk, tk), 1)
    tri = _ones_where(r <= c)
    acc = sum(jnp.dot(part, tri, preferred_element_type=F32) for part in _split3(x_ref[...])) + car_ref[...]
    o_ref[...] = acc
    car_ref[...] = acc[:, tk - 1:tk]
    run_min = jnp.minimum(min_ref[...], jnp.min(acc, axis=-1, keepdims=True))
    min_ref[...] = run_min
    pmin_ref[...] = run_min


def _time_cumsum(x, tk=SEQ_TILE):
    b, hh, length = x.shape
    assert length % tk == 0
    rows = b * hh
    fcum, pmin = pl.pallas_call(
        functools.partial(_cumsum_kernel, tk=tk),
        name="time_cumsum",
        grid=(length // tk,),
        in_specs=[pl.BlockSpec((rows, tk), lambda j: (0, j))],
        out_specs=[pl.BlockSpec((rows, tk), lambda j: (0, j)),
                   pl.BlockSpec((None, rows, 1), lambda j: (j, 0, 0))],
        out_shape=[jax.ShapeDtypeStruct((rows, length), F32),
                   jax.ShapeDtypeStruct((length // tk, rows, 1), F32)],
        scratch_shapes=[pltpu.VMEM((rows, 1), F32), pltpu.VMEM((rows, 1), F32)],
        compiler_params=_cparams(("arbitrary",)),
    )(x.reshape(rows, length))
    return fcum.reshape(b, hh, length), pmin.reshape(length // tk, b, hh, 1)


def _head_pair_queries(q_ref):
    lane = lax.broadcasted_iota(jnp.int32, (1, LANES), 1)
    q = q_ref[...] * jnp.asarray(DH ** -0.5, BF16)
    zero = jnp.zeros_like(q)
    return lane, (jnp.where(lane < DH, q, zero), jnp.where(lane >= DH, q, zero))


def _fox_sample_kernel(q_ref, kn_ref, vn_ref, kp_ref, vp_ref, fq_ref, fk_ref, o_ref, *, ls, past):
    lane, qs = _head_pair_queries(q_ref)
    kp = kp_ref[...].astype(BF16)
    vp = vp_ref[...].astype(BF16)
    kn = kn_ref[...]
    vn = vn_ref[...]
    rr = lax.broadcasted_iota(jnp.int32, (ls, ls), 0)
    cc = lax.broadcasted_iota(jnp.int32, (ls, ls), 1)
    outs = []
    for h in range(2):
        fq = fq_ref[:, h:h + 1]
        s_old = lax.dot_general(qs[h], kp, NT_DIMS, preferred_element_type=F32) + fq - fk_ref[h:h + 1, :past]
        s_new = lax.dot_general(qs[h], kn, NT_DIMS, preferred_element_type=F32) + fq \
            - fk_ref[h:h + 1, past:past + ls]
        s_new = jnp.where(cc <= rr, s_new, -jnp.inf)
        m = jnp.maximum(jnp.max(s_old, axis=-1, keepdims=True), jnp.max(s_new, axis=-1, keepdims=True))
        p_old = jnp.exp(s_old - m)
        p_new = jnp.exp(s_new - m)
        l = jnp.sum(p_old, axis=-1, keepdims=True) + jnp.sum(p_new, axis=-1, keepdims=True)
        o = (jnp.dot(p_old.astype(BF16), vp, preferred_element_type=F32)
             + jnp.dot(p_new.astype(BF16), vn, preferred_element_type=F32))
        outs.append(o / l)
    o_ref[...] = jnp.where(lane < DH, outs[0], outs[1]).astype(BF16)


def _fox_t_kernel(q_ref, k_ref, v_ref, fqr_ref, fkc_ref, fmin_ref, o_ref,
                  m_ref, l_ref, acc_ref, vt_ref, fkb_ref, kabs_ref, s_ref, p_ref, *, tq, tk, n_kt):
    assert tq == tk
    i = pl.program_id(2)
    lane, qs = _head_pair_queries(q_ref)
    qpos0 = i * tq
    rep = tq // LANES

    @pl.when(i == 0)
    def _():
        kabs_ref[...] = jnp.broadcast_to(jnp.max(jnp.abs(k_ref[...]), axis=0, keepdims=True), kabs_ref.shape)
        er = lax.broadcasted_iota(jnp.int32, (LANES, LANES), 0)
        ec = lax.broadcasted_iota(jnp.int32, (LANES, LANES), 1)
        eye = _ones_where(er == ec)
        for h in range(2):
            fkb_ref[h] = jnp.broadcast_to(fkc_ref[:, h:h + 1], fkb_ref.shape[1:])

        def transpose_tile(j, c):
            vj = v_ref[pl.ds(pl.multiple_of(j * tk, tk), tk), :]
            vt_ref[j] = lax.dot_general(eye, vj, NT_DIMS, preferred_element_type=F32).astype(BF16)
            return c

        lax.fori_loop(0, n_kt, transpose_tile, 0)

    m_ref[...] = jnp.full_like(m_ref, M_INIT)
    l_ref[...] = jnp.zeros_like(l_ref)
    acc_ref[...] = jnp.zeros_like(acc_ref)
    kabs = kabs_ref[...]
    slack = [lax.dot_general(kabs, jnp.abs(qs[h]), NT_DIMS, preferred_element_type=F32)[0:1, :]
             + fqr_ref[h:h + 1, :] for h in range(2)]

    def worth_visiting(j):
        jc = jnp.maximum(j, 0)
        gap = jnp.maximum(jnp.max(slack[0] - m_ref[0] - fmin_ref[jc, 0:1, :]),
                          jnp.max(slack[1] - m_ref[1] - fmin_ref[jc, 1:2, :]))
        return gap > -EXP_ZERO

    def scores(j):
        kj = k_ref[pl.ds(pl.multiple_of(j * tk, tk), tk), :]
        return [lax.dot_general(kj, qs[h], NT_DIMS, preferred_element_type=F32) for h in range(2)]

    def weights(j, s, h, vis):
        fk = fkb_ref[h, pl.ds(pl.multiple_of(j * tk, tk), tk), :]
        s = s + fqr_ref[h:h + 1, :] - jnp.concatenate([fk] * rep, axis=1)
        if vis is not None:
            s = jnp.where(vis, s, -jnp.inf)
        m_prev = m_ref[h]
        m_new = jnp.maximum(m_prev, jnp.max(s, axis=0, keepdims=True))
        alpha = jnp.exp(m_prev - m_new)
        p = jnp.exp(s - m_new)
        l_ref[h] = alpha * l_ref[h] + jnp.sum(p, axis=0, keepdims=True)
        m_ref[h] = m_new
        p_ref[h] = p.astype(BF16)
        return alpha

    def weighted_values(j, h):
        return jnp.dot(vt_ref[j, h * DH:(h + 1) * DH, :], p_ref[h], preferred_element_type=F32)

    def stage_scores(slot, j):
        nxt = scores(jnp.maximum(j, 0))
        s_ref[slot, 0] = nxt[0]
        s_ref[slot, 1] = nxt[1]

    first = scores(i)
    stage_scores(1, i - 1)
    kpos = i * tk + lax.broadcasted_iota(jnp.int32, (tk, tq), 0)
    qpos = qpos0 + lax.broadcasted_iota(jnp.int32, (tk, tq), 1)
    for h in range(2):
        weights(i, first[h], h, kpos <= qpos)

    def more(c):
        return c[1]

    def trip(c):
        t, _ = c
        j = i - t
        slot = t % 2
        for h in range(2):
            rows = slice(h * DH, (h + 1) * DH)
            pv = weighted_values(j + 1, h)
            alpha = weights(j, s_ref[slot, h], h, None)
            acc_ref[rows, :] = alpha * (acc_ref[rows, :] + pv)
        stage_scores(1 - slot, j - 1)
        return t + 1, jnp.logical_and(j >= 1, worth_visiting(j - 1))

    visited, _ = lax.while_loop(more, trip, (1, jnp.logical_and(i >= 1, worth_visiting(i - 1))))
    j_fin = i - (visited - 1)
    for h in range(2):
        rows = slice(h * DH, (h + 1) * DH)
        acc_ref[rows, :] = acc_ref[rows, :] + weighted_values(j_fin, h)
    o_t = jnp.concatenate([acc_ref[0:DH, :] / l_ref[0], acc_ref[DH:, :] / l_ref[1]], axis=0).astype(BF16)
    qr = lax.broadcasted_iota(jnp.int32, (tq, tq), 0)
    qc = lax.broadcasted_iota(jnp.int32, (tq, tq), 1)
    o_ref[...] = lax.dot_general(_ones_where(qr == qc), o_t, NT_DIMS,
                                 preferred_element_type=F32).astype(BF16)


def _fox_prompt(p16, fqr, fkc, fmin, *, n_seq, n_qt, lk, q_col_block0, k_col_block0, v_col_block0):
    tq = tk = FOX_TILE
    n_hg = H_ATT // 2
    n_kt = lk // tk
    return pl.pallas_call(
        functools.partial(_fox_t_kernel, tq=tq, tk=tk, n_kt=n_kt),
        name="fox_attention_t",
        grid=(n_seq, n_hg, n_qt),
        in_specs=[
            pl.BlockSpec((tq, LANES), lambda b, g, i: (b * n_qt + i, q_col_block0 + g)),
            pl.BlockSpec((lk, LANES), lambda b, g, i: (b, k_col_block0 + g)),
            pl.BlockSpec((lk, LANES), lambda b, g, i: (b, v_col_block0 + g)),
            pl.BlockSpec((None, None, 2, tq), lambda b, g, i: (b, g, 0, i)),
            pl.BlockSpec((None, None, lk, 2), lambda b, g, i: (b, g, 0, 0)),
            pl.BlockSpec((None, None, n_kt, 2, 1), lambda b, g, i: (b, g, 0, 0, 0)),
        ],
        out_specs=pl.BlockSpec((tq, LANES), lambda b, g, i: (b * n_qt + i, g)),
        out_shape=jax.ShapeDtypeStruct((n_seq * n_qt * tq, D_ATT), BF16),
        scratch_shapes=[pltpu.VMEM((2, 1, tq), F32), pltpu.VMEM((2, 1, tq), F32), pltpu.VMEM((LANES, tq), F32),
                        pltpu.VMEM((n_kt, LANES, tk), BF16), pltpu.VMEM((2, lk, LANES), F32),
                        pltpu.VMEM((2 * SUBLANES, LANES), BF16),
                        pltpu.VMEM((2, 2, tk, tq), F32), pltpu.VMEM((2, tk, tq), BF16)],
        compiler_params=_cparams(("parallel", "parallel", "arbitrary")),
    )(p16, p16, p16, fqr, fkc, fmin)


def _sb_weights(z, tri, vis, run):
    sp = jnp.maximum(z, 0.0) + jnp.log(1.0 + jnp.exp(-jnp.abs(z)))
    log_keep = -sp
    if vis is not None:
        log_keep = jnp.where(vis, log_keep, 0.0)
    hi = log_keep.astype(BF16)
    lo = (log_keep - hi.astype(F32)).astype(BF16)
    within = jnp.dot(hi, tri, preferred_element_type=F32) + jnp.dot(lo, tri, preferred_element_type=F32)
    wgt = jnp.exp((z - sp) + (within + run))
    if vis is not None:
        wgt = jnp.where(vis, wgt, 0.0)
    return wgt.astype(BF16), run + within[:, 0:1] + log_keep[:, 0:1]


def _strict_upper_ones(n):
    rr = lax.broadcasted_iota(jnp.int32, (n, n), 0)
    cc = lax.broadcasted_iota(jnp.int32, (n, n), 1)
    return _ones_where(rr > cc), rr, cc


def _sb_kernel(q_ref, k_ref, v_ref, o_ref, r_ref, acc_ref, z_ref, w_ref, *, tq, tk):
    assert tq == tk
    i = pl.program_id(2)
    lane, qs = _head_pair_queries(q_ref)
    r_ref[...] = jnp.zeros_like(r_ref)
    acc_ref[...] = jnp.zeros_like(acc_ref)
    tri, rr, cc = _strict_upper_ones(tk)

    def logits(j):
        kj = k_ref[pl.ds(pl.multiple_of(j * tk, tk), tk), :]
        return [lax.dot_general(qs[h], kj, NT_DIMS, preferred_element_type=F32) for h in range(2)]

    def stage_logits(slot, j):
        nxt = logits(jnp.maximum(j, 0))
        z_ref[slot, 0] = nxt[0]
        z_ref[slot, 1] = nxt[1]

    def weights(z, h, vis):
        w_ref[h], r_ref[h] = _sb_weights(z, tri, vis, r_ref[h])

    def weighted_values(j, h):
        vj = v_ref[pl.ds(pl.multiple_of(j * tk, tk), tk), :]
        return jnp.dot(w_ref[h], vj, preferred_element_type=F32)

    def worth_visiting():
        return jnp.max(jnp.maximum(r_ref[0], r_ref[1])) > -EXP_ZERO

    first = logits(i)
    stage_logits(1, i - 1)
    for h in range(2):
        weights(first[h], h, cc < rr)

    def more(c):
        return c[1]

    def trip(c):
        t, _ = c
        j = i - t
        slot = t % 2
        for h in range(2):
            acc_ref[h] += weighted_values(j + 1, h)
            weights(z_ref[slot, h], h, None)
        stage_logits(1 - slot, j - 1)
        return t + 1, jnp.logical_and(j >= 1, worth_visiting())

    visited, _ = lax.while_loop(more, trip, (1, jnp.logical_and(i >= 1, worth_visiting())))
    j_fin = i - (visited - 1)
    o_ref[...] = jnp.where(lane < DH, acc_ref[0] + weighted_values(j_fin, 0),
                           acc_ref[1] + weighted_values(j_fin, 1)).astype(BF16)


def _sb_prompt(c16, *, n_seq, n_qt, lk, q_col_block0, k_col_block0, v_col_block0):
    tq = tk = SEQ_TILE
    n_hg = H_ATT // 2
    return pl.pallas_call(
        functools.partial(_sb_kernel, tq=tq, tk=tk),
        name="sb_attention",
        grid=(n_seq, n_hg, n_qt),
        in_specs=[
            pl.BlockSpec((tq, LANES), lambda b, g, i: (b * n_qt + i, q_col_block0 + g)),
            pl.BlockSpec((lk, LANES), lambda b, g, i: (b, k_col_block0 + g)),
            pl.BlockSpec((lk, LANES), lambda b, g, i: (b, v_col_block0 + g)),
        ],
        out_specs=pl.BlockSpec((tq, LANES), lambda b, g, i: (b * n_qt + i, g)),
        out_shape=jax.ShapeDtypeStruct((n_seq * n_qt * tq, D_ATT), BF16),
        scratch_shapes=[pltpu.VMEM((2, tq, 1), F32), pltpu.VMEM((2, tq, LANES), F32),
                        pltpu.VMEM((2, 2, tq, tk), F32), pltpu.VMEM((2, tq, tk), BF16)],
        compiler_params=_cparams(("parallel", "parallel", "arbitrary")),
    )(c16, c16, c16)


def _sb_sample_kernel(q_ref, kn_ref, vn_ref, kp_ref, vp_ref, o_ref, r_ref, acc_ref, *, ls, past, tk):
    lane, qs = _head_pair_queries(q_ref)
    tri_new, rr, cc = _strict_upper_ones(ls)
    tri, _, _ = _strict_upper_ones(tk)
    kn = kn_ref[...]
    vn = vn_ref[...]
    for h in range(2):
        z = lax.dot_general(qs[h], kn, NT_DIMS, preferred_element_type=F32)
        wgt, r_ref[h] = _sb_weights(z, tri_new, cc < rr, jnp.zeros((ls, 1), F32))
        acc_ref[h] = jnp.dot(wgt, vn, preferred_element_type=F32)

    def worth_visiting():
        return jnp.max(jnp.maximum(r_ref[0], r_ref[1])) > -EXP_ZERO

    n_tiles = past // tk

    def more(c):
        t, go = c
        return jnp.logical_and(t < n_tiles, go)

    def trip(c):
        t, _ = c
        rows = pl.ds(pl.multiple_of((n_tiles - 1 - t) * tk, tk), tk)
        kj = kp_ref[rows, :].astype(BF16)
        vj = vp_ref[rows, :].astype(BF16)
        for h in range(2):
            z = lax.dot_general(qs[h], kj, NT_DIMS, preferred_element_type=F32)
            wgt, r_ref[h] = _sb_weights(z, tri, None, r_ref[h])
            acc_ref[h] += jnp.dot(wgt, vj, preferred_element_type=F32)
        return t + 1, worth_visiting()

    lax.while_loop(more, trip, (0, worth_visiting()))
    o_ref[...] = jnp.where(lane < DH, acc_ref[0], acc_ref[1]).astype(BF16)


def _sample_attention(kind, proj16, past_k, past_v, fq=None, fk=None, *, n_seq, ls, row_block0,
                      q_col_block0, k_col_block0, v_col_block0):
    n_hg = H_ATT // 2
    past = past_k.shape[1]
    row = lambda col0: (lambda b, g: (row_block0 + b, col0 + g))
    new_spec = lambda col0: pl.BlockSpec((ls, LANES), row(col0))
    past_spec = pl.BlockSpec((None, past, LANES), lambda b, g: (b, 0, g))
    in_specs = [new_spec(q_col_block0), new_spec(k_col_block0), new_spec(v_col_block0), past_spec, past_spec]
    args = [proj16, proj16, proj16, past_k, past_v]
    if kind == "fox":
        in_specs += [pl.BlockSpec((None, None, ls, 2), lambda b, g: (b, g, 0, 0)),
                     pl.BlockSpec((None, None, 2, fk.shape[-1]), lambda b, g: (b, g, 0, 0))]
        args += [fq, fk]
        kern = functools.partial(_fox_sample_kernel, ls=ls, past=past)
        scratch = []
    else:
        assert past % SEQ_TILE == 0
        kern = functools.partial(_sb_sample_kernel, ls=ls, past=past, tk=SEQ_TILE)
        scratch = [pltpu.VMEM((2, ls, 1), F32), pltpu.VMEM((2, ls, LANES), F32)]
    return pl.pallas_call(
        kern,
        name=kind + "_attention_sample",
        grid=(n_seq, n_hg),
        in_specs=in_specs,
        out_specs=pl.BlockSpec((ls, LANES), lambda b, g: (b, g)),
        out_shape=jax.ShapeDtypeStruct((n_seq * ls, D_ATT), BF16),
        scratch_shapes=scratch,
        compiler_params=_cparams(("parallel", "parallel")),
    )(*args)


def _ssd_kernel(z_ref, xbc_ref, dt_ref, cst_ref, s0_ref, cw_ref, cb_ref, alog_ref, dsk_ref, ng_ref,
                y_ref, sout_ref, cout_ref, xw_ref, s_ref, ysc_ref, *, q, l_valid, n_chunks):
    c = pl.program_id(1)
    hist = CONV_W - 1
    base = SUBLANES

    @pl.when(c == 0)
    def _():
        xw_ref[base - hist:base, :] = cst_ref[...]
        s_ref[...] = s0_ref[...]

    xw_ref[base:base + q, :] = xbc_ref[...]
    conv = cb_ref[...]
    for w in range(CONV_W):
        conv = conv + xw_ref[base - hist + w:base - hist + w + q, :] * cw_ref[w:w + 1, :]
    last = l_valid - 1

    @pl.when(c == last // q)
    def _():
        lr = base + last % q
        cout_ref[...] = xw_ref[lr - hist + 1:lr + 1, :]

    xw_ref[base - hist:base, :] = xw_ref[base + q - hist:base + q, :]

    act = conv * _sigmoid(conv)
    xs = act[:, :D_SSM]
    gw = G_SSM * N_SSM
    bm = act[:, D_SSM:D_SSM + gw].astype(BF16)
    cm = act[:, D_SSM + gw:].astype(BF16)

    rowg = c * q + lax.broadcasted_iota(jnp.int32, (q, LANES), 0)
    dt = jnp.where(rowg < l_valid, dt_ref[...], 0.0)
    adt = dt * (-jnp.exp(alog_ref[...]))
    rr = lax.broadcasted_iota(jnp.int32, (q, q), 0)
    cc = lax.broadcasted_iota(jnp.int32, (q, q), 1)
    causal = rr >= cc
    acs = _split3_dot(_ones_where(causal), adt)
    er = lax.broadcasted_iota(jnp.int32, (LANES, LANES), 0)
    ec = lax.broadcasted_iota(jnp.int32, (LANES, LANES), 1)
    eye = _ones_where(er == ec)
    acs_t = _split3_transpose(eye, acs)
    lane = lax.broadcasted_iota(jnp.int32, (1, LANES), 1)
    first = lane < P_SSM
    srow_first = lax.broadcasted_iota(jnp.int32, (LANES, 1), 0) < P_SSM
    heads_per_group = H_SSM // G_SSM

    cb_mats = []
    for g in range(G_SSM):
        cg = cm[:, g * N_SSM:(g + 1) * N_SSM]
        bg = bm[:, g * N_SSM:(g + 1) * N_SSM]
        cb_mats.append(lax.dot_general(cg, bg, NT_DIMS, preferred_element_type=F32))

    for pr in range(H_SSM // 2):
        h0, h1 = 2 * pr, 2 * pr + 1
        g = h0 // heads_per_group
        cg = cm[:, g * N_SSM:(g + 1) * N_SSM]
        bg = bm[:, g * N_SSM:(g + 1) * N_SSM]
        cols = slice(pr * LANES, (pr + 1) * LANES)
        xs_p = xs[:, cols]
        a0, a1 = acs[:, h0:h0 + 1], acs[:, h1:h1 + 1]
        xdt = xs_p * jnp.where(first, dt[:, h0:h0 + 1], dt[:, h1:h1 + 1])
        y = jnp.zeros((q, LANES), F32)
        for hh, a_col, keep in ((h0, a0, first), (h1, a1, jnp.logical_not(first))):
            seg = a_col - acs_t[hh:hh + 1, :]
            lmat = jnp.where(causal, jnp.exp(jnp.where(causal, seg, 0.0)), 0.0)
            m = (cb_mats[g] * lmat).astype(BF16)
            y = y + jnp.dot(m, jnp.where(keep, xdt, 0.0).astype(BF16), preferred_element_type=F32)
        st = s_ref[cols, :]
        y_off = lax.dot_general(cg, st.astype(BF16), NT_DIMS, preferred_element_type=F32)
        y = y + y_off * jnp.where(first, jnp.exp(a0), jnp.exp(a1))
        e0, e1 = a0[q - 1:q, :], a1[q - 1:q, :]
        wdec = jnp.where(first, jnp.exp(e0 - a0), jnp.exp(e1 - a1))
        xw = lax.dot_general(eye, (xdt * wdec).astype(BF16), NT_DIMS,
                             preferred_element_type=F32).astype(BF16)
        dec_rows = jnp.where(srow_first, jnp.exp(e0), jnp.exp(e1))
        s_ref[cols, :] = dec_rows * st + jnp.dot(xw, bg, preferred_element_type=F32)
        ysc_ref[:, cols] = y + dsk_ref[:, cols] * xs_p

    zt = z_ref[...]
    yg = ysc_ref[...] * (zt * _sigmoid(zt))
    gwid = D_SSM // G_SSM
    for g in range(G_SSM):
        blk = yg[:, g * gwid:(g + 1) * gwid]
        nrm = blk * lax.rsqrt(jnp.mean(blk * blk, axis=-1, keepdims=True) + EPS)
        y_ref[:, g * gwid:(g + 1) * gwid] = (nrm * ng_ref[:, g * gwid:(g + 1) * gwid]).astype(BF16)

    @pl.when(c == n_chunks - 1)
    def _():
        sout_ref[...] = s_ref[...]


def _ssd(proj32, cst, s0, cw, cb, alog, dsk, ng, *, n_seq, n_chunks, q, l_valid, row_block0,
         z_col, xbc_col, dt_col):
    rmap = lambda col: (lambda b, c: (row_block0 + b * n_chunks + c, col))
    const2 = lambda b, c: (0, 0)
    in_specs = [
        pl.BlockSpec((q, D_SSM), rmap(z_col)),
        pl.BlockSpec((q, CONV_DIM), rmap(xbc_col)),
        pl.BlockSpec((q, LANES), rmap(dt_col)),
        pl.BlockSpec((None, CONV_W - 1, CONV_DIM), lambda b, c: (b, 0, 0)),
        pl.BlockSpec((None, H_SSM * P_SSM, N_SSM), lambda b, c: (b, 0, 0)),
        pl.BlockSpec((CONV_W, CONV_DIM), const2),
        pl.BlockSpec((1, CONV_DIM), const2),
        pl.BlockSpec((1, LANES), const2),
        pl.BlockSpec((1, D_SSM), const2),
        pl.BlockSpec((1, D_SSM), const2),
    ]
    args = [proj32, proj32, proj32, cst, s0, cw, cb, alog, dsk, ng]
    return pl.pallas_call(
        functools.partial(_ssd_kernel, q=q, l_valid=l_valid, n_chunks=n_chunks),
        name="ssd_mixer",
        grid=(n_seq, n_chunks),
        in_specs=in_specs,
        out_specs=[pl.BlockSpec((q, D_SSM), lambda b, c: (b * n_chunks + c, 0)),
                   pl.BlockSpec((None, H_SSM * P_SSM, N_SSM), lambda b, c: (b, 0, 0)),
                   pl.BlockSpec((None, CONV_W - 1, CONV_DIM), lambda b, c: (b, 0, 0))],
        out_shape=[jax.ShapeDtypeStruct((n_seq * n_chunks * q, D_SSM), BF16),
                   jax.ShapeDtypeStruct((n_seq, H_SSM * P_SSM, N_SSM), F32),
                   jax.ShapeDtypeStruct((n_seq, CONV_W - 1, CONV_DIM), F32)],
        scratch_shapes=[pltpu.VMEM((q + SUBLANES, CONV_DIM), F32),
                        pltpu.VMEM((H_SSM * P_SSM, N_SSM), F32),
                        pltpu.VMEM((q, D_SSM), F32)],
        compiler_params=_cparams(("arbitrary", "arbitrary")),
    )(*args)


def _round_up(x, m):
    return (x + m - 1) // m * m


def _pad_cols(w, n):
    return jnp.pad(w, ((0, 0), (0, n - w.shape[1])))


def kernel(x_prompt, x_sample, state_s5_re, state_s5_im, cache_fox_k, cache_fox_v, cache_fox_logf, state_ssd, state_conv, cache_sb_k, cache_sb_v, meta_tokens, norm_ffn1_pre, norm_ffn1_post, norm_mix_pre, norm_mix_post, norm_ffn2_pre, norm_ffn2_post, ffn1_w_gate, ffn1_w_up, ffn1_w_down, ffn2_w_gate, ffn2_w_up, ffn2_w_down, ab_w_in, fox_b_f, s5_a_re, s5_a_im, s5_log_dt, s5_b_re, s5_b_im, s5_c_re, s5_c_im, s5_d, s5_w_glu, s5_b_glu, ab_w_out, cd_w_in, ssd_conv_w, ssd_conv_b, ssd_dt_bias, ssd_a_log, ssd_d, ssd_norm, cd_w_out):
    bp, seq, d = x_prompt.shape
    bs, ls, _ = x_sample.shape
    past = cache_fox_k.shape[2]
    depth = norm_ffn1_pre.shape[0]
    assert d == D_MODEL and depth == 2 and ab_w_in.shape[0] == 1 and cd_w_in.shape[0] == 1
    l0 = N_META + seq
    lp = _round_up(l0, PROMPT_PAD)
    n_pt = lp // SEQ_TILE
    n_ft = lp // FOX_TILE
    rows_s = bs * ls
    assert rows_s % SEQ_TILE == 0 and SEQ_TILE % ls == 0 and ls % 16 == 0 and l0 >= CONV_W
    t = bp * lp + rows_s
    tm = next(c for c in (1024, 512, 256) if t % c == 0)
    tm_mix = next(c for c in (1024, 512, 256) if (bp * lp) % c == 0 and rows_s % c == 0)
    tf = 512
    s_blk0 = bp * n_pt
    lk_s = _round_up(past + ls, SEQ_TILE)

    meta = meta_tokens.astype(F32)
    zpad = jnp.zeros((lp - l0, d), F32)
    pieces = []
    for b in range(bp):
        pieces += [meta, x_prompt[b], zpad]
    pieces.append(x_sample.reshape(rows_s, d))
    h = jnp.concatenate(pieces, axis=0)

    g3 = lambda g: g.reshape(depth, 1, d)
    w16 = lambda w: w.astype(BF16)
    ffn1 = (g3(norm_ffn1_pre), g3(norm_ffn1_post), ffn1_w_gate, ffn1_w_up, ffn1_w_down)
    ffn2 = (g3(norm_ffn2_pre), g3(norm_ffn2_post), ffn2_w_gate, ffn2_w_up, ffn2_w_down)
    g_mix_pre, g_mix_post = g3(norm_mix_pre), g3(norm_mix_post)

    def prompt_rows(x):
        return x[:bp * lp].reshape(bp, lp, -1)[:, :l0]

    def sample_rows(x):
        return x[bp * lp:].reshape(bs, ls, -1)

    h = _ffn_half(h, *ffn1, 0, tm, tf)

    n_main = D_S5 + 3 * D_ATT
    n_ab = n_main + PROJ_TN
    w_ab = w16(_pad_cols(ab_w_in[0], n_ab))
    bias_ab = jnp.zeros((1, PROJ_TN), F32).at[0, :H_ATT].set(fox_b_f[0])
    p32, p16 = _inproj(h, g_mix_pre, w_ab, bias_ab, 0, tm, "log_sigmoid")

    logf = p32[:, n_main:n_main + H_ATT]
    kf = p32[:, D_S5 + D_ATT:D_S5 + 2 * D_ATT]
    vf = p32[:, D_S5 + 2 * D_ATT:n_main]

    tab, bfull, cfull = _s5_prep(s5_a_re[0], s5_a_im[0], s5_log_dt[0], s5_b_re[0], s5_b_im[0],
                                 s5_c_re[0], s5_c_im[0])
    h0_p = jnp.zeros((bp, 1, 2 * S5_W), F32)
    y_s5_p, st_p = _s5_scan(p16, bfull, cfull, tab, h0_p, row_block0=0, n_groups=bp, n_tiles=n_pt,
                            seg=SEQ_TILE, chain=True, l_valid=l0)
    spt = SEQ_TILE // ls
    h0_s = jnp.concatenate([state_s5_re[0].reshape(bs, S5_W), state_s5_im[0].reshape(bs, S5_W)], axis=-1)
    y_s5_s, st_s = _s5_scan(p16, bfull, cfull, tab, h0_s.reshape(bs // spt, spt, 2 * S5_W),
                            row_block0=s_blk0, n_groups=1, n_tiles=rows_s // SEQ_TILE,
                            seg=ls, chain=False, l_valid=ls)

    n_hg = H_ATT // 2

    def head_major(x):
        return jnp.transpose(x, (0, 2, 1))

    logf_p = p32[:bp * lp, n_main:n_main + H_ATT].reshape(bp, lp, H_ATT)
    fcum_p, fmin_p = _time_cumsum(head_major(logf_p), FOX_TILE)
    fqr_p = fcum_p.reshape(bp, n_hg, 2, lp)
    fkc_p = jnp.transpose(fqr_p, (0, 1, 3, 2))
    fmin_p = jnp.transpose(fmin_p.reshape(n_ft, bp, n_hg, 2, 1), (1, 2, 0, 3, 4))
    qcol = D_S5 // LANES
    att_cols = dict(q_col_block0=qcol, k_col_block0=qcol + D_ATT // LANES, v_col_block0=qcol + 2 * D_ATT // LANES)
    fox_p = _fox_prompt(p16, fqr_p, fkc_p, fmin_p, n_seq=bp, n_qt=n_ft, lk=lp, **att_cols)

    logf_s = jnp.concatenate([cache_fox_logf[0].astype(F32), logf[bp * lp:].reshape(bs, ls, H_ATT),
                              jnp.zeros((bs, lk_s - past - ls, H_ATT), F32)], axis=1)
    fcum_s = _time_cumsum(head_major(logf_s))[0].reshape(bs, n_hg, 2, lk_s)
    fq_s = jnp.transpose(fcum_s[:, :, :, past:past + ls], (0, 1, 3, 2))
    fox_s = _sample_attention("fox", p16, cache_fox_k[0].reshape(bs, past, D_ATT),
                              cache_fox_v[0].reshape(bs, past, D_ATT), fq_s, fcum_s,
                              n_seq=bs, ls=ls, row_block0=bp * lp // ls, **att_cols)

    h = _mixout(h, g_mix_post, 0, w16(ab_w_out[0]), (fox_p, fox_s), tm_mix, ya=(y_s5_p, y_s5_s), u=p32,
                dsk=s5_d[0].reshape(1, D_S5), wglu=w16(s5_w_glu[0]), bglu=s5_b_glu[0].reshape(1, D_S5))
    h = _ffn_half(h, *ffn2, 0, tm, tf)

    def split_state(st):
        return (st[:, :S5_W].reshape(1, -1, G_S5, N_S5), st[:, S5_W:].reshape(1, -1, G_S5, N_S5))

    s5_re_p, s5_im_p = split_state(st_p.reshape(bp, 2 * S5_W))
    s5_re_s, s5_im_s = split_state(st_s.reshape(bs, 2 * S5_W))
    heads = lambda x: x.reshape(1, x.shape[0], x.shape[1], H_ATT, DH)
    fox_k_p, fox_v_p = heads(prompt_rows(kf)), heads(prompt_rows(vf))
    fox_k_s, fox_v_s = heads(sample_rows(kf)), heads(sample_rows(vf))
    fox_logf_p, fox_logf_s = prompt_rows(logf)[None], sample_rows(logf)[None]

    h = _ffn_half(h, *ffn1, 1, tm, tf)

    wc = cd_w_in[0]
    o_xbc, o_dt = D_SSM, D_SSM + CONV_DIM
    o_q = CONV_DIM + D_SSM
    n_main_cd = o_q + 3 * D_ATT
    n_cd = n_main_cd + PROJ_TN
    w_cd = w16(_pad_cols(jnp.concatenate([wc[:, o_xbc:o_dt], wc[:, :o_xbc], wc[:, o_dt + H_SSM:],
                                          wc[:, o_dt:o_dt + H_SSM]], axis=1), n_cd))
    bias_cd = jnp.zeros((1, PROJ_TN), F32).at[0, :H_SSM].set(ssd_dt_bias[0])
    c32, c16 = _inproj(h, g_mix_pre, w_cd, bias_cd, 1, tm, "softplus")
    kc = c32[:, o_q + D_ATT:o_q + 2 * D_ATT]
    vc = c32[:, o_q + 2 * D_ATT:n_main_cd]

    cw = ssd_conv_w[0]
    cb = ssd_conv_b[0].reshape(1, CONV_DIM)
    alog = jnp.zeros((1, LANES), F32).at[0, :H_SSM].set(ssd_a_log[0])
    dsk = jnp.repeat(ssd_d[0], P_SSM).reshape(1, D_SSM)
    ng = ssd_norm[0].reshape(1, D_SSM)
    ssd_cols = dict(z_col=CONV_DIM // D_SSM, xbc_col=0, dt_col=n_main_cd // LANES)
    ssd_out_p, ssd_st_p, conv_p = _ssd(c32, jnp.zeros((bp, CONV_W - 1, CONV_DIM), F32),
                                       jnp.zeros((bp, H_SSM * P_SSM, N_SSM), F32), cw, cb, alog, dsk, ng,
                                       n_seq=bp, n_chunks=n_pt, q=SEQ_TILE, l_valid=l0, row_block0=0, **ssd_cols)
    ssd_out_s, ssd_st_s, conv_s = _ssd(c32, state_conv[0], state_ssd[0].reshape(bs, H_SSM * P_SSM, N_SSM),
                                       cw, cb, alog, dsk, ng, n_seq=bs, n_chunks=1, q=ls, l_valid=ls,
                                       row_block0=bp * lp // ls, **ssd_cols)

    qcol_cd = o_q // LANES
    att_cols = dict(q_col_block0=qcol_cd, k_col_block0=qcol_cd + D_ATT // LANES,
                    v_col_block0=qcol_cd + 2 * D_ATT // LANES)
    sb_p = _sb_prompt(c16, n_seq=bp, n_qt=n_pt, lk=lp, **att_cols)
    sb_s = _sample_attention("sb", c16, cache_sb_k[0].reshape(bs, past, D_ATT),
                             cache_sb_v[0].reshape(bs, past, D_ATT),
                             n_seq=bs, ls=ls, row_block0=bp * lp // ls, **att_cols)

    h = _mixout(h, g_mix_post, 1, w16(cd_w_out[0]), (sb_p, sb_s), tm_mix, ya=(ssd_out_p, ssd_out_s))
    h = _ffn_half(h, *ffn2, 1, tm, tf)

    ssd_p = ssd_st_p.reshape(1, bp, H_SSM, P_SSM, N_SSM)
    ssd_s = ssd_st_s.reshape(1, bs, H_SSM, P_SSM, N_SSM)
    sb_k_p, sb_v_p = heads(prompt_rows(kc)), heads(prompt_rows(vc))
    sb_k_s, sb_v_s = heads(sample_rows(kc)), heads(sample_rows(vc))

    y_prompt = h[:bp * lp].reshape(bp, lp, d)[:, N_META:l0]
    y_sample = h[bp * lp:].reshape(bs, ls, d)
    return (y_prompt, y_sample,
            s5_re_p, s5_im_p, fox_k_p, fox_v_p, fox_logf_p, ssd_p, conv_p[None], sb_k_p, sb_v_p,
            s5_re_s, s5_im_s, fox_k_s, fox_v_s, fox_logf_s, ssd_s, conv_s[None], sb_k_s, sb_v_s)
```

```python
import functools

import jax
import jax.numpy as jnp
from jax import lax
from jax.experimental import pallas as pl
from jax.experimental.pallas import tpu as pltpu

F32 = jnp.float32
BF16 = jnp.bfloat16

EPS = 1e-6
D_MODEL = 1024
DH = 64
N_META = 16
D_S5 = 512
S5_GROUP = 16
G_S5 = 32
N_S5 = 64
S5_W = G_S5 * N_S5
H_ATT = 8
D_ATT = H_ATT * DH
D_SSM = 512
P_SSM = 64
H_SSM = 8
G_SSM = 2
N_SSM = 128
CONV_W = 4
CONV_DIM = D_SSM + 2 * G_SSM * N_SSM

LANES = 128
SUBLANES = 8
SEQ_TILE = 256
FOX_TILE = 384
PROMPT_PAD = 768
PROJ_TN = 256
VMEM_LIMIT = 56 * 1024 * 1024

NT_DIMS = (((1,), (1,)), ((), ()))
EXP_ZERO = 110.0
M_INIT = -1e30


def _cparams(sem):
    return pltpu.CompilerParams(dimension_semantics=sem, vmem_limit_bytes=VMEM_LIMIT)


def _rms(x, g):
    return x * lax.rsqrt(jnp.mean(x * x, axis=-1, keepdims=True) + EPS) * g


def _sigmoid(x):
    return 1.0 / (1.0 + jnp.exp(-x))


def _softplus(x):
    return jnp.maximum(x, 0.0) + jnp.log1p(jnp.exp(-jnp.abs(x)))


def _ones_where(mask):
    return jnp.where(mask, 1.0, 0.0).astype(BF16)


def _split3(x):
    hi = x.astype(BF16)
    r1 = x - hi.astype(F32)
    mid = r1.astype(BF16)
    lo = (r1 - mid.astype(F32)).astype(BF16)
    return hi, mid, lo


def _split3_dot(tri, x):
    return sum(jnp.dot(tri, part, preferred_element_type=F32) for part in _split3(x))


def _split3_transpose(eye, x):
    return sum(lax.dot_general(eye, part, NT_DIMS, preferred_element_type=F32) for part in _split3(x))


def _ffn_kernel(h_ref, gpre_ref, gpost_ref, wg_ref, wu_ref, wd_ref, o_ref, xn_ref, acc_ref, *, nj):
    j = pl.program_id(1)

    @pl.when(j == 0)
    def _():
        xn_ref[...] = _rms(h_ref[...], gpre_ref[...]).astype(BF16)
        acc_ref[...] = jnp.zeros_like(acc_ref)

    xn = xn_ref[...]
    g = jnp.dot(xn, wg_ref[...].astype(BF16), preferred_element_type=F32)
    u = jnp.dot(xn, wu_ref[...].astype(BF16), preferred_element_type=F32)
    a = (g * _sigmoid(g)) * u
    acc_ref[...] += jnp.dot(a.astype(BF16), wd_ref[...].astype(BF16), preferred_element_type=F32)

    @pl.when(j == nj - 1)
    def _():
        o_ref[...] = h_ref[...] + 0.5 * _rms(acc_ref[...], gpost_ref[...])


def _ffn_half(h, gpre, gpost, wg, wu, wd, layer, tm, tf):
    t, d = h.shape
    ff = wg.shape[-1]
    nj = ff // tf
    return pl.pallas_call(
        functools.partial(_ffn_kernel, nj=nj),
        name="ffn_half",
        grid=(t // tm, nj),
        in_specs=[
            pl.BlockSpec((tm, d), lambda i, j: (i, 0)),
            pl.BlockSpec((None, 1, d), lambda i, j: (layer, 0, 0)),
            pl.BlockSpec((None, 1, d), lambda i, j: (layer, 0, 0)),
            pl.BlockSpec((None, d, tf), lambda i, j: (layer, 0, j)),
            pl.BlockSpec((None, d, tf), lambda i, j: (layer, 0, j)),
            pl.BlockSpec((None, tf, d), lambda i, j: (layer, j, 0)),
        ],
        out_specs=pl.BlockSpec((tm, d), lambda i, j: (i, 0)),
        out_shape=jax.ShapeDtypeStruct((t, d), F32),
        scratch_shapes=[pltpu.VMEM((tm, d), BF16), pltpu.VMEM((tm, d), F32)],
        compiler_params=_cparams(("parallel", "arbitrary")),
    )(h, gpre, gpost, wg, wu, wd)


def _inproj_kernel(h_ref, g_ref, w_ref, b_ref, o32_ref, o16_ref, xn_ref, *, nj, tn, tail):
    j = pl.program_id(1)

    @pl.when(j == 0)
    def _():
        xn_ref[...] = _rms(h_ref[...], g_ref[...]).astype(BF16)

    p = jnp.dot(xn_ref[...], w_ref[...], preferred_element_type=F32)
    o32_ref[...] = p
    o16_ref[...] = p.astype(BF16)

    @pl.when(j == nj - 1)
    def _():
        x = p[:, tn - PROJ_TN:] + b_ref[...]
        r = -_softplus(-x) if tail == "log_sigmoid" else _softplus(x)
        o32_ref[:, tn - PROJ_TN:] = r
        o16_ref[:, tn - PROJ_TN:] = r.astype(BF16)


def _inproj(h, g, w, gate_bias, layer, tm, tail):
    t, d = h.shape
    n = w.shape[-1]
    nj = 2
    tn = n // nj
    assert tn % LANES == 0 and tn >= PROJ_TN
    return pl.pallas_call(
        functools.partial(_inproj_kernel, nj=nj, tn=tn, tail=tail),
        name="mix_inproj",
        grid=(t // tm, nj),
        in_specs=[
            pl.BlockSpec((tm, d), lambda i, j: (i, 0)),
            pl.BlockSpec((None, 1, d), lambda i, j: (layer, 0, 0)),
            pl.BlockSpec((d, tn), lambda i, j: (0, j)),
            pl.BlockSpec((1, PROJ_TN), lambda i, j: (0, 0)),
        ],
        out_specs=[pl.BlockSpec((tm, tn), lambda i, j: (i, j)),
                   pl.BlockSpec((tm, tn), lambda i, j: (i, j))],
        out_shape=[jax.ShapeDtypeStruct((t, n), F32), jax.ShapeDtypeStruct((t, n), BF16)],
        scratch_shapes=[pltpu.VMEM((tm, d), BF16)],
        compiler_params=_cparams(("parallel", "arbitrary")),
    )(h, g, w, gate_bias)


def _gelu_tanh(x):
    return 0.5 * x * (1.0 + jnp.tanh(0.7978845608028654 * (x + 0.044715 * (x * x * x))))


def _mixout_kernel(*refs, s5_glu, n_prompt_tiles):
    from_prompt = pl.program_id(0) < n_prompt_tiles
    pick = lambda p_ref, s_ref: jnp.where(from_prompt, p_ref[...], s_ref[...])
    if s5_glu:
        (yap_ref, yas_ref, u_ref, dsk_ref, wglu_ref, bglu_ref, ybp_ref, ybs_ref,
         wo_ref, h_ref, gpost_ref, o_ref) = refs
        y = pick(yap_ref, yas_ref) + dsk_ref[...] * u_ref[...]
        g = _gelu_tanh(y)
        gate = _sigmoid(jnp.dot(g.astype(BF16), wglu_ref[...], preferred_element_type=F32) + bglu_ref[...])
        a = (g * gate).astype(BF16)
    else:
        (yap_ref, yas_ref, ybp_ref, ybs_ref, wo_ref, h_ref, gpost_ref, o_ref) = refs
        a = pick(yap_ref, yas_ref)
    half = a.shape[-1]
    out = (jnp.dot(a, wo_ref[:half, :], preferred_element_type=F32)
           + jnp.dot(pick(ybp_ref, ybs_ref), wo_ref[half:, :], preferred_element_type=F32))
    o_ref[...] = h_ref[...] + _rms(out, gpost_ref[...])


def _mixout(h, gpost, layer, wo, yb, tm, *, ya, u=None, dsk=None, wglu=None, bglu=None):
    t, d = h.shape
    half = yb[0].shape[-1]
    n_p = ya[0].shape[0] // tm
    assert ya[0].shape[0] % tm == 0 and ya[1].shape[0] % tm == 0 and ya[0].shape[0] + ya[1].shape[0] == t
    s5_glu = u is not None
    row = lambda i: (i, 0)
    const = lambda i: (0, 0)
    pair_specs = [pl.BlockSpec((tm, half), lambda i: (jnp.minimum(i, n_p - 1), 0)),
                  pl.BlockSpec((tm, half), lambda i: (jnp.maximum(i - n_p, 0), 0))]
    args, specs = list(ya), list(pair_specs)
    if s5_glu:
        args += [u, dsk, wglu, bglu]
        specs += [pl.BlockSpec((tm, half), row), pl.BlockSpec((1, half), const),
                  pl.BlockSpec((half, half), const), pl.BlockSpec((1, half), const)]
    args += [*yb, wo, h, gpost]
    specs += [*pair_specs, pl.BlockSpec((2 * half, d), const),
              pl.BlockSpec((tm, d), row), pl.BlockSpec((None, 1, d), lambda i: (layer, 0, 0))]
    return pl.pallas_call(
        functools.partial(_mixout_kernel, s5_glu=s5_glu, n_prompt_tiles=n_p),
        name="mix_out",
        grid=(t // tm,),
        in_specs=specs,
        out_specs=pl.BlockSpec((tm, d), row),
        out_shape=jax.ShapeDtypeStruct((t, d), F32),
        compiler_params=_cparams(("parallel",)),
    )(*args)


def _s5_prep_kernel(are_ref, aim_ref, ldt_ref, bre_ref, bim_ref, cre_ref, cim_ref,
                    tab_ref, bf_ref, cf_ref):
    a_re = are_ref[...]
    a_im = aim_ref[...]
    dt = jnp.exp(ldt_ref[...])
    mag = jnp.exp(dt * a_re)
    ab_re = mag * jnp.cos(dt * a_im)
    ab_im = mag * jnp.sin(dt * a_im)
    den = a_re * a_re + a_im * a_im
    nr = ab_re - 1.0
    coef_re = (nr * a_re + ab_im * a_im) / den
    coef_im = (ab_im * a_re - nr * a_im) / den

    pw_re, pw_im = [ab_re], [ab_im]
    for _ in range(SUBLANES - 1):
        pr, pi = pw_re[-1], pw_im[-1]
        pw_re.append(pr * ab_re - pi * ab_im)
        pw_im.append(pr * ab_im + pi * ab_re)
    sub = lax.broadcasted_iota(jnp.int32, (SUBLANES, S5_W), 0)
    for k in range(3):
        sh = 1 << k
        tab_ref[k] = jnp.where(sub >= sh, pw_re[sh - 1], 0.0)
        tab_ref[3 + k] = jnp.where(sub >= sh, pw_im[sh - 1], 0.0)
    q_re = jnp.zeros((SUBLANES, S5_W), F32)
    q_im = jnp.zeros((SUBLANES, S5_W), F32)
    for r in range(SUBLANES):
        q_re = jnp.where(sub == r, pw_re[r], q_re)
        q_im = jnp.where(sub == r, pw_im[r], q_im)
    tab_ref[6] = q_re
    tab_ref[7] = q_im

    rg = lax.broadcasted_iota(jnp.int32, (D_S5, S5_W), 0) // S5_GROUP
    cg = lax.broadcasted_iota(jnp.int32, (D_S5, S5_W), 1) // N_S5
    b_re = bre_ref[...]
    b_im = bim_ref[...]
    bb_re = coef_re * b_re - coef_im * b_im
    bb_im = coef_re * b_im + coef_im * b_re
    bf_ref[:, :S5_W] = jnp.where(rg == cg, bb_re, 0.0).astype(BF16)
    bf_ref[:, S5_W:] = jnp.where(rg == cg, bb_im, 0.0).astype(BF16)
    rg2 = lax.broadcasted_iota(jnp.int32, (S5_W, D_S5), 0) // N_S5
    cg2 = lax.broadcasted_iota(jnp.int32, (S5_W, D_S5), 1) // S5_GROUP
    cf_ref[:S5_W, :] = jnp.where(rg2 == cg2, cre_ref[...], 0.0).astype(BF16)
    cf_ref[S5_W:, :] = jnp.where(rg2 == cg2, -cim_ref[...], 0.0).astype(BF16)


def _s5_prep(a_re, a_im, log_dt, b_re, b_im, c_re, c_im):
    flat = lambda x: x.reshape(1, S5_W)
    ldt = jnp.broadcast_to(log_dt[:, None], (G_S5, N_S5))
    b_t = lambda b: jnp.tile(jnp.transpose(b, (2, 0, 1)).reshape(S5_GROUP, S5_W), (G_S5, 1))
    c_t = lambda c: jnp.tile(jnp.transpose(c, (0, 2, 1)).reshape(S5_W, S5_GROUP), (1, G_S5))
    return pl.pallas_call(
        _s5_prep_kernel,
        name="s5_prep",
        out_shape=[jax.ShapeDtypeStruct((8, SUBLANES, S5_W), F32),
                   jax.ShapeDtypeStruct((D_S5, 2 * S5_W), BF16),
                   jax.ShapeDtypeStruct((2 * S5_W, D_S5), BF16)],
        compiler_params=pltpu.CompilerParams(vmem_limit_bytes=VMEM_LIMIT),
    )(flat(a_re), flat(a_im), flat(ldt), b_t(b_re), b_t(b_im), c_t(c_re), c_t(c_im))


S5_CB = 256
S5_GROUPS_PER_TRIP = 4


def _s5_kernel(u_ref, bf_ref, cf_ref, tab_ref, h0_ref, y_ref, st_ref, bu_ref, car_ref,
               *, rows, seg, chain, last_tile, last_row):
    i = pl.program_id(1)
    w = S5_W
    n_blk = D_S5 // LANES
    sw = w // n_blk
    for m in range(n_blk):
        u_blk = u_ref[:, m * LANES:(m + 1) * LANES]
        for part in (0, w):
            cols = slice(part + m * sw, part + (m + 1) * sw)
            bu_ref[:, cols] = jnp.dot(u_blk, bf_ref[m * LANES:(m + 1) * LANES, cols], preferred_element_type=F32)
    if chain:
        @pl.when(i == 0)
        def _():
            car_ref[...] = h0_ref[...]

    for c in range(0, w, S5_CB):
        re_cols = slice(c, c + S5_CB)
        im_cols = slice(w + c, w + c + S5_CB)
        pr = [tab_ref[k, :, re_cols] for k in range(3)]
        pi = [tab_ref[3 + k, :, re_cols] for k in range(3)]
        qr = tab_ref[6, :, re_cols]
        qi = tab_ref[7, :, re_cols]
        for sg in range(rows // seg):
            src = car_ref if chain else h0_ref
            srow = 0 if chain else sg
            cr0 = src[srow:srow + 1, re_cols]
            ci0 = src[srow:srow + 1, im_cols]

            def body(a, carry, sg=sg, re_cols=re_cols, im_cols=im_cols, pr=pr, pi=pi, qr=qr, qi=qi):
                cr, ci = carry
                starts = [pl.multiple_of(sg * seg + (a * S5_GROUPS_PER_TRIP + u) * SUBLANES, SUBLANES)
                          for u in range(S5_GROUPS_PER_TRIP)]
                xs = [(bu_ref[pl.ds(r0, SUBLANES), re_cols], bu_ref[pl.ds(r0, SUBLANES), im_cols])
                      for r0 in starts]
                scanned = []
                for xr, xi in xs:
                    for k in range(3):
                        sr = pltpu.roll(xr, 1 << k, 0)
                        si = pltpu.roll(xi, 1 << k, 0)
                        xr, xi = xr + pr[k] * sr - pi[k] * si, xi + pr[k] * si + pi[k] * sr
                    scanned.append((xr, xi))
                done = []
                for xr, xi in scanned:
                    xr, xi = xr + qr * cr - qi * ci, xi + qr * ci + qi * cr
                    cr, ci = xr[SUBLANES - 1:SUBLANES, :], xi[SUBLANES - 1:SUBLANES, :]
                    done.append((xr, xi))
                for r0, (xr, xi) in zip(starts, done):
                    bu_ref[pl.ds(r0, SUBLANES), re_cols] = xr
                    bu_ref[pl.ds(r0, SUBLANES), im_cols] = xi
                return cr, ci

            cr, ci = lax.fori_loop(0, seg // (SUBLANES * S5_GROUPS_PER_TRIP), body, (cr0, ci0))
            if chain:
                car_ref[0:1, re_cols] = cr
                car_ref[0:1, im_cols] = ci
            else:
                st_ref[sg:sg + 1, re_cols] = cr
                st_ref[sg:sg + 1, im_cols] = ci

    for m in range(n_blk):
        ch = slice(m * LANES, (m + 1) * LANES)
        y_ref[:, ch] = sum(jnp.dot(bu_ref[:, part + m * sw:part + (m + 1) * sw].astype(BF16),
                                   cf_ref[part + m * sw:part + (m + 1) * sw, ch], preferred_element_type=F32)
                           for part in (0, w))
    if chain:
        @pl.when(i == last_tile)
        def _():
            st_ref[...] = bu_ref[last_row:last_row + 1, :]


def _s5_scan(proj16, bfull, cfull, tab, h0, *, row_block0, n_groups, n_tiles, seg, chain, l_valid):
    rows = SEQ_TILE
    s = h0.shape[1]
    last = l_valid - 1
    assert seg % (SUBLANES * S5_GROUPS_PER_TRIP) == 0
    kern = functools.partial(_s5_kernel, rows=rows, seg=seg, chain=chain,
                             last_tile=last // rows, last_row=last % rows)
    if chain:
        umap = lambda b, i: (row_block0 + b * n_tiles + i, 0)
        hmap = lambda b, i: (b, 0, 0)
    else:
        umap = lambda b, i: (row_block0 + i, 0)
        hmap = lambda b, i: (i, 0, 0)
    const2 = lambda b, i: (0, 0)
    in_specs = [
        pl.BlockSpec((rows, D_S5), umap),
        pl.BlockSpec((D_S5, 2 * S5_W), const2),
        pl.BlockSpec((2 * S5_W, D_S5), const2),
        pl.BlockSpec((8, SUBLANES, S5_W), lambda b, i: (0, 0, 0)),
        pl.BlockSpec((None, s, 2 * S5_W), hmap),
    ]
    return pl.pallas_call(
        kern,
        name="s5_scan",
        grid=(n_groups, n_tiles),
        in_specs=in_specs,
        out_specs=[pl.BlockSpec((rows, D_S5), lambda b, i: (b * n_tiles + i, 0)),
                   pl.BlockSpec((None, s, 2 * S5_W), hmap)],
        out_shape=[jax.ShapeDtypeStruct((n_groups * n_tiles * rows, D_S5), F32),
                   jax.ShapeDtypeStruct(h0.shape, F32)],
        scratch_shapes=[pltpu.VMEM((rows, 2 * S5_W), F32), pltpu.VMEM((1, 2 * S5_W), F32)],
        compiler_params=_cparams(("arbitrary", "arbitrary")),
    )(proj16, bfull, cfull, tab, h0)


def _cumsum_kernel(x_ref, o_ref, pmin_ref, car_ref, min_ref, *, tk):
    j = pl.program_id(0)

    @pl.when(j == 0)
    def _():
        car_ref[...] = jnp.zeros_like(car_ref)
        min_ref[...] = jnp.full_like(min_ref, jnp.inf)

    r = lax.broadcasted_iota(jnp.int32, (tk, tk), 0)
    c = lax.broadcasted_iota(jnp.int32, (tk, tk), 1)
    tri = _ones_where(r <= c)
    acc = sum(jnp.dot(part, tri, preferred_element_type=F32) for part in _split3(x_ref[...])) + car_ref[...]
    o_ref[...] = acc
    car_ref[...] = acc[:, tk - 1:tk]
    run_min = jnp.minimum(min_ref[...], jnp.min(acc, axis=-1, keepdims=True))
    min_ref[...] = run_min
    pmin_ref[...] = run_min


def _time_cumsum(x, tk=SEQ_TILE):
    b, hh, length = x.shape
    assert length % tk == 0
    rows = b * hh
    fcum, pmin = pl.pallas_call(
        functools.partial(_cumsum_kernel, tk=tk),
        name="time_cumsum",
        grid=(length // tk,),
        in_specs=[pl.BlockSpec((rows, tk), lambda j: (0, j))],
        out_specs=[pl.BlockSpec((rows, tk), lambda j: (0, j)),
                   pl.BlockSpec((None, rows, 1), lambda j: (j, 0, 0))],
        out_shape=[jax.ShapeDtypeStruct((rows, length), F32),
                   jax.ShapeDtypeStruct((length // tk, rows, 1), F32)],
        scratch_shapes=[pltpu.VMEM((rows, 1), F32), pltpu.VMEM((rows, 1), F32)],
        compiler_params=_cparams(("arbitrary",)),
    )(x.reshape(rows, length))
    return fcum.reshape(b, hh, length), pmin.reshape(length // tk, b, hh, 1)


def _head_pair_queries(q_ref):
    lane = lax.broadcasted_iota(jnp.int32, (1, LANES), 1)
    q = q_ref[...] * jnp.asarray(DH ** -0.5, BF16)
    zero = jnp.zeros_like(q)
    return lane, (jnp.where(lane < DH, q, zero), jnp.where(lane >= DH, q, zero))


def _fox_sample_kernel(q_ref, kn_ref, vn_ref, kp_ref, vp_ref, fq_ref, fk_ref, o_ref, *, ls, past):
    lane, qs = _head_pair_queries(q_ref)
    kp = kp_ref[...].astype(BF16)
    vp = vp_ref[...].astype(BF16)
    kn = kn_ref[...]
    vn = vn_ref[...]
    rr = lax.broadcasted_iota(jnp.int32, (ls, ls), 0)
    cc = lax.broadcasted_iota(jnp.int32, (ls, ls), 1)
    outs = []
    for h in range(2):
        fq = fq_ref[:, h:h + 1]
        s_old = lax.dot_general(qs[h], kp, NT_DIMS, preferred_element_type=F32) + fq - fk_ref[h:h + 1, :past]
        s_new = lax.dot_general(qs[h], kn, NT_DIMS, preferred_element_type=F32) + fq \
            - fk_ref[h:h + 1, past:past + ls]
        s_new = jnp.where(cc <= rr, s_new, -jnp.inf)
        m = jnp.maximum(jnp.max(s_old, axis=-1, keepdims=True), jnp.max(s_new, axis=-1, keepdims=True))
        p_old = jnp.exp(s_old - m)
        p_new = jnp.exp(s_new - m)
        l = jnp.sum(p_old, axis=-1, keepdims=True) + jnp.sum(p_new, axis=-1, keepdims=True)
        o = (jnp.dot(p_old.astype(BF16), vp, preferred_element_type=F32)
             + jnp.dot(p_new.astype(BF16), vn, preferred_element_type=F32))
        outs.append(o / l)
    o_ref[...] = jnp.where(lane < DH, outs[0], outs[1]).astype(BF16)


def _fox_sample_native_kernel(q_ref, kn_ref, vn_ref, kp_ref, vp_ref, fq_ref, fk_ref, o_ref, *, ls, past):
    q = q_ref[...] * jnp.asarray(DH ** -0.5, BF16)
    rr = lax.broadcasted_iota(jnp.int32, (ls, ls), 0)
    cc = lax.broadcasted_iota(jnp.int32, (ls, ls), 1)
    outs = []
    for h in range(H_ATT):
        g, h2 = divmod(h, 2)
        cols = slice(h * DH, (h + 1) * DH)
        qh = q[:, cols]
        kp = kp_ref[pl.ds(h, past, stride=H_ATT), :].astype(BF16)
        vp = vp_ref[pl.ds(h, past, stride=H_ATT), :].astype(BF16)
        fq = fq_ref[g, :, h2:h2 + 1]
        s_old = lax.dot_general(qh, kp, NT_DIMS, preferred_element_type=F32) + fq - fk_ref[g, h2:h2 + 1, :past]
        s_new = lax.dot_general(qh, kn_ref[:, cols], NT_DIMS, preferred_element_type=F32) + fq \
            - fk_ref[g, h2:h2 + 1, past:past + ls]
        s_new = jnp.where(cc <= rr, s_new, -jnp.inf)
        m = jnp.maximum(jnp.max(s_old, axis=-1, keepdims=True), jnp.max(s_new, axis=-1, keepdims=True))
        p_old = jnp.exp(s_old - m)
        p_new = jnp.exp(s_new - m)
        l = jnp.sum(p_old, axis=-1, keepdims=True) + jnp.sum(p_new, axis=-1, keepdims=True)
        o = (jnp.dot(p_old.astype(BF16), vp, preferred_element_type=F32)
             + jnp.dot(p_new.astype(BF16), vn_ref[:, cols], preferred_element_type=F32))
        outs.append(o / l)
    o_ref[...] = jnp.concatenate(outs, axis=1).astype(BF16)


def _fox_sample_native(proj16, past_k, past_v, fq, fk, *, n_seq, ls, row_block0, q_col, k_col, v_col):
    past = past_k.shape[1]
    n_hg = H_ATT // 2
    flat = lambda x: x.reshape(n_seq, past * H_ATT, DH)
    new_spec = lambda col: pl.BlockSpec((ls, D_ATT), lambda b: (row_block0 + b, col))
    past_spec = pl.BlockSpec((None, past * H_ATT, DH), lambda b: (b, 0, 0))
    return pl.pallas_call(
        functools.partial(_fox_sample_native_kernel, ls=ls, past=past),
        name="fox_attention_sample",
        grid=(n_seq,),
        in_specs=[new_spec(q_col), new_spec(k_col), new_spec(v_col), past_spec, past_spec,
                  pl.BlockSpec((None, n_hg, ls, 2), lambda b: (b, 0, 0, 0)),
                  pl.BlockSpec((None, n_hg, 2, fk.shape[-1]), lambda b: (b, 0, 0, 0))],
        out_specs=pl.BlockSpec((ls, D_ATT), lambda b: (b, 0)),
        out_shape=jax.ShapeDtypeStruct((n_seq * ls, D_ATT), BF16),
        compiler_params=_cparams(("parallel",)),
    )(proj16, proj16, proj16, flat(past_k), flat(past_v), fq, fk)


def _fox_t_kernel(q_ref, k_ref, v_ref, fqr_ref, fkc_ref, fmin_ref, o_ref,
                  m_ref, l_ref, acc_ref, vt_ref, fkb_ref, kabs_ref, s_ref, p_ref, *, tq, tk, n_kt):
    assert tq == tk
    i = pl.program_id(2)
    lane, qs = _head_pair_queries(q_ref)
    qpos0 = i * tq
    rep = tq // LANES

    @pl.when(i == 0)
    def _():
        kabs_ref[...] = jnp.broadcast_to(jnp.max(jnp.abs(k_ref[...]), axis=0, keepdims=True), kabs_ref.shape)
        er = lax.broadcasted_iota(jnp.int32, (LANES, LANES), 0)
        ec = lax.broadcasted_iota(jnp.int32, (LANES, LANES), 1)
        eye = _ones_where(er == ec)
        for h in range(2):
            fkb_ref[h] = jnp.broadcast_to(fkc_ref[:, h:h + 1], fkb_ref.shape[1:])

        def transpose_tile(j, c):
            vj = v_ref[pl.ds(pl.multiple_of(j * tk, tk), tk), :]
            vt_ref[j] = lax.dot_general(eye, vj, NT_DIMS, preferred_element_type=F32).astype(BF16)
            return c

        lax.fori_loop(0, n_kt, transpose_tile, 0)

    m_ref[...] = jnp.full_like(m_ref, M_INIT)
    l_ref[...] = jnp.zeros_like(l_ref)
    acc_ref[...] = jnp.zeros_like(acc_ref)
    kabs = kabs_ref[...]
    slack = [lax.dot_general(kabs, jnp.abs(qs[h]), NT_DIMS, preferred_element_type=F32)[0:1, :]
             + fqr_ref[h:h + 1, :] for h in range(2)]

    def worth_visiting(j):
        jc = jnp.maximum(j, 0)
        gap = jnp.maximum(jnp.max(slack[0] - m_ref[0] - fmin_ref[jc, 0:1, :]),
                          jnp.max(slack[1] - m_ref[1] - fmin_ref[jc, 1:2, :]))
        return gap > -EXP_ZERO

    def scores(j):
        kj = k_ref[pl.ds(pl.multiple_of(j * tk, tk), tk), :]
        return [lax.dot_general(kj, qs[h], NT_DIMS, preferred_element_type=F32) for h in range(2)]

    def weights(j, s, h, vis):
        fk = fkb_ref[h, pl.ds(pl.multiple_of(j * tk, tk), tk), :]
        s = s + fqr_ref[h:h + 1, :] - jnp.concatenate([fk] * rep, axis=1)
        if vis is not None:
            s = jnp.where(vis, s, -jnp.inf)
        m_prev = m_ref[h]
        m_new = jnp.maximum(m_prev, jnp.max(s, axis=0, keepdims=True))
        alpha = jnp.exp(m_prev - m_new)
        p = jnp.exp(s - m_new)
        l_ref[h] = alpha * l_ref[h] + jnp.sum(p, axis=0, keepdims=True)
        m_ref[h] = m_new
        p_ref[h] = p.astype(BF16)
        return alpha

    def weighted_values(j, h):
        return jnp.dot(vt_ref[j, h * DH:(h + 1) * DH, :], p_ref[h], preferred_element_type=F32)

    def stage_scores(slot, j):
        nxt = scores(jnp.maximum(j, 0))
        s_ref[slot, 0] = nxt[0]
        s_ref[slot, 1] = nxt[1]

    first = scores(i)
    stage_scores(1, i - 1)
    kpos = i * tk + lax.broadcasted_iota(jnp.int32, (tk, tq), 0)
    qpos = qpos0 + lax.broadcasted_iota(jnp.int32, (tk, tq), 1)
    for h in range(2):
        weights(i, first[h], h, kpos <= qpos)

    def more(c):
        return c[1]

    def trip(c):
        t, _ = c
        j = i - t
        slot = t % 2
        for h in range(2):
            rows = slice(h * DH, (h + 1) * DH)
            pv = weighted_values(j + 1, h)
            alpha = weights(j, s_ref[slot, h], h, None)
            acc_ref[rows, :] = alpha * (acc_ref[rows, :] + pv)
        stage_scores(1 - slot, j - 1)
        return t + 1, jnp.logical_and(j >= 1, worth_visiting(j - 1))

    visited, _ = lax.while_loop(more, trip, (1, jnp.logical_and(i >= 1, worth_visiting(i - 1))))
    j_fin = i - (visited - 1)
    for h in range(2):
        rows = slice(h * DH, (h + 1) * DH)
        acc_ref[rows, :] = acc_ref[rows, :] + weighted_values(j_fin, h)
    o_t = jnp.concatenate([acc_ref[0:DH, :] / l_ref[0], acc_ref[DH:, :] / l_ref[1]], axis=0).astype(BF16)
    qr = lax.broadcasted_iota(jnp.int32, (tq, tq), 0)
    qc = lax.broadcasted_iota(jnp.int32, (tq, tq), 1)
    o_ref[...] = lax.dot_general(_ones_where(qr == qc), o_t, NT_DIMS,
                                 preferred_element_type=F32).astype(BF16)


def _fox_prompt(p16, fqr, fkc, fmin, *, n_seq, n_qt, lk, q_col_block0, k_col_block0, v_col_block0):
    tq = tk = FOX_TILE
    n_hg = H_ATT // 2
    n_kt = lk // tk
    return pl.pallas_call(
        functools.partial(_fox_t_kernel, tq=tq, tk=tk, n_kt=n_kt),
        name="fox_attention_t",
        grid=(n_seq, n_hg, n_qt),
        in_specs=[
            pl.BlockSpec((tq, LANES), lambda b, g, i: (b * n_qt + i, q_col_block0 + g)),
            pl.BlockSpec((lk, LANES), lambda b, g, i: (b, k_col_block0 + g)),
            pl.BlockSpec((lk, LANES), lambda b, g, i: (b, v_col_block0 + g)),
            pl.BlockSpec((None, None, 2, tq), lambda b, g, i: (b, g, 0, i)),
            pl.BlockSpec((None, None, lk, 2), lambda b, g, i: (b, g, 0, 0)),
            pl.BlockSpec((None, None, n_kt, 2, 1), lambda b, g, i: (b, g, 0, 0, 0)),
        ],
        out_specs=pl.BlockSpec((tq, LANES), lambda b, g, i: (b * n_qt + i, g)),
        out_shape=jax.ShapeDtypeStruct((n_seq * n_qt * tq, D_ATT), BF16),
        scratch_shapes=[pltpu.VMEM((2, 1, tq), F32), pltpu.VMEM((2, 1, tq), F32), pltpu.VMEM((LANES, tq), F32),
                        pltpu.VMEM((n_kt, LANES, tk), BF16), pltpu.VMEM((2, lk, LANES), F32),
                        pltpu.VMEM((2 * SUBLANES, LANES), BF16),
                        pltpu.VMEM((2, 2, tk, tq), F32), pltpu.VMEM((2, tk, tq), BF16)],
        compiler_params=_cparams(("parallel", "parallel", "arbitrary")),
    )(p16, p16, p16, fqr, fkc, fmin)


def _sb_weights(z, tri, vis, run):
    sp = jnp.maximum(z, 0.0) + jnp.log(1.0 + jnp.exp(-jnp.abs(z)))
    log_keep = -sp
    if vis is not None:
        log_keep = jnp.where(vis, log_keep, 0.0)
    hi = log_keep.astype(BF16)
    lo = (log_keep - hi.astype(F32)).astype(BF16)
    within = jnp.dot(hi, tri, preferred_element_type=F32) + jnp.dot(lo, tri, preferred_element_type=F32)
    wgt = jnp.exp((z - sp) + (within + run))
    if vis is not None:
        wgt = jnp.where(vis, wgt, 0.0)
    return wgt.astype(BF16), run + within[:, 0:1] + log_keep[:, 0:1]


def _strict_upper_ones(n):
    rr = lax.broadcasted_iota(jnp.int32, (n, n), 0)
    cc = lax.broadcasted_iota(jnp.int32, (n, n), 1)
    return _ones_where(rr > cc), rr, cc


def _sb_kernel(q_ref, k_ref, v_ref, o_ref, r_ref, acc_ref, z_ref, w_ref, *, tq, tk):
    assert tq == tk
    i = pl.program_id(2)
    lane, qs = _head_pair_queries(q_ref)
    r_ref[...] = jnp.zeros_like(r_ref)
    acc_ref[...] = jnp.zeros_like(acc_ref)
    tri, rr, cc = _strict_upper_ones(tk)

    def logits(j):
        kj = k_ref[pl.ds(pl.multiple_of(j * tk, tk), tk), :]
        return [lax.dot_general(qs[h], kj, NT_DIMS, preferred_element_type=F32) for h in range(2)]

    def stage_logits(slot, j):
        nxt = logits(jnp.maximum(j, 0))
        z_ref[slot, 0] = nxt[0]
        z_ref[slot, 1] = nxt[1]

    def weights(z, h, vis):
        w_ref[h], r_ref[h] = _sb_weights(z, tri, vis, r_ref[h])

    def weighted_values(j, h):
        vj = v_ref[pl.ds(pl.multiple_of(j * tk, tk), tk), :]
        return jnp.dot(w_ref[h], vj, preferred_element_type=F32)

    def worth_visiting():
        return jnp.max(jnp.maximum(r_ref[0], r_ref[1])) > -EXP_ZERO

    first = logits(i)
    stage_logits(1, i - 1)
    for h in range(2):
        weights(first[h], h, cc < rr)

    def more(c):
        return c[1]

    def trip(c):
        t, _ = c
        j = i - t
        slot = t % 2
        for h in range(2):
            acc_ref[h] += weighted_values(j + 1, h)
            weights(z_ref[slot, h], h, None)
        stage_logits(1 - slot, j - 1)
        return t + 1, jnp.logical_and(j >= 1, worth_visiting())

    visited, _ = lax.while_loop(more, trip, (1, jnp.logical_and(i >= 1, worth_visiting())))
    j_fin = i - (visited - 1)
    o_ref[...] = jnp.where(lane < DH, acc_ref[0] + weighted_values(j_fin, 0),
                           acc_ref[1] + weighted_values(j_fin, 1)).astype(BF16)


def _sb_prompt(c16, *, n_seq, n_qt, lk, q_col_block0, k_col_block0, v_col_block0):
    tq = tk = SEQ_TILE
    n_hg = H_ATT // 2
    return pl.pallas_call(
        functools.partial(_sb_kernel, tq=tq, tk=tk),
        name="sb_attention",
        grid=(n_seq, n_hg, n_qt),
        in_specs=[
            pl.BlockSpec((tq, LANES), lambda b, g, i: (b * n_qt + i, q_col_block0 + g)),
            pl.BlockSpec((lk, LANES), lambda b, g, i: (b, k_col_block0 + g)),
            pl.BlockSpec((lk, LANES), lambda b, g, i: (b, v_col_block0 + g)),
        ],
        out_specs=pl.BlockSpec((tq, LANES), lambda b, g, i: (b * n_qt + i, g)),
        out_shape=jax.ShapeDtypeStruct((n_seq * n_qt * tq, D_ATT), BF16),
        scratch_shapes=[pltpu.VMEM((2, tq, 1), F32), pltpu.VMEM((2, tq, LANES), F32),
                        pltpu.VMEM((2, 2, tq, tk), F32), pltpu.VMEM((2, tq, tk), BF16)],
        compiler_params=_cparams(("parallel", "parallel", "arbitrary")),
    )(c16, c16, c16)


def _sb_sample_kernel(q_ref, kn_ref, vn_ref, kp_ref, vp_ref, o_ref, r_ref, acc_ref, *, ls, past, tk):
    lane, qs = _head_pair_queries(q_ref)
    tri_new, rr, cc = _strict_upper_ones(ls)
    tri, _, _ = _strict_upper_ones(tk)
    kn = kn_ref[...]
    vn = vn_ref[...]
    for h in range(2):
        z = lax.dot_general(qs[h], kn, NT_DIMS, preferred_element_type=F32)
        wgt, r_ref[h] = _sb_weights(z, tri_new, cc < rr, jnp.zeros((ls, 1), F32))
        acc_ref[h] = jnp.dot(wgt, vn, preferred_element_type=F32)

    def worth_visiting():
        return jnp.max(jnp.maximum(r_ref[0], r_ref[1])) > -EXP_ZERO

    n_tiles = past // tk

    def more(c):
        t, go = c
        return jnp.logical_and(t < n_tiles, go)

    def trip(c):
        t, _ = c
        rows = pl.ds(pl.multiple_of((n_tiles - 1 - t) * tk, tk), tk)
        kj = kp_ref[rows, :].astype(BF16)
        vj = vp_ref[rows, :].astype(BF16)
        for h in range(2):
            z = lax.dot_general(qs[h], kj, NT_DIMS, preferred_element_type=F32)
            wgt, r_ref[h] = _sb_weights(z, tri, None, r_ref[h])
            acc_ref[h] += jnp.dot(wgt, vj, preferred_element_type=F32)
        return t + 1, worth_visiting()

    lax.while_loop(more, trip, (0, worth_visiting()))
    o_ref[...] = jnp.where(lane < DH, acc_ref[0], acc_ref[1]).astype(BF16)


def _sample_attention(kind, proj16, past_k, past_v, fq=None, fk=None, *, n_seq, ls, row_block0,
                      q_col_block0, k_col_block0, v_col_block0):
    n_hg = H_ATT // 2
    past = past_k.shape[1]
    row = lambda col0: (lambda b, g: (row_block0 + b, col0 + g))
    new_spec = lambda col0: pl.BlockSpec((ls, LANES), row(col0))
    past_spec = pl.BlockSpec((None, past, LANES), lambda b, g: (b, 0, g))
    in_specs = [new_spec(q_col_block0), new_spec(k_col_block0), new_spec(v_col_block0), past_spec, past_spec]
    args = [proj16, proj16, proj16, past_k, past_v]
    if kind == "fox":
        in_specs += [pl.BlockSpec((None, None, ls, 2), lambda b, g: (b, g, 0, 0)),
                     pl.BlockSpec((None, None, 2, fk.shape[-1]), lambda b, g: (b, g, 0, 0))]
        args += [fq, fk]
        kern = functools.partial(_fox_sample_kernel, ls=ls, past=past)
        scratch = []
    else:
        assert past % SEQ_TILE == 0
        kern = functools.partial(_sb_sample_kernel, ls=ls, past=past, tk=SEQ_TILE)
        scratch = [pltpu.VMEM((2, ls, 1), F32), pltpu.VMEM((2, ls, LANES), F32)]
    return pl.pallas_call(
        kern,
        name=kind + "_attention_sample",
        grid=(n_seq, n_hg),
        in_specs=in_specs,
        out_specs=pl.BlockSpec((ls, LANES), lambda b, g: (b, g)),
        out_shape=jax.ShapeDtypeStruct((n_seq * ls, D_ATT), BF16),
        scratch_shapes=scratch,
        compiler_params=_cparams(("parallel", "parallel")),
    )(*args)


def _ssd_kernel(z_ref, xbc_ref, dt_ref, cst_ref, s0_ref, cw_ref, cb_ref, alog_ref, dsk_ref, ng_ref,
                y_ref, sout_ref, cout_ref, xw_ref, s_ref, ysc_ref, *, q, l_valid, n_chunks):
    c = pl.program_id(1)
    hist = CONV_W - 1
    base = SUBLANES

    @pl.when(c == 0)
    def _():
        xw_ref[base - hist:base, :] = cst_ref[...]
        s_ref[...] = s0_ref[...]

    xw_ref[base:base + q, :] = xbc_ref[...]
    conv = cb_ref[...]
    for w in range(CONV_W):
        conv = conv + xw_ref[base - hist + w:base - hist + w + q, :] * cw_ref[w:w + 1, :]
    last = l_valid - 1

    @pl.when(c == last // q)
    def _():
        lr = base + last % q
        cout_ref[...] = xw_ref[lr - hist + 1:lr + 1, :]

    xw_ref[base - hist:base, :] = xw_ref[base + q - hist:base + q, :]

    act = conv * _sigmoid(conv)
    xs = act[:, :D_SSM]
    gw = G_SSM * N_SSM
    bm = act[:, D_SSM:D_SSM + gw].astype(BF16)
    cm = act[:, D_SSM + gw:].astype(BF16)

    rowg = c * q + lax.broadcasted_iota(jnp.int32, (q, LANES), 0)
    dt = jnp.where(rowg < l_valid, dt_ref[...], 0.0)
    adt = dt * (-jnp.exp(alog_ref[...]))
    rr = lax.broadcasted_iota(jnp.int32, (q, q), 0)
    cc = lax.broadcasted_iota(jnp.int32, (q, q), 1)
    causal = rr >= cc
    acs = _split3_dot(_ones_where(causal), adt)
    er = lax.broadcasted_iota(jnp.int32, (LANES, LANES), 0)
    ec = lax.broadcasted_iota(jnp.int32, (LANES, LANES), 1)
    eye = _ones_where(er == ec)
    acs_t = _split3_transpose(eye, acs)
    lane = lax.broadcasted_iota(jnp.int32, (1, LANES), 1)
    first = lane < P_SSM
    srow_first = lax.broadcasted_iota(jnp.int32, (LANES, 1), 0) < P_SSM
    heads_per_group = H_SSM // G_SSM

    cb_mats = []
    for g in range(G_SSM):
        cg = cm[:, g * N_SSM:(g + 1) * N_SSM]
        bg = bm[:, g * N_SSM:(g + 1) * N_SSM]
        cb_mats.append(lax.dot_general(cg, bg, NT_DIMS, preferred_element_type=F32))

    for pr in range(H_SSM // 2):
        h0, h1 = 2 * pr, 2 * pr + 1
        g = h0 // heads_per_group
        cg = cm[:, g * N_SSM:(g + 1) * N_SSM]
        bg = bm[:, g * N_SSM:(g + 1) * N_SSM]
        cols = slice(pr * LANES, (pr + 1) * LANES)
        xs_p = xs[:, cols]
        a0, a1 = acs[:, h0:h0 + 1], acs[:, h1:h1 + 1]
        xdt = xs_p * jnp.where(first, dt[:, h0:h0 + 1], dt[:, h1:h1 + 1])
        y = jnp.zeros((q, LANES), F32)
        for hh, a_col, keep in ((h0, a0, first), (h1, a1, jnp.logical_not(first))):
            seg = a_col - acs_t[hh:hh + 1, :]
            lmat = jnp.where(causal, jnp.exp(jnp.where(causal, seg, 0.0)), 0.0)
            m = (cb_mats[g] * lmat).astype(BF16)
            y = y + jnp.dot(m, jnp.where(keep, xdt, 0.0).astype(BF16), preferred_element_type=F32)
        st = s_ref[cols, :]
        y_off = lax.dot_general(cg, st.astype(BF16), NT_DIMS, preferred_element_type=F32)
        y = y + y_off * jnp.where(first, jnp.exp(a0), jnp.exp(a1))
        e0, e1 = a0[q - 1:q, :], a1[q - 1:q, :]
        wdec = jnp.where(first, jnp.exp(e0 - a0), jnp.exp(e1 - a1))
        xw = lax.dot_general(eye, (xdt * wdec).astype(BF16), NT_DIMS,
                             preferred_element_type=F32).astype(BF16)
        dec_rows = jnp.where(srow_first, jnp.exp(e0), jnp.exp(e1))
        s_ref[cols, :] = dec_rows * st + jnp.dot(xw, bg, preferred_element_type=F32)
        ysc_ref[:, cols] = y + dsk_ref[:, cols] * xs_p

    zt = z_ref[...]
    yg = ysc_ref[...] * (zt * _sigmoid(zt))
    gwid = D_SSM // G_SSM
    for g in range(G_SSM):
        blk = yg[:, g * gwid:(g + 1) * gwid]
        nrm = blk * lax.rsqrt(jnp.mean(blk * blk, axis=-1, keepdims=True) + EPS)
        y_ref[:, g * gwid:(g + 1) * gwid] = (nrm * ng_ref[:, g * gwid:(g + 1) * gwid]).astype(BF16)

    @pl.when(c == n_chunks - 1)
    def _():
        sout_ref[...] = s_ref[...]


def _ssd(proj32, cst, s0, cw, cb, alog, dsk, ng, *, n_seq, n_chunks, q, l_valid, row_block0,
         z_col, xbc_col, dt_col):
    rmap = lambda col: (lambda b, c: (row_block0 + b * n_chunks + c, col))
    const2 = lambda b, c: (0, 0)
    in_specs = [
        pl.BlockSpec((q, D_SSM), rmap(z_col)),
        pl.BlockSpec((q, CONV_DIM), rmap(xbc_col)),
        pl.BlockSpec((q, LANES), rmap(dt_col)),
        pl.BlockSpec((None, CONV_W - 1, CONV_DIM), lambda b, c: (b, 0, 0)),
        pl.BlockSpec((None, H_SSM * P_SSM, N_SSM), lambda b, c: (b, 0, 0)),
        pl.BlockSpec((CONV_W, CONV_DIM), const2),
        pl.BlockSpec((1, CONV_DIM), const2),
        pl.BlockSpec((1, LANES), const2),
        pl.BlockSpec((1, D_SSM), const2),
        pl.BlockSpec((1, D_SSM), const2),
    ]
    args = [proj32, proj32, proj32, cst, s0, cw, cb, alog, dsk, ng]
    return pl.pallas_call(
        functools.partial(_ssd_kernel, q=q, l_valid=l_valid, n_chunks=n_chunks),
        name="ssd_mixer",
        grid=(n_seq, n_chunks),
        in_specs=in_specs,
        out_specs=[pl.BlockSpec((q, D_SSM), lambda b, c: (b * n_chunks + c, 0)),
                   pl.BlockSpec((None, H_SSM * P_SSM, N_SSM), lambda b, c: (b, 0, 0)),
                   pl.BlockSpec((None, CONV_W - 1, CONV_DIM), lambda b, c: (b, 0, 0))],
        out_shape=[jax.ShapeDtypeStruct((n_seq * n_chunks * q, D_SSM), BF16),
                   jax.ShapeDtypeStruct((n_seq, H_SSM * P_SSM, N_SSM), F32),
                   jax.ShapeDtypeStruct((n_seq, CONV_W - 1, CONV_DIM), F32)],
        scratch_shapes=[pltpu.VMEM((q + SUBLANES, CONV_DIM), F32),
                        pltpu.VMEM((H_SSM * P_SSM, N_SSM), F32),
                        pltpu.VMEM((q, D_SSM), F32)],
        compiler_params=_cparams(("arbitrary", "arbitrary")),
    )(*args)


def _round_up(x, m):
    return (x + m - 1) // m * m


def _pad_cols(w, n):
    return jnp.pad(w, ((0, 0), (0, n - w.shape[1])))


def kernel(x_prompt, x_sample, state_s5_re, state_s5_im, cache_fox_k, cache_fox_v, cache_fox_logf, state_ssd, state_conv, cache_sb_k, cache_sb_v, meta_tokens, norm_ffn1_pre, norm_ffn1_post, norm_mix_pre, norm_mix_post, norm_ffn2_pre, norm_ffn2_post, ffn1_w_gate, ffn1_w_up, ffn1_w_down, ffn2_w_gate, ffn2_w_up, ffn2_w_down, ab_w_in, fox_b_f, s5_a_re, s5_a_im, s5_log_dt, s5_b_re, s5_b_im, s5_c_re, s5_c_im, s5_d, s5_w_glu, s5_b_glu, ab_w_out, cd_w_in, ssd_conv_w, ssd_conv_b, ssd_dt_bias, ssd_a_log, ssd_d, ssd_norm, cd_w_out):
    bp, seq, d = x_prompt.shape
    bs, ls, _ = x_sample.shape
    past = cache_fox_k.shape[2]
    depth = norm_ffn1_pre.shape[0]
    assert d == D_MODEL and depth == 2 and ab_w_in.shape[0] == 1 and cd_w_in.shape[0] == 1
    l0 = N_META + seq
    lp = _round_up(l0, PROMPT_PAD)
    n_pt = lp // SEQ_TILE
    n_ft = lp // FOX_TILE
    rows_s = bs * ls
    assert rows_s % SEQ_TILE == 0 and SEQ_TILE % ls == 0 and ls % 16 == 0 and l0 >= CONV_W
    t = bp * lp + rows_s
    tm = next(c for c in (1024, 512, 256) if t % c == 0)
    tm_mix = next(c for c in (1024, 512, 256) if (bp * lp) % c == 0 and rows_s % c == 0)
    tf = 512
    s_blk0 = bp * n_pt
    lk_s = _round_up(past + ls, SEQ_TILE)

    meta = meta_tokens.astype(F32)
    zpad = jnp.zeros((lp - l0, d), F32)
    pieces = []
    for b in range(bp):
        pieces += [meta, x_prompt[b], zpad]
    pieces.append(x_sample.reshape(rows_s, d))
    h = jnp.concatenate(pieces, axis=0)

    g3 = lambda g: g.reshape(depth, 1, d)
    w16 = lambda w: w.astype(BF16)
    ffn1 = (g3(norm_ffn1_pre), g3(norm_ffn1_post), ffn1_w_gate, ffn1_w_up, ffn1_w_down)
    ffn2 = (g3(norm_ffn2_pre), g3(norm_ffn2_post), ffn2_w_gate, ffn2_w_up, ffn2_w_down)
    g_mix_pre, g_mix_post = g3(norm_mix_pre), g3(norm_mix_post)

    def prompt_rows(x):
        return x[:bp * lp].reshape(bp, lp, -1)[:, :l0]

    def sample_rows(x):
        return x[bp * lp:].reshape(bs, ls, -1)

    h = _ffn_half(h, *ffn1, 0, tm, tf)

    n_main = D_S5 + 3 * D_ATT
    n_ab = n_main + PROJ_TN
    w_ab = w16(_pad_cols(ab_w_in[0], n_ab))
    bias_ab = jnp.zeros((1, PROJ_TN), F32).at[0, :H_ATT].set(fox_b_f[0])
    p32, p16 = _inproj(h, g_mix_pre, w_ab, bias_ab, 0, tm, "log_sigmoid")

    logf = p32[:, n_main:n_main + H_ATT]
    kf = p32[:, D_S5 + D_ATT:D_S5 + 2 * D_ATT]
    vf = p32[:, D_S5 + 2 * D_ATT:n_main]

    tab, bfull, cfull = _s5_prep(s5_a_re[0], s5_a_im[0], s5_log_dt[0], s5_b_re[0], s5_b_im[0],
                                 s5_c_re[0], s5_c_im[0])
    h0_p = jnp.zeros((bp, 1, 2 * S5_W), F32)
    y_s5_p, st_p = _s5_scan(p16, bfull, cfull, tab, h0_p, row_block0=0, n_groups=bp, n_tiles=n_pt,
                            seg=SEQ_TILE, chain=True, l_valid=l0)
    spt = SEQ_TILE // ls
    h0_s = jnp.concatenate([state_s5_re[0].reshape(bs, S5_W), state_s5_im[0].reshape(bs, S5_W)], axis=-1)
    y_s5_s, st_s = _s5_scan(p16, bfull, cfull, tab, h0_s.reshape(bs // spt, spt, 2 * S5_W),
                            row_block0=s_blk0, n_groups=1, n_tiles=rows_s // SEQ_TILE,
                            seg=ls, chain=False, l_valid=ls)

    n_hg = H_ATT // 2

    def head_major(x):
        return jnp.transpose(x, (0, 2, 1))

    logf_p = p32[:bp * lp, n_main:n_main + H_ATT].reshape(bp, lp, H_ATT)
    fcum_p, fmin_p = _time_cumsum(head_major(logf_p), FOX_TILE)
    fqr_p = fcum_p.reshape(bp, n_hg, 2, lp)
    fkc_p = jnp.transpose(fqr_p, (0, 1, 3, 2))
    fmin_p = jnp.transpose(fmin_p.reshape(n_ft, bp, n_hg, 2, 1), (1, 2, 0, 3, 4))
    qcol = D_S5 // LANES
    att_cols = dict(q_col_block0=qcol, k_col_block0=qcol + D_ATT // LANES, v_col_block0=qcol + 2 * D_ATT // LANES)
    fox_p = _fox_prompt(p16, fqr_p, fkc_p, fmin_p, n_seq=bp, n_qt=n_ft, lk=lp, **att_cols)

    logf_s = jnp.concatenate([cache_fox_logf[0].astype(F32), logf[bp * lp:].reshape(bs, ls, H_ATT),
                              jnp.zeros((bs, lk_s - past - ls, H_ATT), F32)], axis=1)
    fcum_s = _time_cumsum(head_major(logf_s))[0].reshape(bs, n_hg, 2, lk_s)
    fq_s = jnp.transpose(fcum_s[:, :, :, past:past + ls], (0, 1, 3, 2))
    fox_s = _fox_sample_native(p16, cache_fox_k[0], cache_fox_v[0], fq_s, fcum_s, n_seq=bs, ls=ls,
                               row_block0=bp * lp // ls, q_col=D_S5 // D_ATT, k_col=D_S5 // D_ATT + 1,
                               v_col=D_S5 // D_ATT + 2)

    h = _mixout(h, g_mix_post, 0, w16(ab_w_out[0]), (fox_p, fox_s), tm_mix, ya=(y_s5_p, y_s5_s), u=p32,
                dsk=s5_d[0].reshape(1, D_S5), wglu=w16(s5_w_glu[0]), bglu=s5_b_glu[0].reshape(1, D_S5))
    h = _ffn_half(h, *ffn2, 0, tm, tf)

    def split_state(st):
        return (st[:, :S5_W].reshape(1, -1, G_S5, N_S5), st[:, S5_W:].reshape(1, -1, G_S5, N_S5))

    s5_re_p, s5_im_p = split_state(st_p.reshape(bp, 2 * S5_W))
    s5_re_s, s5_im_s = split_state(st_s.reshape(bs, 2 * S5_W))
    heads = lambda x: x.reshape(1, x.shape[0], x.shape[1], H_ATT, DH)
    fox_k_p, fox_v_p = heads(prompt_rows(kf)), heads(prompt_rows(vf))
    fox_k_s, fox_v_s = heads(sample_rows(kf)), heads(sample_rows(vf))
    fox_logf_p, fox_logf_s = prompt_rows(logf)[None], sample_rows(logf)[None]

    h = _ffn_half(h, *ffn1, 1, tm, tf)

    wc = cd_w_in[0]
    o_xbc, o_dt = D_SSM, D_SSM + CONV_DIM
    o_q = CONV_DIM + D_SSM
    n_main_cd = o_q + 3 * D_ATT
    n_cd = n_main_cd + PROJ_TN
    w_cd = w16(_pad_cols(jnp.concatenate([wc[:, o_xbc:o_dt], wc[:, :o_xbc], wc[:, o_dt + H_SSM:],
                                          wc[:, o_dt:o_dt + H_SSM]], axis=1), n_cd))
    bias_cd = jnp.zeros((1, PROJ_TN), F32).at[0, :H_SSM].set(ssd_dt_bias[0])
    c32, c16 = _inproj(h, g_mix_pre, w_cd, bias_cd, 1, tm, "softplus")
    kc = c32[:, o_q + D_ATT:o_q + 2 * D_ATT]
    vc = c32[:, o_q + 2 * D_ATT:n_main_cd]

    cw = ssd_conv_w[0]
    cb = ssd_conv_b[0].reshape(1, CONV_DIM)
    alog = jnp.zeros((1, LANES), F32).at[0, :H_SSM].set(ssd_a_log[0])
    dsk = jnp.repeat(ssd_d[0], P_SSM).reshape(1, D_SSM)
    ng = ssd_norm[0].reshape(1, D_SSM)
    ssd_cols = dict(z_col=CONV_DIM // D_SSM, xbc_col=0, dt_col=n_main_cd // LANES)
    ssd_out_p, ssd_st_p, conv_p = _ssd(c32, jnp.zeros((bp, CONV_W - 1, CONV_DIM), F32),
                                       jnp.zeros((bp, H_SSM * P_SSM, N_SSM), F32), cw, cb, alog, dsk, ng,
                                       n_seq=bp, n_chunks=n_pt, q=SEQ_TILE, l_valid=l0, row_block0=0, **ssd_cols)
    ssd_out_s, ssd_st_s, conv_s = _ssd(c32, state_conv[0], state_ssd[0].reshape(bs, H_SSM * P_SSM, N_SSM),
                                       cw, cb, alog, dsk, ng, n_seq=bs, n_chunks=1, q=ls, l_valid=ls,
                                       row_block0=bp * lp // ls, **ssd_cols)

    qcol_cd = o_q // LANES
    att_cols = dict(q_col_block0=qcol_cd, k_col_block0=qcol_cd + D_ATT // LANES,
                    v_col_block0=qcol_cd + 2 * D_ATT // LANES)
    sb_p = _sb_prompt(c16, n_seq=bp, n_qt=n_pt, lk=lp, **att_cols)
    sb_s = _sample_attention("sb", c16, cache_sb_k[0].reshape(bs, past, D_ATT),
                             cache_sb_v[0].reshape(bs, past, D_ATT),
                             n_seq=bs, ls=ls, row_block0=bp * lp // ls, **att_cols)

    h = _mixout(h, g_mix_post, 1, w16(cd_w_out[0]), (sb_p, sb_s), tm_mix, ya=(ssd_out_p, ssd_out_s))
    h = _ffn_half(h, *ffn2, 1, tm, tf)

    ssd_p = ssd_st_p.reshape(1, bp, H_SSM, P_SSM, N_SSM)
    ssd_s = ssd_st_s.reshape(1, bs, H_SSM, P_SSM, N_SSM)
    sb_k_p, sb_v_p = heads(prompt_rows(kc)), heads(prompt_rows(vc))
    sb_k_s, sb_v_s = heads(sample_rows(kc)), heads(sample_rows(vc))

    y_prompt = h[:bp * lp].reshape(bp, lp, d)[:, N_META:l0]
    y_sample = h[bp * lp:].reshape(bs, ls, d)
    return (y_prompt, y_sample,
            s5_re_p, s5_im_p, fox_k_p, fox_v_p, fox_logf_p, ssd_p, conv_p[None], sb_k_p, sb_v_p,
            s5_re_s, s5_im_s, fox_k_s, fox_v_s, fox_logf_s, ssd_s, conv_s[None], sb_k_s, sb_v_s)
```
